```python
import math
import jax, jax.numpy as jnp
from jax import lax
import numpy as np

D_MODEL = 1024
BATCH = 16
SEQ = 4096
DEPTH = 2

A_PAIRS = ((128, 1), (512, 4), (2048, 16))
A_GROUPS = len(A_PAIRS)
A_HEADS = 4
A_HEAD_DIM = 128
A_GROUP_WIDTH = A_HEADS * A_HEAD_DIM
A_QKV_WIDTH = A_GROUPS * A_GROUP_WIDTH
A_OUT = A_GROUP_WIDTH
A_BLOCK = 128
ROPE_THETA = 500000.0
ROT_DIM = A_HEAD_DIM // 4

B_HEAD_DIM = 128
B_HEADS = D_MODEL // B_HEAD_DIM
B_WIDTH = B_HEADS * B_HEAD_DIM
CONV_K = 4
CHUNK = 64

IN_WIDTHS = (A_QKV_WIDTH, A_QKV_WIDTH, A_QKV_WIDTH,
             3 * B_WIDTH,
             B_WIDTH,
             B_HEADS,
             B_HEADS,
             D_MODEL, D_MODEL)
N_IN = int(sum(IN_WIDTHS))
IN_SPLITS = [int(s) for s in np.cumsum(IN_WIDTHS)[:-1]]

D_FF = 2816
N_EXPERTS = 8
TOP_K = 2
D_EXPERT = 3584
N_DENSE = (DEPTH + 1) // 2
N_MOE = DEPTH // 2

DN_ALPHA = (2 * DEPTH) ** 0.25
DN_BETA = (8 * DEPTH) ** -0.25
LN_EPS = 1e-5
RMS_EPS = 1e-6

kernel_name = "hybrid_dilated_attn_gated_deltanet_moe_deepnorm"


def layer_norm(x, g, b):
    xf = x.astype(jnp.float32)
    mu = xf.mean(-1, keepdims=True)
    var = jnp.square(xf - mu).mean(-1, keepdims=True)
    return ((xf - mu) * lax.rsqrt(var + LN_EPS) * g.astype(jnp.float32) + b.astype(jnp.float32)).astype(x.dtype)


def rotary_tables(positions):
    inv_freq = ROPE_THETA ** (-jnp.arange(0, ROT_DIM, 2, dtype=jnp.float32) / ROT_DIM)
    ang = positions.astype(jnp.float32)[..., None] * inv_freq
    return jnp.cos(ang), jnp.sin(ang)


def apply_partial_rotary(t, cos, sin):
    tr, tp = t[..., :ROT_DIM].astype(jnp.float32), t[..., ROT_DIM:]
    t1, t2 = tr[..., :ROT_DIM // 2], tr[..., ROT_DIM // 2:]
    c, s = cos[:, :, None, None, :], sin[:, :, None, None, :]
    rot = jnp.concatenate([t1 * c - t2 * s, t2 * c + t1 * s], axis=-1).astype(t.dtype)
    return jnp.concatenate([rot, tp], axis=-1)


def dilated_window_attention(q, k, v, window, dilation):
    bsz, seq, nh, dh = q.shape
    span = window // dilation
    sub_len = seq // dilation
    blk = min(A_BLOCK, sub_len)
    n_blk = -(-sub_len // blk)
    pad_len = n_blk * blk

    def to_sub(t):
        return t.reshape(bsz, sub_len, dilation, nh, dh).transpose(0, 2, 3, 1, 4)

    qs = jnp.pad(to_sub(q), ((0, 0),) * 3 + ((0, pad_len - sub_len), (0, 0)))
    kv_pad = ((0, 0),) * 3 + ((span, pad_len - sub_len), (0, 0))
    ks = jnp.pad(to_sub(k), kv_pad)
    vs = jnp.pad(to_sub(v), kv_pad)
    qb = qs.reshape(bsz, dilation, nh, n_blk, blk, dh)
    key_idx = np.arange(n_blk)[:, None] * blk + np.arange(blk + span)[None, :]
    kb = jnp.take(ks, key_idx, axis=3)
    vb = jnp.take(vs, key_idx, axis=3)
    qpos = np.arange(n_blk)[:, None] * blk + np.arange(blk)[None, :]
    kpos = key_idx - span
    dist = qpos[:, :, None] - kpos[:, None, :]
    mask = (dist >= 0) & (dist <= span) & (kpos[:, None, :] >= 0)

    scores = jnp.einsum('brhnqe,brhnke->brhnqk', qb, kb,
                        preferred_element_type=jnp.float32) / math.sqrt(dh)
    scores = jnp.where(mask, scores, -jnp.inf)
    m = scores.max(-1, keepdims=True)
    p = jnp.exp(scores - m)
    den = p.sum(-1, keepdims=True)
    o = jnp.einsum('brhnqk,brhnke->brhnqe', p, vb.astype(jnp.float32)) / den
    lse = (m + jnp.log(den))[..., 0]
    o = o.reshape(bsz, dilation, nh, pad_len, dh)[:, :, :, :sub_len]
    lse = lse.reshape(bsz, dilation, nh, pad_len)[..., :sub_len]
    o = o.transpose(0, 3, 1, 2, 4).reshape(bsz, seq, nh, dh)
    lse = lse.transpose(0, 3, 1, 2).reshape(bsz, seq, nh)
    return o, lse


def mixer_dilated(q, k, v, cos, sin):
    bsz, seq = q.shape[:2]
    q = apply_partial_rotary(q, cos, sin)
    k = apply_partial_rotary(k, cos, sin)
    outs, lses = [], []
    for g, (window, dilation) in enumerate(A_PAIRS):
        o, l = dilated_window_attention(q[:, :, g], k[:, :, g], v[:, :, g], window, dilation)
        outs.append(o)
        lses.append(l)
    o = jnp.stack(outs, axis=2)
    w = jax.nn.softmax(jnp.stack(lses, axis=2), axis=2)
    y = jnp.sum(o * w[..., None], axis=2)
    return y.reshape(bsz, seq, A_OUT)


def short_conv(t, w):
    y = lax.conv_general_dilated(t, w[:, None, :].astype(t.dtype), window_strides=(1,),
                                 padding=((CONV_K - 1, 0),),
                                 dimension_numbers=('NWC', 'WIO', 'NWC'),
                                 feature_group_count=t.shape[-1])
    return jax.nn.silu(y)


def l2_normalize(t):
    tf = t.astype(jnp.float32)
    return tf * lax.rsqrt(jnp.sum(tf * tf, -1, keepdims=True) + RMS_EPS)


def gated_delta_rule(q, k, v, g, beta):
    bsz, seq, nh, dk = q.shape
    dv = v.shape[-1]
    nc = seq // CHUNK

    def chunks(t):
        t = jnp.moveaxis(t.astype(jnp.float32), 2, 1)
        return t.reshape(bsz, nh, nc, CHUNK, *t.shape[3:])

    q, k, v, g, beta = chunks(q), chunks(k), chunks(v), chunks(g), chunks(beta)
    g = jnp.cumsum(g, axis=-1)
    strict = np.tril(np.ones((CHUNK, CHUNK), bool), -1)
    incl = np.tril(np.ones((CHUNK, CHUNK), bool), 0)
    decay = jnp.exp(jnp.where(incl, g[..., :, None] - g[..., None, :], -jnp.inf))
    k_beta = k * beta[..., None]
    v_beta = v * beta[..., None]
    lower = jnp.where(strict, jnp.einsum('bhnie,bhnje->bhnij', k_beta, k) * decay, 0.0)
    eye = jnp.eye(CHUNK, dtype=jnp.float32)
    t_inv = lax.linalg.triangular_solve(eye + lower, jnp.broadcast_to(eye, lower.shape),
                                        left_side=True, lower=True, unit_diagonal=True)
    u = t_inv @ v_beta
    w = t_inv @ (k_beta * jnp.exp(g)[..., None])
    attn_intra = jnp.where(incl, jnp.einsum('bhnie,bhnje->bhnij', q, k) * decay, 0.0)
    q_dec = q * jnp.exp(g)[..., None]
    g_last = g[..., -1]
    k_tail = k * jnp.exp(g_last[..., None] - g)[..., None]
    state_decay = jnp.exp(g_last)[..., None, None]

    def step(state, inp):
        q_c, k_c, u_c, w_c, a_c, sd_c = inp
        v_new = u_c - w_c @ state
        o_c = q_c @ state + a_c @ v_new
        state = state * sd_c + jnp.einsum('bhce,bhcv->bhev', k_c, v_new)
        return state, o_c

    xs = tuple(jnp.moveaxis(t, 2, 0) for t in (q_dec, k_tail, u, w, attn_intra, state_decay))
    state0 = jnp.zeros((bsz, nh, dk, dv), jnp.float32)
    _, o = lax.scan(step, state0, xs)
    o = jnp.moveaxis(o, 0, 2).reshape(bsz, nh, seq, dv)
    return jnp.transpose(o, (0, 2, 1, 3))


def mixer_gated_deltanet(qkv, z, b_logit, a_logit, conv_w, a_log, dt_bias, norm_w):
    bsz, seq = qkv.shape[:2]
    qkv = short_conv(qkv, conv_w)
    q, k, v = jnp.split(qkv, 3, axis=-1)
    shp = (bsz, seq, B_HEADS, B_HEAD_DIM)
    q = l2_normalize(q.reshape(shp)) * (B_HEAD_DIM ** -0.5)
    k = l2_normalize(k.reshape(shp))
    v = v.reshape(shp)
    beta = jax.nn.sigmoid(b_logit.astype(jnp.float32))
    g = -jnp.exp(a_log.astype(jnp.float32)) * jax.nn.softplus(
        a_logit.astype(jnp.float32) + dt_bias.astype(jnp.float32))
    o = gated_delta_rule(q, k, v, g, beta)
    o = o * lax.rsqrt(jnp.mean(o * o, -1, keepdims=True) + RMS_EPS) * norm_w.astype(jnp.float32)
    o = o * jax.nn.silu(z.reshape(shp).astype(jnp.float32))
    return o.reshape(bsz, seq, B_WIDTH)


def swiglu(x, w_gate, w_up, w_down):
    return (jax.nn.silu(x @ w_gate) * (x @ w_up)) @ w_down


def moe_swiglu(x, router_w, w_gate, w_up, w_down):
    logits = (x @ router_w).astype(jnp.float32)
    top_val, top_idx = lax.top_k(logits, TOP_K)
    top_w = jax.nn.softmax(top_val, axis=-1)
    gates = jnp.sum(jax.nn.one_hot(top_idx, N_EXPERTS, dtype=jnp.float32) * top_w[..., None], axis=-2)
    gates = gates.astype(x.dtype)
    y = jnp.zeros_like(x)
    for e in range(N_EXPERTS):
        y = y + gates[..., e:e + 1] * swiglu(x, w_gate[e], w_up[e], w_down[e])
    return y


def setup_inputs(seed: int = 0) -> dict:
    key = jax.random.key(seed)
    ks = jax.random.split(key, 24)

    def nrm(k, shape, scale):
        return jax.random.normal(k, shape, jnp.float32) * scale

    x = jax.random.normal(ks[0], (BATCH, SEQ, D_MODEL), jnp.float32)
    positions = (jnp.arange(SEQ, dtype=jnp.int32)[None, :]
                 + jax.random.randint(ks[1], (BATCH, 1), 0, 1024, dtype=jnp.int32))
    w_in = nrm(ks[2], (DEPTH, D_MODEL, N_IN), D_MODEL ** -0.5)
    conv_w = nrm(ks[3], (DEPTH, CONV_K, 3 * B_WIDTH), CONV_K ** -0.5)
    a_log = jnp.log(jax.random.uniform(ks[4], (DEPTH, B_HEADS), jnp.float32, 1.0, 16.0))
    dt = jnp.exp(jax.random.uniform(ks[5], (DEPTH, B_HEADS), jnp.float32,
                                    math.log(1e-3), math.log(1e-1)))
    dt_bias = dt + jnp.log(-jnp.expm1(-dt))
    dn_norm_w = 1.0 + nrm(ks[6], (DEPTH, B_HEAD_DIM), 0.02)
    w_branch_a = nrm(ks[7], (DEPTH, A_OUT, D_MODEL), A_OUT ** -0.5)
    w_branch_b = nrm(ks[8], (DEPTH, B_WIDTH, D_MODEL), B_WIDTH ** -0.5)
    w_out = nrm(ks[9], (DEPTH, D_MODEL, D_MODEL), D_MODEL ** -0.5 * DN_BETA)
    ln1_g = 1.0 + nrm(ks[10], (DEPTH, D_MODEL), 0.02)
    ln1_b = nrm(ks[11], (DEPTH, D_MODEL), 0.02)
    ffn_w_gate = nrm(ks[12], (N_DENSE, D_MODEL, D_FF), D_MODEL ** -0.5)
    ffn_w_up = nrm(ks[13], (N_DENSE, D_MODEL, D_FF), D_MODEL ** -0.5)
    ffn_w_down = nrm(ks[14], (N_DENSE, D_FF, D_MODEL), D_FF ** -0.5 * DN_BETA)
    router_w = nrm(ks[15], (N_MOE, D_MODEL, N_EXPERTS), D_MODEL ** -0.5)
    moe_w_gate = nrm(ks[16], (N_MOE, N_EXPERTS, D_MODEL, D_EXPERT), D_MODEL ** -0.5)
    moe_w_up = nrm(ks[17], (N_MOE, N_EXPERTS, D_MODEL, D_EXPERT), D_MODEL ** -0.5)
    moe_w_down = nrm(ks[18], (N_MOE, N_EXPERTS, D_EXPERT, D_MODEL), D_EXPERT ** -0.5 * DN_BETA)
    ln2_g = 1.0 + nrm(ks[19], (DEPTH, D_MODEL), 0.02)
    ln2_b = nrm(ks[20], (DEPTH, D_MODEL), 0.02)
    return {"x": x, "positions": positions, "w_in": w_in, "conv_w": conv_w,
            "a_log": a_log, "dt_bias": dt_bias, "dn_norm_w": dn_norm_w,
            "w_branch_a": w_branch_a, "w_branch_b": w_branch_b, "w_out": w_out,
            "ln1_g": ln1_g, "ln1_b": ln1_b,
            "ffn_w_gate": ffn_w_gate, "ffn_w_up": ffn_w_up, "ffn_w_down": ffn_w_down,
            "router_w": router_w, "moe_w_gate": moe_w_gate, "moe_w_up": moe_w_up,
            "moe_w_down": moe_w_down, "ln2_g": ln2_g, "ln2_b": ln2_b}


def reference(x, positions, w_in, conv_w, a_log, dt_bias, dn_norm_w, w_branch_a, w_branch_b,
              w_out, ln1_g, ln1_b, ffn_w_gate, ffn_w_up, ffn_w_down, router_w,
              moe_w_gate, moe_w_up, moe_w_down, ln2_g, ln2_b):
    bsz, seq, _ = x.shape
    cos, sin = rotary_tables(positions)
    a_shape = (bsz, seq, A_GROUPS, A_HEADS, A_HEAD_DIM)
    for layer in range(DEPTH):
        h = x @ w_in[layer]
        qa, ka, va, qkv_b, z_b, beta_b, a_b, gate_a, gate_b = jnp.split(h, IN_SPLITS, axis=-1)
        y_a = mixer_dilated(qa.reshape(a_shape), ka.reshape(a_shape), va.reshape(a_shape),
                            cos, sin).astype(x.dtype)
        y_b = mixer_gated_deltanet(qkv_b, z_b, beta_b, a_b, conv_w[layer], a_log[layer],
                                   dt_bias[layer], dn_norm_w[layer]).astype(x.dtype)
        merged = (jax.nn.sigmoid(gate_a) * (y_a @ w_branch_a[layer])
                  + jax.nn.sigmoid(gate_b) * (y_b @ w_branch_b[layer]))
        x = layer_norm(DN_ALPHA * x + merged @ w_out[layer], ln1_g[layer], ln1_b[layer])
        if layer % 2 == 0:
            f = swiglu(x, ffn_w_gate[layer // 2], ffn_w_up[layer // 2], ffn_w_down[layer // 2])
        else:
            f = moe_swiglu(x, router_w[layer // 2], moe_w_gate[layer // 2],
                           moe_w_up[layer // 2], moe_w_down[layer // 2])
        x = layer_norm(DN_ALPHA * x + f, ln2_g[layer], ln2_b[layer])
    return x
```

```python
import functools
import math

import jax
import jax.numpy as jnp
from jax import lax
from jax.experimental import pallas as pl
from jax.experimental.pallas import tpu as pltpu

F32 = jnp.float32
BF16 = jnp.bfloat16

D_MODEL = 1024
DEPTH = 2
A_PAIRS = ((128, 1), (512, 4), (2048, 16))
A_HEADS = 4
HEAD_DIM = 128
A_GROUP_WIDTH = A_HEADS * HEAD_DIM
A_QKV_WIDTH = len(A_PAIRS) * A_GROUP_WIDTH
A_BLOCK = 128
ROPE_THETA = 500000.0
ROT_DIM = HEAD_DIM // 4
B_HEADS = 8
B_WIDTH = B_HEADS * HEAD_DIM
CONV_K = 4
CHUNK = 64
N_EXPERTS = 8
DN_ALPHA = (2 * DEPTH) ** 0.25
LN_EPS = 1e-5
RMS_EPS = 1e-6
NEG = -1e30
LANES = 128
VMEM_LIMIT = 56 * 1024 * 1024


def _cparams(sem):
    return pltpu.CompilerParams(dimension_semantics=sem, vmem_limit_bytes=VMEM_LIMIT)


def _dot(a, b):
    return jnp.dot(a, b, preferred_element_type=F32)


def _dot_nt(a, b):
    return lax.dot_general(a, b, (((1,), (1,)), ((), ())), preferred_element_type=F32)


def _dot_tn(a, b):
    return lax.dot_general(a, b, (((0,), (0,)), ((), ())), preferred_element_type=F32)


def _col_of(x, idx):
    lane = lax.broadcasted_iota(jnp.int32, x.shape, 1)
    return jnp.sum(jnp.where(lane == idx, x, 0.0), axis=1, keepdims=True)


def _layer_norm(v, g, b):
    mu = jnp.mean(v, axis=-1, keepdims=True)
    c = v - mu
    var = jnp.mean(c * c, axis=-1, keepdims=True)
    return c * lax.rsqrt(var + LN_EPS) * g + b


def _mm_kernel(x_ref, w_ref, o_ref):
    o_ref[...] = _dot(x_ref[...], w_ref[...]).astype(o_ref.dtype)


def _matmul(x, w, out_dtype, tm, tn):
    m, k = x.shape
    n = w.shape[1]
    return pl.pallas_call(
        _mm_kernel,
        grid=(m // tm, n // tn),
        in_specs=[pl.BlockSpec((tm, k), lambda i, j: (i, 0)),
                  pl.BlockSpec((k, tn), lambda i, j: (0, j))],
        out_specs=pl.BlockSpec((tm, tn), lambda i, j: (i, j)),
        out_shape=jax.ShapeDtypeStruct((m, n), out_dtype),
        compiler_params=_cparams(("parallel", "arbitrary")),
        name="proj_matmul",
    )(x, w)


def _rope(t, c, s1, s2):
    return t * c + pltpu.roll(t, LANES - ROT_DIM // 2, 1) * s1 + pltpu.roll(t, ROT_DIM // 2, 1) * s2


def _attn_kernel(q_ref, kc_ref, kp_ref, vc_ref, vp_ref,
                 cc_ref, s1c_ref, s2c_ref, cp_ref, s1p_ref, s2p_ref, o_ref, l_ref):
    n = pl.program_id(2)
    row = lax.broadcasted_iota(jnp.int32, (A_BLOCK, A_BLOCK), 0)
    col = lax.broadcasted_iota(jnp.int32, (A_BLOCK, A_BLOCK), 1)
    mask_c = col <= row
    mask_p = jnp.logical_and(col >= row, n > 0)
    cc, s1c, s2c = cc_ref[...], s1c_ref[...], s2c_ref[...]
    cp, s1p, s2p = cp_ref[...], s1p_ref[...], s2p_ref[...]
    scale = 1.0 / math.sqrt(HEAD_DIM)
    for h in range(A_HEADS):
        sl = slice(h * HEAD_DIM, (h + 1) * HEAD_DIM)
        q = (_rope(q_ref[:, sl].astype(F32), cc, s1c, s2c) * scale).astype(BF16)
        kc = _rope(kc_ref[:, sl].astype(F32), cc, s1c, s2c).astype(BF16)
        kp = _rope(kp_ref[:, sl].astype(F32), cp, s1p, s2p).astype(BF16)
        s_c = jnp.where(mask_c, _dot_nt(q, kc), NEG)
        s_p = jnp.where(mask_p, _dot_nt(q, kp), NEG)
        m = jnp.maximum(jnp.max(s_c, axis=1, keepdims=True), jnp.max(s_p, axis=1, keepdims=True))
        p_c = jnp.exp(s_c - m)
        p_p = jnp.exp(s_p - m)
        den = jnp.sum(p_c, axis=1, keepdims=True) + jnp.sum(p_p, axis=1, keepdims=True)
        o = (_dot(p_c.astype(BF16), vc_ref[:, sl]) + _dot(p_p.astype(BF16), vp_ref[:, sl])) / den
        o_ref[:, sl] = o.astype(o_ref.dtype)
        l_ref[:, sl] = jnp.broadcast_to(m + jnp.log(den), (A_BLOCK, HEAD_DIM))


def _attention_group(q_arr, k_arr, v_arr, cq, ck, cv, tabs):
    bsz, dil, sub_len, _ = q_arr.shape
    nblk = sub_len // A_BLOCK
    w = A_GROUP_WIDTH

    def cur(c):
        return pl.BlockSpec((None, None, A_BLOCK, w), lambda b, r, n: (b, r, n, c))

    def prev(c):
        return pl.BlockSpec((None, None, A_BLOCK, w), lambda b, r, n: (b, r, jnp.maximum(n - 1, 0), c))

    tcur = pl.BlockSpec((None, None, A_BLOCK, LANES), lambda b, r, n: (b, r, n, 0))
    tprev = pl.BlockSpec((None, None, A_BLOCK, LANES), lambda b, r, n: (b, r, jnp.maximum(n - 1, 0), 0))
    out_spec = pl.BlockSpec((None, None, A_BLOCK, w), lambda b, r, n: (b, r, n, 0))
    return pl.pallas_call(
        _attn_kernel,
        grid=(bsz, dil, nblk),
        in_specs=[cur(cq), cur(ck), prev(ck), cur(cv), prev(cv), tcur, tcur, tcur, tprev, tprev, tprev],
        out_specs=[out_spec, out_spec],
        out_shape=[jax.ShapeDtypeStruct((bsz, dil, sub_len, w), BF16),
                   jax.ShapeDtypeStruct((bsz, dil, sub_len, w), F32)],
        compiler_params=_cparams(("parallel", "parallel", "arbitrary")),
        name="dilated_attention",
    )(q_arr, k_arr, k_arr, v_arr, v_arr, *tabs, *tabs)


def _gate_kernel(bd_ref, arow_ref, dtrow_ref, beta_ref, gc_ref):
    x = bd_ref[...]
    beta_ref[...] = jax.nn.sigmoid(x)
    y = x + dtrow_ref[...]
    softplus = jnp.maximum(y, 0.0) + jnp.log(1.0 + jnp.exp(-jnp.abs(y)))
    g = arow_ref[...] * softplus
    tt = x.shape[0]
    ri = lax.broadcasted_iota(jnp.int32, (tt, tt), 0)
    ci = lax.broadcasted_iota(jnp.int32, (tt, tt), 1)
    tri = jnp.where(jnp.logical_and(ri // CHUNK == ci // CHUNK, ci <= ri), 1.0, 0.0).astype(F32)
    gc_ref[...] = jnp.dot(tri, g, precision=lax.Precision.HIGHEST, preferred_element_type=F32)


def _gates(bd, arow, dtrow, tt=512):
    t = bd.shape[0]
    spec = pl.BlockSpec((tt, LANES), lambda i: (i, 0))
    rspec = pl.BlockSpec((1, LANES), lambda i: (0, 0))
    return pl.pallas_call(
        _gate_kernel,
        grid=(t // tt,),
        in_specs=[spec, rspec, rspec],
        out_specs=[spec, spec],
        out_shape=[jax.ShapeDtypeStruct((t, LANES), F32)] * 2,
        compiler_params=_cparams(("parallel",)),
        name="deltanet_gates",
    )(bd, arow, dtrow)


HALO = 16


def _dn_prep_kernel(q_ref, k_ref, v_ref, qh_ref, kh_ref, vh_ref, wq_ref, wk_ref, wv_ref,
                    beta_ref, gc_ref, gct_ref, qo_ref, ko_ref, vo_ref, a_ref, scr):
    t = pl.program_id(1)
    h = pl.program_id(2)
    tt = q_ref.shape[0]

    def conv_silu(x_ref, halo_ref, w_ref):
        scr[0:HALO, :] = jnp.where(t > 0, halo_ref[...].astype(F32), 0.0)
        scr[HALO:, :] = x_ref[...].astype(F32)
        w = w_ref[...]
        y = scr[HALO - 3:HALO - 3 + tt, :] * w[0:1, :]
        for j in range(1, CONV_K):
            y = y + scr[HALO - 3 + j:HALO - 3 + j + tt, :] * w[j:j + 1, :]
        return y * jax.nn.sigmoid(y)

    def l2n(v):
        return v * lax.rsqrt(jnp.sum(v * v, axis=1, keepdims=True) + RMS_EPS)

    q = l2n(conv_silu(q_ref, qh_ref, wq_ref)) * (HEAD_DIM ** -0.5)
    qo_ref[...] = q.astype(qo_ref.dtype)
    k = l2n(conv_silu(k_ref, kh_ref, wk_ref)).astype(BF16)
    ko_ref[...] = k
    vo_ref[...] = conv_silu(v_ref, vh_ref, wv_ref).astype(vo_ref.dtype)

    bcol = _col_of(beta_ref[...], h)
    gcol = _col_of(gc_ref[...], B_HEADS + h)
    grow = gct_ref[pl.ds(h, 1), :]
    g2 = 2 * CHUNK
    ri = lax.broadcasted_iota(jnp.int32, (g2, g2), 0)
    ci = lax.broadcasted_iota(jnp.int32, (g2, g2), 1)
    strict = jnp.logical_and((ri >= CHUNK) == (ci >= CHUNK), ri > ci)
    lane = lax.broadcasted_iota(jnp.int32, (CHUNK, g2), 1)
    for gi in range(tt // g2):
        rs = slice(gi * g2, (gi + 1) * g2)
        kb = k[rs, :]
        kbeta = (kb.astype(F32) * bcol[rs, :]).astype(BF16)
        diff = gcol[rs, :] - grow[:, rs]
        a = _dot_nt(kbeta, kb) * jnp.exp(jnp.where(strict, diff, NEG))
        a_ref[gi] = jnp.where(lane < CHUNK, a[0:CHUNK, :], a[CHUNK:g2, :])


def _dn_prep(qkvb, conv_w, beta, gc, gct, tt):
    bsz, seq, _ = qkvb.shape
    nh = B_HEADS
    hb = tt // HALO

    def tok(off):
        return pl.BlockSpec((None, tt, HEAD_DIM), lambda b, t, h: (b, t, off + h))

    def halo(off):
        return pl.BlockSpec((None, HALO, HEAD_DIM), lambda b, t, h: (b, jnp.maximum(t * hb - 1, 0), off + h))

    def cw(off):
        return pl.BlockSpec((CONV_K, HEAD_DIM), lambda b, t, h: (0, off + h))

    full = pl.BlockSpec((None, tt, LANES), lambda b, t, h: (b, t, 0))
    gts = pl.BlockSpec((None, nh, tt), lambda b, t, h: (b, 0, t))
    ospec = pl.BlockSpec((None, tt, HEAD_DIM), lambda b, t, h: (b, t, h))
    aspec = pl.BlockSpec((None, None, tt // (2 * CHUNK), CHUNK, 2 * CHUNK), lambda b, t, h: (b, h, t, 0, 0))
    return pl.pallas_call(
        _dn_prep_kernel,
        grid=(bsz, seq // tt, nh),
        in_specs=[tok(0), tok(nh), tok(2 * nh), halo(0), halo(nh), halo(2 * nh),
                  cw(0), cw(nh), cw(2 * nh), full, full, gts],
        out_specs=[ospec, ospec, ospec, aspec],
        out_shape=[jax.ShapeDtypeStruct((bsz, seq, B_WIDTH), BF16)] * 3
        + [jax.ShapeDtypeStruct((bsz, nh, seq // (2 * CHUNK), CHUNK, 2 * CHUNK), F32)],
        scratch_shapes=[pltpu.VMEM((tt + HALO, HEAD_DIM), F32)],
        compiler_params=_cparams(("parallel", "parallel", "arbitrary")),
        name="deltanet_prep",
    )(qkvb, qkvb, qkvb, qkvb, qkvb, qkvb, conv_w, conv_w, conv_w, beta, gc, gct)


JB = 16


def _tri_solve_kernel(a_ref, t_ref):
    i = pl.program_id(1)
    for jb in range(CHUNK // JB):
        def body(m, acc):
            a = a_ref[m]
            return tuple(acc[jj] - a * t_ref[m * CHUNK + jb * JB + jj] for jj in range(JB))

        zero = jnp.zeros(a_ref.shape[1:], F32)
        acc = lax.fori_loop(jb * JB, i, body, tuple(zero for _ in range(JB)))
        for jj in range(JB):
            j = jb * JB + jj
            t_ref[i * CHUNK + j] = acc[jj] + jnp.where(i == j, 1.0, 0.0).astype(F32)


def _tri_solve(a_t):
    nn, ns, _ = a_t.shape
    sb = min(ns, 8)
    return pl.pallas_call(
        _tri_solve_kernel,
        grid=(ns // sb, CHUNK),
        in_specs=[pl.BlockSpec((CHUNK, sb, LANES), lambda s, i: (i, s, 0))],
        out_specs=pl.BlockSpec((nn, sb, LANES), lambda s, i: (0, s, 0)),
        out_shape=jax.ShapeDtypeStruct(a_t.shape, F32),
        compiler_params=_cparams(("parallel", "arbitrary")),
        name="deltanet_tri_solve",
    )(a_t)


def _dn_main_kernel(q_ref, k_ref, v_ref, beta_ref, gc_ref, gct_ref, t_ref, z_ref, nw_ref, o_ref, state):
    t = pl.program_id(2)
    h = pl.program_id(1)
    tt = q_ref.shape[0]
    g2 = 2 * CHUNK

    @pl.when(t == 0)
    def _():
        state[...] = jnp.zeros_like(state)

    bcol = _col_of(beta_ref[...], h)
    gcol = _col_of(gc_ref[...], B_HEADS + h)
    grow = gct_ref[pl.ds(h, 1), :]
    ri = lax.broadcasted_iota(jnp.int32, (g2, g2), 0)
    ci = lax.broadcasted_iota(jnp.int32, (g2, g2), 1)
    incl = jnp.logical_and((ri >= CHUNK) == (ci >= CHUNK), ri >= ci)
    lane = lax.broadcasted_iota(jnp.int32, (CHUNK, g2), 1)
    nw = nw_ref[...]
    s = state[...]
    for gi in range(tt // g2):
        rs = slice(gi * g2, (gi + 1) * g2)
        qb = q_ref[rs, :]
        kb = k_ref[rs, :]
        kf = kb.astype(F32)
        b = bcol[rs, :]
        gcl = gcol[rs, :]
        eg = jnp.exp(gcl)
        kbeta = kf * b
        rhs = jnp.concatenate([v_ref[rs, :].astype(F32) * b, kbeta * eg], axis=1).astype(BF16)
        tp = t_ref[gi]
        tbd = jnp.concatenate([jnp.where(lane < CHUNK, tp, 0.0), jnp.where(lane >= CHUNK, tp, 0.0)], axis=0)
        uw = _dot(tbd.astype(BF16), rhs)
        u = uw[:, :HEAD_DIM]
        w = uw[:, HEAD_DIM:].astype(BF16)
        diff = gcl - grow[:, rs]
        attn = (_dot_nt(qb, kb) * jnp.exp(jnp.where(incl, diff, NEG))).astype(BF16)
        qdec = (qb.astype(F32) * eg).astype(BF16)
        outs = []
        vnews = []
        for cc in range(2):
            cs = slice(cc * CHUNK, (cc + 1) * CHUNK)
            glast = gcl[cc * CHUNK + CHUNK - 1:cc * CHUNK + CHUNK, :]
            ktail = (kf[cs, :] * jnp.exp(glast - gcl[cs, :])).astype(BF16)
            sb = s.astype(BF16)
            vnew = u[cs, :] - _dot(w[cs, :], sb)
            vnews.append(vnew.astype(BF16))
            if cc == 0:
                intra = _dot(attn[cs, 0:CHUNK], vnews[0])
            else:
                intra = _dot(attn[cs, :], jnp.concatenate(vnews, axis=0))
            outs.append(_dot(qdec[cs, :], sb) + intra)
            s = s * jnp.exp(glast) + _dot_tn(ktail, vnews[cc])
        o = jnp.concatenate(outs, axis=0)
        o = o * lax.rsqrt(jnp.mean(o * o, axis=1, keepdims=True) + RMS_EPS) * nw
        z = z_ref[rs, :].astype(F32)
        o_ref[rs, :] = (o * (z * jax.nn.sigmoid(z))).astype(o_ref.dtype)
    state[...] = s


def _dn_main(qn, kn, vn, beta, gc, gct, t_p, zg, norm_w, tt):
    bsz, seq, _ = qn.shape
    nh = B_HEADS
    tok = pl.BlockSpec((None, tt, HEAD_DIM), lambda b, h, t: (b, t, h))
    full = pl.BlockSpec((None, tt, LANES), lambda b, h, t: (b, t, 0))
    gts = pl.BlockSpec((None, nh, tt), lambda b, h, t: (b, 0, t))
    tspec = pl.BlockSpec((None, None, tt // (2 * CHUNK), CHUNK, 2 * CHUNK), lambda b, h, t: (b, h, t, 0, 0))
    nspec = pl.BlockSpec((1, HEAD_DIM), lambda b, h, t: (0, 0))
    return pl.pallas_call(
        _dn_main_kernel,
        grid=(bsz, nh, seq // tt),
        in_specs=[tok, tok, tok, full, full, gts, tspec, tok, nspec],
        out_specs=tok,
        out_shape=jax.ShapeDtypeStruct((bsz, seq, B_WIDTH), BF16),
        scratch_shapes=[pltpu.VMEM((HEAD_DIM, HEAD_DIM), F32)],
        compiler_params=_cparams(("parallel", "parallel", "arbitrary")),
        name="deltanet_main",
    )(qn, kn, vn, beta, gc, gct, t_p, zg, norm_w)


def _mix_out_kernel(o0_ref, o1_ref, o2_ref, l0_ref, l1_ref, l2_ref, yb_ref, ga_ref, gb_ref, x_ref,
                    wa_ref, wb_ref, wo_ref, g_ref, b_ref, xo_ref, xbo_ref):
    l0, l1, l2 = l0_ref[...], l1_ref[...], l2_ref[...]
    m = jnp.maximum(jnp.maximum(l0, l1), l2)
    e0, e1, e2 = jnp.exp(l0 - m), jnp.exp(l1 - m), jnp.exp(l2 - m)
    ya = (e0 * o0_ref[...].astype(F32) + e1 * o1_ref[...].astype(F32) + e2 * o2_ref[...].astype(F32)) / (e0 + e1 + e2)
    ma = _dot(ya.astype(BF16), wa_ref[...])
    mb = _dot(yb_ref[...], wb_ref[...])
    merged = jax.nn.sigmoid(ga_ref[...].astype(F32)) * ma + jax.nn.sigmoid(gb_ref[...].astype(F32)) * mb
    r = _dot(merged.astype(BF16), wo_ref[...])
    y = _layer_norm(DN_ALPHA * x_ref[...] + r, g_ref[...], b_ref[...])
    xo_ref[...] = y
    xbo_ref[...] = y.astype(BF16)


def _mix_out(o_list, l_list, yb, zg, x, wa, wb, wo, g, b, tm=512):
    t = x.shape[0]
    d = D_MODEL
    aw = A_GROUP_WIDTH

    def rows(wd, c=0):
        return pl.BlockSpec((tm, wd), lambda i: (i, c))

    def whole(shape):
        return pl.BlockSpec(shape, lambda i: (0, 0))

    return pl.pallas_call(
        _mix_out_kernel,
        grid=(t // tm,),
        in_specs=[rows(aw)] * 6 + [rows(d), rows(d, 1), rows(d, 2), rows(d),
                                   whole((aw, d)), whole((d, d)), whole((d, d)), whole((1, d)), whole((1, d))],
        out_specs=[rows(d), rows(d)],
        out_shape=[jax.ShapeDtypeStruct((t, d), F32), jax.ShapeDtypeStruct((t, d), BF16)],
        compiler_params=_cparams(("parallel",)),
        name="mix_out",
    )(*o_list, *l_list, yb, zg, zg, x, wa, wb, wo, g, b)


def _ffn_kernel(xb_ref, x_ref, wg_ref, wu_ref, wd_ref, g_ref, b_ref, xo_ref, xbo_ref, acc):
    f = pl.program_id(1)

    @pl.when(f == 0)
    def _():
        acc[...] = jnp.zeros_like(acc)

    xb = xb_ref[...]
    gt = _dot(xb, wg_ref[...])
    up = _dot(xb, wu_ref[...])
    hh = (gt * jax.nn.sigmoid(gt) * up).astype(BF16)
    acc[...] += _dot(hh, wd_ref[...])

    @pl.when(f == pl.num_programs(1) - 1)
    def _():
        y = _layer_norm(DN_ALPHA * x_ref[...] + acc[...], g_ref[...], b_ref[...])
        xo_ref[...] = y
        xbo_ref[...] = y.astype(BF16)


def _ffn(xb, x, wg, wu, wd, g, b, tm=1024, tf=256):
    t, d = x.shape
    dff = wg.shape[1]
    rows = pl.BlockSpec((tm, d), lambda i, f: (i, 0))
    vec = pl.BlockSpec((1, d), lambda i, f: (0, 0))
    return pl.pallas_call(
        _ffn_kernel,
        grid=(t // tm, dff // tf),
        in_specs=[rows, rows,
                  pl.BlockSpec((d, tf), lambda i, f: (0, f)),
                  pl.BlockSpec((d, tf), lambda i, f: (0, f)),
                  pl.BlockSpec((tf, d), lambda i, f: (f, 0)), vec, vec],
        out_specs=[rows, rows],
        out_shape=[jax.ShapeDtypeStruct((t, d), F32), jax.ShapeDtypeStruct((t, d), BF16)],
        scratch_shapes=[pltpu.VMEM((tm, d), F32)],
        compiler_params=_cparams(("parallel", "arbitrary")),
        name="dense_swiglu",
    )(xb, x, wg, wu, wd, g, b)


def _router_kernel(x_ref, rw_ref, gates_ref):
    logits = jnp.dot(x_ref[...], rw_ref[...], precision=lax.Precision.HIGHEST, preferred_element_type=F32)
    lane = lax.broadcasted_iota(jnp.int32, logits.shape, 1)
    lanef = lane.astype(F32)
    ninf = -jnp.inf
    lg = jnp.where(lane < N_EXPERTS, logits, ninf)
    m1 = jnp.max(lg, axis=1, keepdims=True)
    i1 = jnp.min(jnp.where(lg == m1, lanef, float(LANES)), axis=1, keepdims=True)
    lg2 = jnp.where(lanef == i1, ninf, lg)
    m2 = jnp.max(lg2, axis=1, keepdims=True)
    i2 = jnp.min(jnp.where(lg2 == m2, lanef, float(LANES)), axis=1, keepdims=True)
    e = jnp.exp(m2 - m1)
    w1 = 1.0 / (1.0 + e)
    w2 = e / (1.0 + e)
    gates_ref[...] = jnp.where(lanef == i1, w1, jnp.where(lanef == i2, w2, 0.0))


def _router(x, rw_pad, tm=1024):
    t, d = x.shape
    return pl.pallas_call(
        _router_kernel,
        grid=(t // tm,),
        in_specs=[pl.BlockSpec((tm, d), lambda i: (i, 0)), pl.BlockSpec((d, LANES), lambda i: (0, 0))],
        out_specs=pl.BlockSpec((tm, LANES), lambda i: (i, 0)),
        out_shape=jax.ShapeDtypeStruct((t, LANES), F32),
        compiler_params=_cparams(("parallel",)),
        name="moe_router",
    )(x, rw_pad)


def _moe_kernel(xb_ref, x_ref, gates_ref, wg_ref, wu_ref, wd_ref, g_ref, b_ref, xo_ref, acc):
    e = pl.program_id(1)
    f = pl.program_id(2)

    @pl.when(jnp.logical_and(e == 0, f == 0))
    def _():
        acc[...] = jnp.zeros_like(acc)

    xb = xb_ref[...]
    gate = _col_of(gates_ref[...], e)
    gt = _dot(xb, wg_ref[...])
    up = _dot(xb, wu_ref[...])
    hh = (gt * jax.nn.sigmoid(gt) * up * gate).astype(BF16)
    acc[...] += _dot(hh, wd_ref[...])

    @pl.when(jnp.logical_and(e == pl.num_programs(1) - 1, f == pl.num_programs(2) - 1))
    def _():
        xo_ref[...] = _layer_norm(DN_ALPHA * x_ref[...] + acc[...], g_ref[...], b_ref[...])


def _moe(xb, x, gates, wg, wu, wd, g, b, tm=1024, tf=512):
    t, d = x.shape
    ne, _, dex = wg.shape
    rows = pl.BlockSpec((tm, d), lambda i, e, f: (i, 0))
    vec = pl.BlockSpec((1, d), lambda i, e, f: (0, 0))
    return pl.pallas_call(
        _moe_kernel,
        grid=(t // tm, ne, dex // tf),
        in_specs=[rows, rows, pl.BlockSpec((tm, LANES), lambda i, e, f: (i, 0)),
                  pl.BlockSpec((None, d, tf), lambda i, e, f: (e, 0, f)),
                  pl.BlockSpec((None, d, tf), lambda i, e, f: (e, 0, f)),
                  pl.BlockSpec((None, tf, d), lambda i, e, f: (e, f, 0)), vec, vec],
        out_specs=rows,
        out_shape=jax.ShapeDtypeStruct((t, d), F32),
        scratch_shapes=[pltpu.VMEM((tm, d), F32)],
        compiler_params=_cparams(("parallel", "arbitrary", "arbitrary")),
        name="moe_swiglu",
    )(xb, x, gates, wg, wu, wd, g, b)


def _to_sub(a, bsz, seq, dil):
    w = a.shape[-1]
    return a.reshape(bsz, seq // dil, dil, w).transpose(0, 2, 1, 3)


def _from_sub(a):
    bsz, dil, sub_len, w = a.shape
    return a.transpose(0, 2, 1, 3).reshape(bsz * sub_len * dil, w)


def _rope_tables(positions):
    half = ROT_DIM // 2
    inv_freq = ROPE_THETA ** (-jnp.arange(0, ROT_DIM, 2, dtype=F32) / ROT_DIM)
    ang = positions.astype(F32)[..., None] * inv_freq
    cos, sin = jnp.cos(ang), jnp.sin(ang)
    shp = cos.shape[:-1]
    c = jnp.concatenate([cos, cos, jnp.ones(shp + (LANES - ROT_DIM,), F32)], axis=-1)
    s1 = jnp.concatenate([-sin, jnp.zeros(shp + (LANES - half,), F32)], axis=-1)
    s2 = jnp.concatenate([jnp.zeros(shp + (half,), F32), sin, jnp.zeros(shp + (LANES - ROT_DIM,), F32)], axis=-1)
    return c, s1, s2


def kernel(x, positions, w_in, conv_w, a_log, dt_bias, dn_norm_w, w_branch_a, w_branch_b, w_out, ln1_g, ln1_b,
           ffn_w_gate, ffn_w_up, ffn_w_down, router_w, moe_w_gate, moe_w_up, moe_w_down, ln2_g, ln2_b):
    bsz, seq, d = x.shape
    t = bsz * seq
    nh = B_HEADS
    qa_w = 3 * A_QKV_WIDTH
    o_qkvb = qa_w
    o_z = o_qkvb + 3 * B_WIDTH
    o_bd = o_z + B_WIDTH
    o_gates = o_bd + 2 * nh

    tabs = _rope_tables(positions)
    tabs_g = []
    for _, dil in A_PAIRS:
        tabs_g.append([_to_sub(tb.reshape(t, LANES), bsz, seq, dil) for tb in tabs])

    dn_tt = 512
    xf = x.reshape(t, d)
    xb = xf.astype(BF16)
    for layer in range(DEPTH):
        w = w_in[layer]
        w_qkva = w[:, :qa_w].astype(BF16)
        w_qkvb = w[:, o_qkvb:o_z].astype(BF16)
        w_zg = jnp.concatenate([w[:, o_z:o_bd], w[:, o_gates:]], axis=1).astype(BF16)
        w_bd = jnp.pad(w[:, o_bd:o_gates], ((0, 0), (0, LANES - 2 * nh))).astype(BF16)

        qkva = _matmul(xb, w_qkva, BF16, 1024, 1536)
        qkvb = _matmul(xb, w_qkvb, BF16, 1024, 1024)
        zg = _matmul(xb, w_zg, BF16, 1024, 1024)
        bd = _matmul(xb, w_bd, F32, 1024, LANES)

        o_list, l_list = [], []
        for g, (_, dil) in enumerate(A_PAIRS):
            if dil == 1:
                arr = qkva.reshape(bsz, 1, seq, qa_w)
                o_g, l_g = _attention_group(arr, arr, arr, g, 3 + g, 6 + g, tabs_g[g])
            else:
                parts = [_to_sub(qkva[:, (3 * p + g) * A_GROUP_WIDTH:(3 * p + g + 1) * A_GROUP_WIDTH], bsz, seq, dil)
                         for p in range(3)]
                o_g, l_g = _attention_group(parts[0], parts[1], parts[2], 0, 0, 0, tabs_g[g])
            o_list.append(_from_sub(o_g))
            l_list.append(_from_sub(l_g))

        zeros8 = jnp.zeros((nh,), F32)
        arow = jnp.concatenate([zeros8, -jnp.exp(a_log[layer].astype(F32)), jnp.zeros((LANES - 2 * nh,), F32)])[None, :]
        dtrow = jnp.concatenate([zeros8, dt_bias[layer].astype(F32), jnp.zeros((LANES - 2 * nh,), F32)])[None, :]
        beta, gc = _gates(bd, arow, dtrow)
        beta3 = beta.reshape(bsz, seq, LANES)
        gc3 = gc.reshape(bsz, seq, LANES)
        gct = gc3[:, :, nh:2 * nh].transpose(0, 2, 1)
        qn, kn, vn, a_p = _dn_prep(qkvb.reshape(bsz, seq, 3 * B_WIDTH), conv_w[layer].astype(F32),
                                   beta3, gc3, gct, dn_tt)
        ng = seq // (2 * CHUNK)
        nch = bsz * nh * ng * 2
        a_t = a_p.reshape(bsz, nh, ng, CHUNK, 2, CHUNK).transpose(3, 5, 0, 1, 2, 4).reshape(CHUNK * CHUNK, nch // LANES, LANES)
        t_t = _tri_solve(a_t)
        t_p = t_t.reshape(CHUNK, CHUNK, bsz, nh, ng, 2).transpose(2, 3, 4, 0, 5, 1).reshape(bsz, nh, ng, CHUNK, 2 * CHUNK)
        yb = _dn_main(qn, kn, vn, beta3, gc3, gct, t_p, zg.reshape(bsz, seq, 3 * B_WIDTH),
                      dn_norm_w[layer].astype(F32)[None, :], dn_tt)

        xf, xb = _mix_out(o_list, l_list, yb.reshape(t, B_WIDTH), zg, xf,
                          w_branch_a[layer].astype(BF16), w_branch_b[layer].astype(BF16), w_out[layer].astype(BF16),
                          ln1_g[layer].astype(F32)[None, :], ln1_b[layer].astype(F32)[None, :])

        g2 = ln2_g[layer].astype(F32)[None, :]
        b2 = ln2_b[layer].astype(F32)[None, :]
        if layer % 2 == 0:
            i = layer // 2
            xf, xb = _ffn(xb, xf, ffn_w_gate[i].astype(BF16), ffn_w_up[i].astype(BF16), ffn_w_down[i].astype(BF16), g2, b2)
        else:
            i = layer // 2
            rw = jnp.pad(router_w[i].astype(F32), ((0, 0), (0, LANES - N_EXPERTS)))
            gates = _router(xf, rw)
            xf = _moe(xb, xf, gates, moe_w_gate[i].astype(BF16), moe_w_up[i].astype(BF16),
                      moe_w_down[i].astype(BF16), g2, b2)
            xb = xf.astype(BF16)
    return xf.reshape(bsz, seq, d)
```

```python
import functools
import math

import jax
import jax.numpy as jnp
from jax import lax
from jax.experimental import pallas as pl
from jax.experimental.pallas import tpu as pltpu

F32 = jnp.float32
BF16 = jnp.bfloat16

D_MODEL = 1024
DEPTH = 2
A_PAIRS = ((128, 1), (512, 4), (2048, 16))
A_HEADS = 4
HEAD_DIM = 128
A_GROUP_WIDTH = A_HEADS * HEAD_DIM
A_QKV_WIDTH = len(A_PAIRS) * A_GROUP_WIDTH
A_BLOCK = 128
ROPE_THETA = 500000.0
ROT_DIM = HEAD_DIM // 4
B_HEADS = 8
B_WIDTH = B_HEADS * HEAD_DIM
CONV_K = 4
CHUNK = 64
N_EXPERTS = 8
DN_ALPHA = (2 * DEPTH) ** 0.25
LN_EPS = 1e-5
RMS_EPS = 1e-6
NEG = -1e30
LANES = 128
VMEM_LIMIT = 56 * 1024 * 1024


def _cparams(sem):
    return pltpu.CompilerParams(dimension_semantics=sem, vmem_limit_bytes=VMEM_LIMIT)


def _dot(a, b):
    return jnp.dot(a, b, preferred_element_type=F32)


def _dot_nt(a, b):
    return lax.dot_general(a, b, (((1,), (1,)), ((), ())), preferred_element_type=F32)


def _dot_tn(a, b):
    return lax.dot_general(a, b, (((0,), (0,)), ((), ())), preferred_element_type=F32)


def _col_of(x, idx):
    lane = lax.broadcasted_iota(jnp.int32, x.shape, 1)
    return jnp.sum(jnp.where(lane == idx, x, 0.0), axis=1, keepdims=True)


def _layer_norm(v, g, b):
    mu = jnp.mean(v, axis=-1, keepdims=True)
    c = v - mu
    var = jnp.mean(c * c, axis=-1, keepdims=True)
    return c * lax.rsqrt(var + LN_EPS) * g + b


def _mm_kernel(x_ref, w_ref, o_ref):
    o_ref[...] = _dot(x_ref[...], w_ref[...]).astype(o_ref.dtype)


def _matmul(x, w, out_dtype, tm, tn):
    m, k = x.shape
    n = w.shape[1]
    return pl.pallas_call(
        _mm_kernel,
        grid=(m // tm, n // tn),
        in_specs=[pl.BlockSpec((tm, k), lambda i, j: (i, 0)),
                  pl.BlockSpec((k, tn), lambda i, j: (0, j))],
        out_specs=pl.BlockSpec((tm, tn), lambda i, j: (i, j)),
        out_shape=jax.ShapeDtypeStruct((m, n), out_dtype),
        compiler_params=_cparams(("parallel", "arbitrary")),
        name="proj_matmul",
    )(x, w)


def _rope(t, c, s1, s2):
    return t * c + pltpu.roll(t, LANES - ROT_DIM // 2, 1) * s1 + pltpu.roll(t, ROT_DIM // 2, 1) * s2


def _attn_kernel(q_ref, kc_ref, kp_ref, vc_ref, vp_ref,
                 cc_ref, s1c_ref, s2c_ref, cp_ref, s1p_ref, s2p_ref, o_ref, l_ref):
    n = pl.program_id(2)
    row = lax.broadcasted_iota(jnp.int32, (A_BLOCK, A_BLOCK), 0)
    col = lax.broadcasted_iota(jnp.int32, (A_BLOCK, A_BLOCK), 1)
    mask_c = col <= row
    mask_p = jnp.logical_and(col >= row, n > 0)
    cc, s1c, s2c = cc_ref[...], s1c_ref[...], s2c_ref[...]
    cp, s1p, s2p = cp_ref[...], s1p_ref[...], s2p_ref[...]
    scale = 1.0 / math.sqrt(HEAD_DIM)
    for h in range(A_HEADS):
        sl = slice(h * HEAD_DIM, (h + 1) * HEAD_DIM)
        q = (_rope(q_ref[:, sl].astype(F32), cc, s1c, s2c) * scale).astype(BF16)
        kc = _rope(kc_ref[:, sl].astype(F32), cc, s1c, s2c).astype(BF16)
        kp = _rope(kp_ref[:, sl].astype(F32), cp, s1p, s2p).astype(BF16)
        s_c = jnp.where(mask_c, _dot_nt(q, kc), NEG)
        s_p = jnp.where(mask_p, _dot_nt(q, kp), NEG)
        m = jnp.maximum(jnp.max(s_c, axis=1, keepdims=True), jnp.max(s_p, axis=1, keepdims=True))
        p_c = jnp.exp(s_c - m)
        p_p = jnp.exp(s_p - m)
        den = jnp.sum(p_c, axis=1, keepdims=True) + jnp.sum(p_p, axis=1, keepdims=True)
        o = (_dot(p_c.astype(BF16), vc_ref[:, sl]) + _dot(p_p.astype(BF16), vp_ref[:, sl])) / den
        o_ref[:, sl] = o.astype(o_ref.dtype)
        l_ref[:, sl] = jnp.broadcast_to(m + jnp.log(den), (A_BLOCK, HEAD_DIM))


def _attention_group(q_arr, k_arr, v_arr, cq, ck, cv, tabs):
    bsz, dil, sub_len, _ = q_arr.shape
    nblk = sub_len // A_BLOCK
    w = A_GROUP_WIDTH

    def cur(c):
        return pl.BlockSpec((None, None, A_BLOCK, w), lambda b, r, n: (b, r, n, c))

    def prev(c):
        return pl.BlockSpec((None, None, A_BLOCK, w), lambda b, r, n: (b, r, jnp.maximum(n - 1, 0), c))

    tcur = pl.BlockSpec((None, None, A_BLOCK, LANES), lambda b, r, n: (b, r, n, 0))
    tprev = pl.BlockSpec((None, None, A_BLOCK, LANES), lambda b, r, n: (b, r, jnp.maximum(n - 1, 0), 0))
    out_spec = pl.BlockSpec((None, None, A_BLOCK, w), lambda b, r, n: (b, r, n, 0))
    return pl.pallas_call(
        _attn_kernel,
        grid=(bsz, dil, nblk),
        in_specs=[cur(cq), cur(ck), prev(ck), cur(cv), prev(cv), tcur, tcur, tcur, tprev, tprev, tprev],
        out_specs=[out_spec, out_spec],
        out_shape=[jax.ShapeDtypeStruct((bsz, dil, sub_len, w), BF16),
                   jax.ShapeDtypeStruct((bsz, dil, sub_len, w), F32)],
        compiler_params=_cparams(("parallel", "parallel", "arbitrary")),
        name="dilated_attention",
    )(q_arr, k_arr, k_arr, v_arr, v_arr, *tabs, *tabs)


def _gate_kernel(bd_ref, arow_ref, dtrow_ref, beta_ref, gc_ref):
    x = bd_ref[...]
    beta_ref[...] = jax.nn.sigmoid(x)
    y = x + dtrow_ref[...]
    softplus = jnp.maximum(y, 0.0) + jnp.log(1.0 + jnp.exp(-jnp.abs(y)))
    g = arow_ref[...] * softplus
    tt = x.shape[0]
    ri = lax.broadcasted_iota(jnp.int32, (tt, tt), 0)
    ci = lax.broadcasted_iota(jnp.int32, (tt, tt), 1)
    tri = jnp.where(jnp.logical_and(ri // CHUNK == ci // CHUNK, ci <= ri), 1.0, 0.0).astype(F32)
    gc_ref[...] = jnp.dot(tri, g, precision=lax.Precision.HIGHEST, preferred_element_type=F32)


def _gates(bd, arow, dtrow, tt=512):
    t = bd.shape[0]
    spec = pl.BlockSpec((tt, LANES), lambda i: (i, 0))
    rspec = pl.BlockSpec((1, LANES), lambda i: (0, 0))
    return pl.pallas_call(
        _gate_kernel,
        grid=(t // tt,),
        in_specs=[spec, rspec, rspec],
        out_specs=[spec, spec],
        out_shape=[jax.ShapeDtypeStruct((t, LANES), F32)] * 2,
        compiler_params=_cparams(("parallel",)),
        name="deltanet_gates",
    )(bd, arow, dtrow)


HALO = 16


def _dn_prep_kernel(q_ref, k_ref, v_ref, qh_ref, kh_ref, vh_ref, wq_ref, wk_ref, wv_ref,
                    beta_ref, gc_ref, gct_ref, qo_ref, ko_ref, vo_ref, a_ref, scr):
    t = pl.program_id(1)
    h = pl.program_id(2)
    tt = q_ref.shape[0]

    def conv_silu(x_ref, halo_ref, w_ref):
        scr[0:HALO, :] = jnp.where(t > 0, halo_ref[...].astype(F32), 0.0)
        scr[HALO:, :] = x_ref[...].astype(F32)
        w = w_ref[...]
        y = scr[HALO - 3:HALO - 3 + tt, :] * w[0:1, :]
        for j in range(1, CONV_K):
            y = y + scr[HALO - 3 + j:HALO - 3 + j + tt, :] * w[j:j + 1, :]
        return y * jax.nn.sigmoid(y)

    def l2n(v):
        return v * lax.rsqrt(jnp.sum(v * v, axis=1, keepdims=True) + RMS_EPS)

    q = l2n(conv_silu(q_ref, qh_ref, wq_ref)) * (HEAD_DIM ** -0.5)
    qo_ref[...] = q.astype(qo_ref.dtype)
    k = l2n(conv_silu(k_ref, kh_ref, wk_ref)).astype(BF16)
    ko_ref[...] = k
    vo_ref[...] = conv_silu(v_ref, vh_ref, wv_ref).astype(vo_ref.dtype)

    bcol = _col_of(beta_ref[...], h)
    gcol = _col_of(gc_ref[...], B_HEADS + h)
    grow = gct_ref[pl.ds(h, 1), :]
    g2 = 2 * CHUNK
    ri = lax.broadcasted_iota(jnp.int32, (g2, g2), 0)
    ci = lax.broadcasted_iota(jnp.int32, (g2, g2), 1)
    strict = jnp.logical_and((ri >= CHUNK) == (ci >= CHUNK), ri > ci)
    lane = lax.broadcasted_iota(jnp.int32, (CHUNK, g2), 1)
    for gi in range(tt // g2):
        rs = slice(gi * g2, (gi + 1) * g2)
        kb = k[rs, :]
        kbeta = (kb.astype(F32) * bcol[rs, :]).astype(BF16)
        diff = gcol[rs, :] - grow[:, rs]
        a = _dot_nt(kbeta, kb) * jnp.exp(jnp.where(strict, diff, NEG))
        a_ref[gi] = jnp.where(lane < CHUNK, a[0:CHUNK, :], a[CHUNK:g2, :])


def _dn_prep(qkvb, conv_w, beta, gc, gct, tt):
    bsz, seq, _ = qkvb.shape
    nh = B_HEADS
    hb = tt // HALO

    def tok(off):
        return pl.BlockSpec((None, tt, HEAD_DIM), lambda b, t, h: (b, t, off + h))

    def halo(off):
        return pl.BlockSpec((None, HALO, HEAD_DIM), lambda b, t, h: (b, jnp.maximum(t * hb - 1, 0), off + h))

    def cw(off):
        return pl.BlockSpec((CONV_K, HEAD_DIM), lambda b, t, h: (0, off + h))

    full = pl.BlockSpec((None, tt, LANES), lambda b, t, h: (b, t, 0))
    gts = pl.BlockSpec((None, nh, tt), lambda b, t, h: (b, 0, t))
    ospec = pl.BlockSpec((None, tt, HEAD_DIM), lambda b, t, h: (b, t, h))
    aspec = pl.BlockSpec((None, None, tt // (2 * CHUNK), CHUNK, 2 * CHUNK), lambda b, t, h: (b, h, t, 0, 0))
    return pl.pallas_call(
        _dn_prep_kernel,
        grid=(bsz, seq // tt, nh),
        in_specs=[tok(0), tok(nh), tok(2 * nh), halo(0), halo(nh), halo(2 * nh),
                  cw(0), cw(nh), cw(2 * nh), full, full, gts],
        out_specs=[ospec, ospec, ospec, aspec],
        out_shape=[jax.ShapeDtypeStruct((bsz, seq, B_WIDTH), BF16)] * 3
        + [jax.ShapeDtypeStruct((bsz, nh, seq // (2 * CHUNK), CHUNK, 2 * CHUNK), F32)],
        scratch_shapes=[pltpu.VMEM((tt + HALO, HEAD_DIM), F32)],
        compiler_params=_cparams(("parallel", "parallel", "arbitrary")),
        name="deltanet_prep",
    )(qkvb, qkvb, qkvb, qkvb, qkvb, qkvb, conv_w, conv_w, conv_w, beta, gc, gct)


JB = 16


def _tri_solve_kernel(a_ref, t_ref):
    i = pl.program_id(1)
    for jb in range(CHUNK // JB):
        def body(m, acc):
            a = a_ref[m]
            return tuple(acc[jj] - a * t_ref[m * CHUNK + jb * JB + jj] for jj in range(JB))

        zero = jnp.zeros(a_ref.shape[1:], F32)
        acc = lax.fori_loop(jb * JB, i, body, tuple(zero for _ in range(JB)))
        for jj in range(JB):
            j = jb * JB + jj
            t_ref[i * CHUNK + j] = acc[jj] + jnp.where(i == j, 1.0, 0.0).astype(F32)


def _tri_solve(a_t):
    nn, ns, _ = a_t.shape
    sb = min(ns, 8)
    return pl.pallas_call(
        _tri_solve_kernel,
        grid=(ns // sb, CHUNK),
        in_specs=[pl.BlockSpec((CHUNK, sb, LANES), lambda s, i: (i, s, 0))],
        out_specs=pl.BlockSpec((nn, sb, LANES), lambda s, i: (0, s, 0)),
        out_shape=jax.ShapeDtypeStruct(a_t.shape, F32),
        compiler_params=_cparams(("parallel", "arbitrary")),
        name="deltanet_tri_solve",
    )(a_t)


def _dn_main_kernel(q_ref, k_ref, v_ref, beta_ref, gc_ref, gct_ref, t_ref, z_ref, nw_ref, o_ref, state):
    t = pl.program_id(2)
    h = pl.program_id(1)
    tt = q_ref.shape[0]
    g2 = 2 * CHUNK

    @pl.when(t == 0)
    def _():
        state[...] = jnp.zeros_like(state)

    bcol = _col_of(beta_ref[...], h)
    gcol = _col_of(gc_ref[...], B_HEADS + h)
    grow = gct_ref[pl.ds(h, 1), :]
    ri = lax.broadcasted_iota(jnp.int32, (g2, g2), 0)
    ci = lax.broadcasted_iota(jnp.int32, (g2, g2), 1)
    incl = jnp.logical_and((ri >= CHUNK) == (ci >= CHUNK), ri >= ci)
    lane = lax.broadcasted_iota(jnp.int32, (CHUNK, g2), 1)
    nw = nw_ref[...]
    s = state[...]
    for gi in range(tt // g2):
        rs = slice(gi * g2, (gi + 1) * g2)
        qb = q_ref[rs, :]
        kb = k_ref[rs, :]
        kf = kb.astype(F32)
        b = bcol[rs, :]
        gcl = gcol[rs, :]
        eg = jnp.exp(gcl)
        kbeta = kf * b
        rhs = jnp.concatenate([v_ref[rs, :].astype(F32) * b, kbeta * eg], axis=1).astype(BF16)
        tp = t_ref[gi]
        tbd = jnp.concatenate([jnp.where(lane < CHUNK, tp, 0.0), jnp.where(lane >= CHUNK, tp, 0.0)], axis=0)
        uw = _dot(tbd.astype(BF16), rhs)
        u = uw[:, :HEAD_DIM]
        w = uw[:, HEAD_DIM:].astype(BF16)
        diff = gcl - grow[:, rs]
        attn = (_dot_nt(qb, kb) * jnp.exp(jnp.where(incl, diff, NEG))).astype(BF16)
        qdec = (qb.astype(F32) * eg).astype(BF16)
        outs = []
        vnews = []
        for cc in range(2):
            cs = slice(cc * CHUNK, (cc + 1) * CHUNK)
            glast = gcl[cc * CHUNK + CHUNK - 1:cc * CHUNK + CHUNK, :]
            ktail = (kf[cs, :] * jnp.exp(glast - gcl[cs, :])).astype(BF16)
            sb = s.astype(BF16)
            vnew = u[cs, :] - _dot(w[cs, :], sb)
            vnews.append(vnew.astype(BF16))
            if cc == 0:
                intra = _dot(attn[cs, 0:CHUNK], vnews[0])
            else:
                intra = _dot(attn[cs, :], jnp.concatenate(vnews, axis=0))
            outs.append(_dot(qdec[cs, :], sb) + intra)
            s = s * jnp.exp(glast) + _dot_tn(ktail, vnews[cc])
        o = jnp.concatenate(outs, axis=0)
        o = o * lax.rsqrt(jnp.mean(o * o, axis=1, keepdims=True) + RMS_EPS) * nw
        z = z_ref[rs, :].astype(F32)
        o_ref[rs, :] = (o * (z * jax.nn.sigmoid(z))).astype(o_ref.dtype)
    state[...] = s


def _dn_main(qn, kn, vn, beta, gc, gct, t_p, zg, norm_w, tt):
    bsz, seq, _ = qn.shape
    nh = B_HEADS
    tok = pl.BlockSpec((None, tt, HEAD_DIM), lambda b, h, t: (b, t, h))
    full = pl.BlockSpec((None, tt, LANES), lambda b, h, t: (b, t, 0))
    gts = pl.BlockSpec((None, nh, tt), lambda b, h, t: (b, 0, t))
    tspec = pl.BlockSpec((None, None, tt // (2 * CHUNK), CHUNK, 2 * CHUNK), lambda b, h, t: (b, h, t, 0, 0))
    nspec = pl.BlockSpec((1, HEAD_DIM), lambda b, h, t: (0, 0))
    return pl.pallas_call(
        _dn_main_kernel,
        grid=(bsz, nh, seq // tt),
        in_specs=[tok, tok, tok, full, full, gts, tspec, tok, nspec],
        out_specs=tok,
        out_shape=jax.ShapeDtypeStruct((bsz, seq, B_WIDTH), BF16),
        scratch_shapes=[pltpu.VMEM((HEAD_DIM, HEAD_DIM), F32)],
        compiler_params=_cparams(("parallel", "parallel", "arbitrary")),
        name="deltanet_main",
    )(qn, kn, vn, beta, gc, gct, t_p, zg, norm_w)


def _mix_out_kernel(o0_ref, o1_ref, o2_ref, l0_ref, l1_ref, l2_ref, yb_ref, ga_ref, gb_ref, x_ref,
                    wa_ref, wb_ref, wo_ref, g_ref, b_ref, xo_ref, xbo_ref):
    l0, l1, l2 = l0_ref[...], l1_ref[...], l2_ref[...]
    m = jnp.maximum(jnp.maximum(l0, l1), l2)
    e0, e1, e2 = jnp.exp(l0 - m), jnp.exp(l1 - m), jnp.exp(l2 - m)
    ya = (e0 * o0_ref[...].astype(F32) + e1 * o1_ref[...].astype(F32) + e2 * o2_ref[...].astype(F32)) / (e0 + e1 + e2)
    ma = _dot(ya.astype(BF16), wa_ref[...])
    mb = _dot(yb_ref[...], wb_ref[...])
    merged = jax.nn.sigmoid(ga_ref[...].astype(F32)) * ma + jax.nn.sigmoid(gb_ref[...].astype(F32)) * mb
    r = _dot(merged.astype(BF16), wo_ref[...])
    y = _layer_norm(DN_ALPHA * x_ref[...] + r, g_ref[...], b_ref[...])
    xo_ref[...] = y
    xbo_ref[...] = y.astype(BF16)


def _mix_out(o_list, l_list, yb, zg, x, wa, wb, wo, g, b, tm=512):
    t = x.shape[0]
    d = D_MODEL
    aw = A_GROUP_WIDTH

    def rows(wd, c=0):
        return pl.BlockSpec((tm, wd), lambda i: (i, c))

    def whole(shape):
        return pl.BlockSpec(shape, lambda i: (0, 0))

    return pl.pallas_call(
        _mix_out_kernel,
        grid=(t // tm,),
        in_specs=[rows(aw)] * 6 + [rows(d), rows(d, 1), rows(d, 2), rows(d),
                                   whole((aw, d)), whole((d, d)), whole((d, d)), whole((1, d)), whole((1, d))],
        out_specs=[rows(d), rows(d)],
        out_shape=[jax.ShapeDtypeStruct((t, d), F32), jax.ShapeDtypeStruct((t, d), BF16)],
        compiler_params=_cparams(("parallel",)),
        name="mix_out",
    )(*o_list, *l_list, yb, zg, zg, x, wa, wb, wo, g, b)


def _ffn_kernel(xb_ref, x_ref, wg_ref, wu_ref, wd_ref, g_ref, b_ref, xo_ref, xbo_ref, acc):
    f = pl.program_id(1)

    @pl.when(f == 0)
    def _():
        acc[...] = jnp.zeros_like(acc)

    xb = xb_ref[...]
    gt = _dot(xb, wg_ref[...])
    up = _dot(xb, wu_ref[...])
    hh = (gt * jax.nn.sigmoid(gt) * up).astype(BF16)
    acc[...] += _dot(hh, wd_ref[...])

    @pl.when(f == pl.num_programs(1) - 1)
    def _():
        y = _layer_norm(DN_ALPHA * x_ref[...] + acc[...], g_ref[...], b_ref[...])
        xo_ref[...] = y
        xbo_ref[...] = y.astype(BF16)


def _ffn(xb, x, wg, wu, wd, g, b, tm=1024, tf=256):
    t, d = x.shape
    dff = wg.shape[1]
    rows = pl.BlockSpec((tm, d), lambda i, f: (i, 0))
    vec = pl.BlockSpec((1, d), lambda i, f: (0, 0))
    return pl.pallas_call(
        _ffn_kernel,
        grid=(t // tm, dff // tf),
        in_specs=[rows, rows,
                  pl.BlockSpec((d, tf), lambda i, f: (0, f)),
                  pl.BlockSpec((d, tf), lambda i, f: (0, f)),
                  pl.BlockSpec((tf, d), lambda i, f: (f, 0)), vec, vec],
        out_specs=[rows, rows],
        out_shape=[jax.ShapeDtypeStruct((t, d), F32), jax.ShapeDtypeStruct((t, d), BF16)],
        scratch_shapes=[pltpu.VMEM((tm, d), F32)],
        compiler_params=_cparams(("parallel", "arbitrary")),
        name="dense_swiglu",
    )(xb, x, wg, wu, wd, g, b)


def _router_kernel(x_ref, rw_ref, tri_ref, gates_ref, rank_ref, selt_ref, cnt_ref):
    logits = jnp.dot(x_ref[...], rw_ref[...], precision=lax.Precision.HIGHEST, preferred_element_type=F32)
    lane = lax.broadcasted_iota(jnp.int32, logits.shape, 1)
    lanef = lane.astype(F32)
    ninf = -jnp.inf
    lg = jnp.where(lane < N_EXPERTS, logits, ninf)
    m1 = jnp.max(lg, axis=1, keepdims=True)
    i1 = jnp.min(jnp.where(lg == m1, lanef, float(LANES)), axis=1, keepdims=True)
    lg2 = jnp.where(lanef == i1, ninf, lg)
    m2 = jnp.max(lg2, axis=1, keepdims=True)
    i2 = jnp.min(jnp.where(lg2 == m2, lanef, float(LANES)), axis=1, keepdims=True)
    e = jnp.exp(m2 - m1)
    w1 = 1.0 / (1.0 + e)
    w2 = e / (1.0 + e)
    gates = jnp.where(lanef == i1, w1, jnp.where(lanef == i2, w2, 0.0))
    gates_ref[...] = gates
    sel = gates > 0.0
    onef = jnp.where(sel, 1.0, 0.0)
    rank = _dot(tri_ref[...], onef.astype(BF16))
    rank_ref[...] = rank
    selt_ref[...] = jnp.where(sel, rank, -1.0).T[0:N_EXPERTS, :]
    cnt_ref[...] = jnp.broadcast_to(jnp.sum(onef, axis=0, keepdims=True), cnt_ref.shape).astype(jnp.int32)


def _router(x, rw_pad, tm):
    t, d = x.shape
    nt = t // tm
    tri = jnp.tril(jnp.ones((tm, tm), BF16), -1)
    rows = pl.BlockSpec((tm, LANES), lambda i: (i, 0))
    return pl.pallas_call(
        _router_kernel,
        grid=(nt,),
        in_specs=[pl.BlockSpec((tm, d), lambda i: (i, 0)), pl.BlockSpec((d, LANES), lambda i: (0, 0)),
                  pl.BlockSpec((tm, tm), lambda i: (0, 0))],
        out_specs=[rows, rows, pl.BlockSpec((None, N_EXPERTS, tm), lambda i: (i, 0, 0)),
                   pl.BlockSpec((None, 8, LANES), lambda i: (i, 0, 0))],
        out_shape=[jax.ShapeDtypeStruct((t, LANES), F32), jax.ShapeDtypeStruct((t, LANES), F32),
                   jax.ShapeDtypeStruct((nt, N_EXPERTS, tm), F32), jax.ShapeDtypeStruct((nt, 8, LANES), jnp.int32)],
        compiler_params=_cparams(("parallel",)),
        name="moe_router",
    )(x, rw_pad, tri)


MOE_TM = 1024
MOE_CAP = 320
MOE_FC = 512


def _expert_kernel(xb_ref, selt_ref, wg_ref, wu_ref, wd_ref, oc_ref):
    e = pl.program_id(0)
    tm = xb_ref.shape[0]
    sel_row = selt_ref[pl.ds(e, 1), :]
    cidx = lax.broadcasted_iota(jnp.int32, (MOE_CAP, tm), 0).astype(F32)
    onehot = jnp.where(sel_row == cidx, 1.0, 0.0).astype(BF16)
    xc = _dot(onehot, xb_ref[...]).astype(BF16)
    acc = jnp.zeros((MOE_CAP, D_MODEL), F32)
    for c in range(wg_ref.shape[1] // MOE_FC):
        cs = slice(c * MOE_FC, (c + 1) * MOE_FC)
        gt = _dot(xc, wg_ref[:, cs])
        up = _dot(xc, wu_ref[:, cs])
        hh = (gt * jax.nn.sigmoid(gt) * up).astype(BF16)
        acc = acc + _dot(hh, wd_ref[cs, :])
    oc_ref[...] = acc.astype(oc_ref.dtype)


def _experts(xb, selt, wg, wu, wd):
    t, d = xb.shape
    ne, _, dex = wg.shape
    nt = t // MOE_TM
    one = pl.Buffered(1)
    return pl.pallas_call(
        _expert_kernel,
        grid=(ne, nt),
        in_specs=[pl.BlockSpec((MOE_TM, d), lambda e, i: (i, 0)),
                  pl.BlockSpec((None, N_EXPERTS, MOE_TM), lambda e, i: (i, 0, 0)),
                  pl.BlockSpec((None, d, dex), lambda e, i: (e, 0, 0), pipeline_mode=one),
                  pl.BlockSpec((None, d, dex), lambda e, i: (e, 0, 0), pipeline_mode=one),
                  pl.BlockSpec((None, dex, d), lambda e, i: (e, 0, 0), pipeline_mode=one)],
        out_specs=pl.BlockSpec((None, None, MOE_CAP, d), lambda e, i: (e, i, 0, 0)),
        out_shape=jax.ShapeDtypeStruct((ne, nt, MOE_CAP, d), BF16),
        compiler_params=_cparams(("arbitrary", "arbitrary")),
        name="moe_experts",
    )(xb, selt, wg, wu, wd)


def _combine_kernel(x_ref, gates_ref, rank_ref, oc_ref, g_ref, b_ref, xo_ref):
    tm = x_ref.shape[0]
    gates = gates_ref[...]
    rank = rank_ref[...]
    lane = lax.broadcasted_iota(jnp.int32, (tm, MOE_CAP), 1).astype(F32)
    y = jnp.zeros((tm, D_MODEL), F32)
    for e in range(N_EXPERTS):
        gcol = gates[:, e:e + 1]
        scol = jnp.where(gcol > 0.0, rank[:, e:e + 1], -1.0)
        onehot = jnp.where(scol == lane, 1.0, 0.0).astype(BF16)
        y = y + _dot(onehot, oc_ref[e]) * gcol
    xo_ref[...] = _layer_norm(DN_ALPHA * x_ref[...] + y, g_ref[...], b_ref[...])


def _combine(x, gates, rank, oc, g, b):
    t, d = x.shape
    ne = oc.shape[0]
    rows = pl.BlockSpec((MOE_TM, d), lambda i: (i, 0))
    lrows = pl.BlockSpec((MOE_TM, LANES), lambda i: (i, 0))
    vec = pl.BlockSpec((1, d), lambda i: (0, 0))
    return pl.pallas_call(
        _combine_kernel,
        grid=(t // MOE_TM,),
        in_specs=[rows, lrows, lrows, pl.BlockSpec((ne, None, MOE_CAP, d), lambda i: (0, i, 0, 0)), vec, vec],
        out_specs=rows,
        out_shape=jax.ShapeDtypeStruct((t, d), F32),
        compiler_params=_cparams(("parallel",)),
        name="moe_combine",
    )(x, gates, rank, oc, g, b)


def _moe_kernel(xb_ref, x_ref, gates_ref, wg_ref, wu_ref, wd_ref, g_ref, b_ref, xo_ref, acc):
    e = pl.program_id(1)
    f = pl.program_id(2)

    @pl.when(jnp.logical_and(e == 0, f == 0))
    def _():
        acc[...] = jnp.zeros_like(acc)

    xb = xb_ref[...]
    gate = _col_of(gates_ref[...], e)
    gt = _dot(xb, wg_ref[...])
    up = _dot(xb, wu_ref[...])
    hh = (gt * jax.nn.sigmoid(gt) * up * gate).astype(BF16)
    acc[...] += _dot(hh, wd_ref[...])

    @pl.when(jnp.logical_and(e == pl.num_programs(1) - 1, f == pl.num_programs(2) - 1))
    def _():
        xo_ref[...] = _layer_norm(DN_ALPHA * x_ref[...] + acc[...], g_ref[...], b_ref[...])


def _moe(xb, x, gates, wg, wu, wd, g, b, tm=1024, tf=512):
    t, d = x.shape
    ne, _, dex = wg.shape
    rows = pl.BlockSpec((tm, d), lambda i, e, f: (i, 0))
    vec = pl.BlockSpec((1, d), lambda i, e, f: (0, 0))
    return pl.pallas_call(
        _moe_kernel,
        grid=(t // tm, ne, dex // tf),
        in_specs=[rows, rows, pl.BlockSpec((tm, LANES), lambda i, e, f: (i, 0)),
                  pl.BlockSpec((None, d, tf), lambda i, e, f: (e, 0, f)),
                  pl.BlockSpec((None, d, tf), lambda i, e, f: (e, 0, f)),
                  pl.BlockSpec((None, tf, d), lambda i, e, f: (e, f, 0)), vec, vec],
        out_specs=rows,
        out_shape=jax.ShapeDtypeStruct((t, d), F32),
        scratch_shapes=[pltpu.VMEM((tm, d), F32)],
        compiler_params=_cparams(("parallel", "arbitrary", "arbitrary")),
        name="moe_swiglu",
    )(xb, x, gates, wg, wu, wd, g, b)


def _to_sub(a, bsz, seq, dil):
    w = a.shape[-1]
    return a.reshape(bsz, seq // dil, dil, w).transpose(0, 2, 1, 3)


def _from_sub(a):
    bsz, dil, sub_len, w = a.shape
    return a.transpose(0, 2, 1, 3).reshape(bsz * sub_len * dil, w)


def _rope_tables(positions):
    half = ROT_DIM // 2
    inv_freq = ROPE_THETA ** (-jnp.arange(0, ROT_DIM, 2, dtype=F32) / ROT_DIM)
    ang = positions.astype(F32)[..., None] * inv_freq
    cos, sin = jnp.cos(ang), jnp.sin(ang)
    shp = cos.shape[:-1]
    c = jnp.concatenate([cos, cos, jnp.ones(shp + (LANES - ROT_DIM,), F32)], axis=-1)
    s1 = jnp.concatenate([-sin, jnp.zeros(shp + (LANES - half,), F32)], axis=-1)
    s2 = jnp.concatenate([jnp.zeros(shp + (half,), F32), sin, jnp.zeros(shp + (LANES - ROT_DIM,), F32)], axis=-1)
    return c, s1, s2


def kernel(x, positions, w_in, conv_w, a_log, dt_bias, dn_norm_w, w_branch_a, w_branch_b, w_out, ln1_g, ln1_b,
           ffn_w_gate, ffn_w_up, ffn_w_down, router_w, moe_w_gate, moe_w_up, moe_w_down, ln2_g, ln2_b):
    bsz, seq, d = x.shape
    t = bsz * seq
    nh = B_HEADS
    qa_w = 3 * A_QKV_WIDTH
    o_qkvb = qa_w
    o_z = o_qkvb + 3 * B_WIDTH
    o_bd = o_z + B_WIDTH
    o_gates = o_bd + 2 * nh

    tabs = _rope_tables(positions)
    tabs_g = []
    for _, dil in A_PAIRS:
        tabs_g.append([_to_sub(tb.reshape(t, LANES), bsz, seq, dil) for tb in tabs])

    dn_tt = 512
    xf = x.reshape(t, d)
    xb = xf.astype(BF16)
    for layer in range(DEPTH):
        w = w_in[layer]
        w_qkva = w[:, :qa_w].astype(BF16)
        w_qkvb = w[:, o_qkvb:o_z].astype(BF16)
        w_zg = jnp.concatenate([w[:, o_z:o_bd], w[:, o_gates:]], axis=1).astype(BF16)
        w_bd = jnp.pad(w[:, o_bd:o_gates], ((0, 0), (0, LANES - 2 * nh))).astype(BF16)

        qkva = _matmul(xb, w_qkva, BF16, 1024, 1536)
        qkvb = _matmul(xb, w_qkvb, BF16, 1024, 1024)
        zg = _matmul(xb, w_zg, BF16, 1024, 1024)
        bd = _matmul(xb, w_bd, F32, 1024, LANES)

        o_list, l_list = [], []
        for g, (_, dil) in enumerate(A_PAIRS):
            if dil == 1:
                arr = qkva.reshape(bsz, 1, seq, qa_w)
                o_g, l_g = _attention_group(arr, arr, arr, g, 3 + g, 6 + g, tabs_g[g])
            else:
                parts = [_to_sub(qkva[:, (3 * p + g) * A_GROUP_WIDTH:(3 * p + g + 1) * A_GROUP_WIDTH], bsz, seq, dil)
                         for p in range(3)]
                o_g, l_g = _attention_group(parts[0], parts[1], parts[2], 0, 0, 0, tabs_g[g])
            o_list.append(_from_sub(o_g))
            l_list.append(_from_sub(l_g))

        zeros8 = jnp.zeros((nh,), F32)
        arow = jnp.concatenate([zeros8, -jnp.exp(a_log[layer].astype(F32)), jnp.zeros((LANES - 2 * nh,), F32)])[None, :]
        dtrow = jnp.concatenate([zeros8, dt_bias[layer].astype(F32), jnp.zeros((LANES - 2 * nh,), F32)])[None, :]
        beta, gc = _gates(bd, arow, dtrow)
        beta3 = beta.reshape(bsz, seq, LANES)
        gc3 = gc.reshape(bsz, seq, LANES)
        gct = gc3[:, :, nh:2 * nh].transpose(0, 2, 1)
        qn, kn, vn, a_p = _dn_prep(qkvb.reshape(bsz, seq, 3 * B_WIDTH), conv_w[layer].astype(F32),
                                   beta3, gc3, gct, dn_tt)
        ng = seq // (2 * CHUNK)
        nch = bsz * nh * ng * 2
        a_t = a_p.reshape(bsz, nh, ng, CHUNK, 2, CHUNK).transpose(3, 5, 0, 1, 2, 4).reshape(CHUNK * CHUNK, nch // LANES, LANES)
        t_t = _tri_solve(a_t)
        t_p = t_t.reshape(CHUNK, CHUNK, bsz, nh, ng, 2).transpose(2, 3, 4, 0, 5, 1).reshape(bsz, nh, ng, CHUNK, 2 * CHUNK)
        yb = _dn_main(qn, kn, vn, beta3, gc3, gct, t_p, zg.reshape(bsz, seq, 3 * B_WIDTH),
                      dn_norm_w[layer].astype(F32)[None, :], dn_tt)

        xf, xb = _mix_out(o_list, l_list, yb.reshape(t, B_WIDTH), zg, xf,
                          w_branch_a[layer].astype(BF16), w_branch_b[layer].astype(BF16), w_out[layer].astype(BF16),
                          ln1_g[layer].astype(F32)[None, :], ln1_b[layer].astype(F32)[None, :])

        g2 = ln2_g[layer].astype(F32)[None, :]
        b2 = ln2_b[layer].astype(F32)[None, :]
        if layer % 2 == 0:
            i = layer // 2
            xf, xb = _ffn(xb, xf, ffn_w_gate[i].astype(BF16), ffn_w_up[i].astype(BF16), ffn_w_down[i].astype(BF16), g2, b2)
        else:
            i = layer // 2
            rw = jnp.pad(router_w[i].astype(F32), ((0, 0), (0, LANES - N_EXPERTS)))
            gates, rank, selt, cnt = _router(xf, rw, MOE_TM)
            wg = moe_w_gate[i].astype(BF16)
            wu = moe_w_up[i].astype(BF16)
            wd = moe_w_down[i].astype(BF16)

            def routed(xb, xf, gates, rank, selt):
                return _combine(xf, gates, rank, _experts(xb, selt, wg, wu, wd), g2, b2)

            def dense(xb, xf, gates, rank, selt):
                return _moe(xb, xf, gates, wg, wu, wd, g2, b2)

            xf = lax.cond(jnp.max(cnt) > MOE_CAP, dense, routed, xb, xf, gates, rank, selt)
            xb = xf.astype(BF16)
    return xf.reshape(bsz, seq, d)
```

```python
import functools
import math

import jax
import jax.numpy as jnp
from jax import lax
from jax.experimental import pallas as pl
from jax.experimental.pallas import tpu as pltpu

F32 = jnp.float32
BF16 = jnp.bfloat16

D_MODEL = 1024
DEPTH = 2
A_PAIRS = ((128, 1), (512, 4), (2048, 16))
A_HEADS = 4
HEAD_DIM = 128
A_GROUP_WIDTH = A_HEADS * HEAD_DIM
A_QKV_WIDTH = len(A_PAIRS) * A_GROUP_WIDTH
A_BLOCK = 128
ROPE_THETA = 500000.0
ROT_DIM = HEAD_DIM // 4
B_HEADS = 8
B_WIDTH = B_HEADS * HEAD_DIM
CONV_K = 4
CHUNK = 64
N_EXPERTS = 8
DN_ALPHA = (2 * DEPTH) ** 0.25
LN_EPS = 1e-5
RMS_EPS = 1e-6
NEG = -1e30
LANES = 128
VMEM_LIMIT = 56 * 1024 * 1024


def _cparams(sem):
    return pltpu.CompilerParams(dimension_semantics=sem, vmem_limit_bytes=VMEM_LIMIT)


def _dot(a, b):
    return jnp.dot(a, b, preferred_element_type=F32)


def _dot_nt(a, b):
    return lax.dot_general(a, b, (((1,), (1,)), ((), ())), preferred_element_type=F32)


def _dot_tn(a, b):
    return lax.dot_general(a, b, (((0,), (0,)), ((), ())), preferred_element_type=F32)


def _col_of(x, idx):
    lane = lax.broadcasted_iota(jnp.int32, x.shape, 1)
    return jnp.sum(jnp.where(lane == idx, x, 0.0), axis=1, keepdims=True)


def _layer_norm(v, g, b):
    mu = jnp.mean(v, axis=-1, keepdims=True)
    c = v - mu
    var = jnp.mean(c * c, axis=-1, keepdims=True)
    return c * lax.rsqrt(var + LN_EPS) * g + b


def _mm_kernel(x_ref, w_ref, o_ref):
    o_ref[...] = _dot(x_ref[...], w_ref[...]).astype(o_ref.dtype)


def _matmul(x, w, out_dtype, tm, tn):
    m, k = x.shape
    n = w.shape[1]
    return pl.pallas_call(
        _mm_kernel,
        grid=(m // tm, n // tn),
        in_specs=[pl.BlockSpec((tm, k), lambda i, j: (i, 0)),
                  pl.BlockSpec((k, tn), lambda i, j: (0, j))],
        out_specs=pl.BlockSpec((tm, tn), lambda i, j: (i, j)),
        out_shape=jax.ShapeDtypeStruct((m, n), out_dtype),
        compiler_params=_cparams(("parallel", "arbitrary")),
        name="proj_matmul",
    )(x, w)


def _rope(t, c, s1, s2):
    return t * c + pltpu.roll(t, LANES - ROT_DIM // 2, 1) * s1 + pltpu.roll(t, ROT_DIM // 2, 1) * s2


N_GROUPS = len(A_PAIRS)
DILS = tuple(d for _, d in A_PAIRS)


def _proj_a_kernel(x_ref, w_ref, c_ref, s1_ref, s2_ref, *refs):
    outs = refs[:3 * N_GROUPS]
    scr = refs[3 * N_GROUPS]
    j = pl.program_id(1)
    tm = x_ref.shape[0]
    acc = _dot(x_ref[...], w_ref[...])
    scale = 1.0 / math.sqrt(HEAD_DIM)
    for jj in range(3 * N_GROUPS):
        p, g = divmod(jj, N_GROUPS)
        dil = DILS[g]

        @pl.when(j == jj)
        def _(p=p, dil=dil, o_ref=outs[jj]):
            for h in range(A_HEADS):
                sl = slice(h * HEAD_DIM, (h + 1) * HEAD_DIM)
                t = acc[:, sl]
                if p < 2:
                    t = _rope(t, c_ref[...], s1_ref[...], s2_ref[...])
                if p == 0:
                    t = t * scale
                if dil == 1:
                    o_ref[:, sl] = t.astype(o_ref.dtype)
                else:
                    scr[h] = t
                    for r in range(dil):
                        o_ref[r, :, sl] = scr[h, pl.ds(r, tm // dil, stride=dil), :].astype(o_ref.dtype)


def _proj_a(xb, w_qkva, tabs, bsz, seq, tm=1024):
    t, k = xb.shape
    npb = seq // tm
    w = A_GROUP_WIDTH
    out_specs, out_shape = [], []
    for jj in range(3 * N_GROUPS):
        dil = DILS[jj % N_GROUPS]
        if dil == 1:
            out_specs.append(pl.BlockSpec((None, None, tm, w), lambda i, j: (i // npb, 0, i % npb, 0)))
        else:
            out_specs.append(pl.BlockSpec((None, dil, tm // dil, w), lambda i, j: (i // npb, 0, i % npb, 0)))
        out_shape.append(jax.ShapeDtypeStruct((bsz, dil, seq // dil, w), BF16))
    tab = pl.BlockSpec((tm, LANES), lambda i, j: (i, 0))
    return pl.pallas_call(
        _proj_a_kernel,
        grid=(t // tm, 3 * N_GROUPS),
        in_specs=[pl.BlockSpec((tm, k), lambda i, j: (i, 0)), pl.BlockSpec((k, w), lambda i, j: (0, j)), tab, tab, tab],
        out_specs=out_specs,
        out_shape=out_shape,
        scratch_shapes=[pltpu.VMEM((A_HEADS, tm, HEAD_DIM), F32)],
        compiler_params=_cparams(("parallel", "arbitrary")),
        name="proj_mixer_a",
    )(xb, w_qkva, *tabs)


ATT_TT = 2048
ATT_UNROLL = 4


def _attn_kernel(*refs):
    ins = refs[:5 * N_GROUPS]
    o_ref = refs[5 * N_GROUPS]
    od, ld, on, ln = refs[5 * N_GROUPS + 1:]
    it = pl.program_id(1)
    row = lax.broadcasted_iota(jnp.int32, (A_BLOCK, A_BLOCK), 0)
    col = lax.broadcasted_iota(jnp.int32, (A_BLOCK, A_BLOCK), 1)
    mask_c = col <= row
    mask_p = col >= row
    nblocks = ATT_TT // A_BLOCK
    for g, dil in enumerate(DILS):
        q_ref, k_ref, v_ref, kh_ref, vh_ref = ins[5 * g:5 * g + 5]
        nlb = nblocks // dil
        rows = ATT_TT // dil

        def block(c, carry, g=g, nlb=nlb, rows=rows, q_ref=q_ref, k_ref=k_ref, v_ref=v_ref, kh_ref=kh_ref, vh_ref=vh_ref):
            r = c // nlb
            nb = c % nlb
            off = pl.multiple_of(nb * A_BLOCK, A_BLOCK)
            q = q_ref[r, pl.ds(off, A_BLOCK), :]
            kc = k_ref[r, pl.ds(off, A_BLOCK), :]
            vc = v_ref[r, pl.ds(off, A_BLOCK), :]
            if nlb == 1:
                kp, vp = kh_ref[r], vh_ref[r]
                valid = it > 0
            else:
                poff = pl.multiple_of(jnp.maximum(nb - 1, 0) * A_BLOCK, A_BLOCK)
                first = nb == 0
                kp = jnp.where(first, kh_ref[r], k_ref[r, pl.ds(poff, A_BLOCK), :])
                vp = jnp.where(first, vh_ref[r], v_ref[r, pl.ds(poff, A_BLOCK), :])
                valid = jnp.logical_or(nb > 0, it > 0)
            s_c = jnp.where(mask_c, _dot_nt(q, kc), NEG)
            s_p = jnp.where(jnp.logical_and(mask_p, valid), _dot_nt(q, kp), NEG)
            m = jnp.maximum(jnp.max(s_c, axis=1, keepdims=True), jnp.max(s_p, axis=1, keepdims=True))
            p_c = jnp.exp(s_c - m)
            p_p = jnp.exp(s_p - m)
            den = jnp.sum(p_c, axis=1, keepdims=True) + jnp.sum(p_p, axis=1, keepdims=True)
            o = (_dot(p_c.astype(BF16), vc) + _dot(p_p.astype(BF16), vp)) / den
            dst = pl.multiple_of(r * rows + off, A_BLOCK)
            od[g, pl.ds(dst, A_BLOCK), :] = o
            ld[g, pl.ds(dst, A_BLOCK), :] = jnp.broadcast_to(m + jnp.log(den), (A_BLOCK, HEAD_DIM))
            return carry

        lax.fori_loop(0, nblocks, block, 0, unroll=ATT_UNROLL)
    for g, dil in enumerate(DILS):
        if dil == 1:
            continue
        rows = ATT_TT // dil
        for r in range(dil):
            on[g - 1, pl.ds(r, rows, stride=dil), :] = od[g, r * rows:(r + 1) * rows, :]
            ln[g - 1, pl.ds(r, rows, stride=dil), :] = ld[g, r * rows:(r + 1) * rows, :]
    step = 256
    for c in range(ATT_TT // step):
        sl = slice(c * step, (c + 1) * step)
        lse = [ld[0, sl, :]] + [ln[g - 1, sl, :] for g in range(1, N_GROUPS)]
        outs = [od[0, sl, :]] + [on[g - 1, sl, :] for g in range(1, N_GROUPS)]
        m = functools.reduce(jnp.maximum, lse)
        es = [jnp.exp(l - m) for l in lse]
        num = functools.reduce(lambda a, b: a + b, [e * o for e, o in zip(es, outs)])
        o_ref[sl, :] = (num / functools.reduce(lambda a, b: a + b, es)).astype(o_ref.dtype)


def _attention(qkv_sub, bsz, seq):
    specs, args = [], []
    for g, dil in enumerate(DILS):
        rows = ATT_TT // dil
        hb = rows // A_BLOCK
        cur = pl.BlockSpec((None, dil, rows, HEAD_DIM), lambda b, i, h: (b, 0, i, h))
        halo = pl.BlockSpec((None, dil, A_BLOCK, HEAD_DIM), lambda b, i, h, hb=hb: (b, 0, jnp.maximum(i * hb - 1, 0), h))
        specs += [cur, cur, cur, halo, halo]
        args += [qkv_sub[0][g], qkv_sub[1][g], qkv_sub[2][g], qkv_sub[1][g], qkv_sub[2][g]]
    return pl.pallas_call(
        _attn_kernel,
        grid=(bsz, seq // ATT_TT, A_HEADS),
        in_specs=specs,
        out_specs=pl.BlockSpec((None, ATT_TT, HEAD_DIM), lambda b, i, h: (b, i, h)),
        out_shape=jax.ShapeDtypeStruct((bsz, seq, A_GROUP_WIDTH), BF16),
        scratch_shapes=[pltpu.VMEM((N_GROUPS, ATT_TT, HEAD_DIM), F32)] * 2
        + [pltpu.VMEM((N_GROUPS - 1, ATT_TT, HEAD_DIM), F32)] * 2,
        compiler_params=_cparams(("parallel", "parallel", "parallel")),
        name="dilated_attention",
    )(*args)


def _gate_kernel(bd_ref, arow_ref, dtrow_ref, beta_ref, gc_ref):
    x = bd_ref[...]
    beta_ref[...] = jax.nn.sigmoid(x)
    y = x + dtrow_ref[...]
    softplus = jnp.maximum(y, 0.0) + jnp.log(1.0 + jnp.exp(-jnp.abs(y)))
    g = arow_ref[...] * softplus
    tt = x.shape[0]
    ri = lax.broadcasted_iota(jnp.int32, (tt, tt), 0)
    ci = lax.broadcasted_iota(jnp.int32, (tt, tt), 1)
    tri = jnp.where(jnp.logical_and(ri // CHUNK == ci // CHUNK, ci <= ri), 1.0, 0.0).astype(F32)
    gc_ref[...] = jnp.dot(tri, g, precision=lax.Precision.HIGHEST, preferred_element_type=F32)


def _gates(bd, arow, dtrow, tt=512):
    t = bd.shape[0]
    spec = pl.BlockSpec((tt, LANES), lambda i: (i, 0))
    rspec = pl.BlockSpec((1, LANES), lambda i: (0, 0))
    return pl.pallas_call(
        _gate_kernel,
        grid=(t // tt,),
        in_specs=[spec, rspec, rspec],
        out_specs=[spec, spec],
        out_shape=[jax.ShapeDtypeStruct((t, LANES), F32)] * 2,
        compiler_params=_cparams(("parallel",)),
        name="deltanet_gates",
    )(bd, arow, dtrow)


HALO = 16


def _dn_prep_kernel(q_ref, k_ref, v_ref, qh_ref, kh_ref, vh_ref, wq_ref, wk_ref, wv_ref,
                    beta_ref, gc_ref, gct_ref, qo_ref, ko_ref, vo_ref, a_ref, scr):
    t = pl.program_id(1)
    h = pl.program_id(2)
    tt = q_ref.shape[0]

    def conv_silu(x_ref, halo_ref, w_ref):
        scr[0:HALO, :] = jnp.where(t > 0, halo_ref[...].astype(F32), 0.0)
        scr[HALO:, :] = x_ref[...].astype(F32)
        w = w_ref[...]
        y = scr[HALO - 3:HALO - 3 + tt, :] * w[0:1, :]
        for j in range(1, CONV_K):
            y = y + scr[HALO - 3 + j:HALO - 3 + j + tt, :] * w[j:j + 1, :]
        return y * jax.nn.sigmoid(y)

    def l2n(v):
        return v * lax.rsqrt(jnp.sum(v * v, axis=1, keepdims=True) + RMS_EPS)

    q = l2n(conv_silu(q_ref, qh_ref, wq_ref)) * (HEAD_DIM ** -0.5)
    qo_ref[...] = q.astype(qo_ref.dtype)
    k = l2n(conv_silu(k_ref, kh_ref, wk_ref)).astype(BF16)
    ko_ref[...] = k
    vo_ref[...] = conv_silu(v_ref, vh_ref, wv_ref).astype(vo_ref.dtype)

    bcol = _col_of(beta_ref[...], h)
    gcol = _col_of(gc_ref[...], B_HEADS + h)
    grow = gct_ref[pl.ds(h, 1), :]
    g2 = 2 * CHUNK
    ri = lax.broadcasted_iota(jnp.int32, (g2, g2), 0)
    ci = lax.broadcasted_iota(jnp.int32, (g2, g2), 1)
    strict = jnp.logical_and((ri >= CHUNK) == (ci >= CHUNK), ri > ci)
    lane = lax.broadcasted_iota(jnp.int32, (CHUNK, g2), 1)
    for gi in range(tt // g2):
        rs = slice(gi * g2, (gi + 1) * g2)
        kb = k[rs, :]
        kbeta = (kb.astype(F32) * bcol[rs, :]).astype(BF16)
        diff = gcol[rs, :] - grow[:, rs]
        a = _dot_nt(kbeta, kb) * jnp.exp(jnp.where(strict, diff, NEG))
        a_ref[gi] = jnp.where(lane < CHUNK, a[0:CHUNK, :], a[CHUNK:g2, :])


def _dn_prep(qkvb, conv_w, beta, gc, gct, tt):
    bsz, seq, _ = qkvb.shape
    nh = B_HEADS
    hb = tt // HALO

    def tok(off):
        return pl.BlockSpec((None, tt, HEAD_DIM), lambda b, t, h: (b, t, off + h))

    def halo(off):
        return pl.BlockSpec((None, HALO, HEAD_DIM), lambda b, t, h: (b, jnp.maximum(t * hb - 1, 0), off + h))

    def cw(off):
        return pl.BlockSpec((CONV_K, HEAD_DIM), lambda b, t, h: (0, off + h))

    full = pl.BlockSpec((None, tt, LANES), lambda b, t, h: (b, t, 0))
    gts = pl.BlockSpec((None, nh, tt), lambda b, t, h: (b, 0, t))
    ospec = pl.BlockSpec((None, tt, HEAD_DIM), lambda b, t, h: (b, t, h))
    aspec = pl.BlockSpec((None, None, tt // (2 * CHUNK), CHUNK, 2 * CHUNK), lambda b, t, h: (b, h, t, 0, 0))
    return pl.pallas_call(
        _dn_prep_kernel,
        grid=(bsz, seq // tt, nh),
        in_specs=[tok(0), tok(nh), tok(2 * nh), halo(0), halo(nh), halo(2 * nh),
                  cw(0), cw(nh), cw(2 * nh), full, full, gts],
        out_specs=[ospec, ospec, ospec, aspec],
        out_shape=[jax.ShapeDtypeStruct((bsz, seq, B_WIDTH), BF16)] * 3
        + [jax.ShapeDtypeStruct((bsz, nh, seq // (2 * CHUNK), CHUNK, 2 * CHUNK), F32)],
        scratch_shapes=[pltpu.VMEM((tt + HALO, HEAD_DIM), F32)],
        compiler_params=_cparams(("parallel", "parallel", "arbitrary")),
        name="deltanet_prep",
    )(qkvb, qkvb, qkvb, qkvb, qkvb, qkvb, conv_w, conv_w, conv_w, beta, gc, gct)


SOLVE_TILES = 128
SUB = 8


def _tri_solve_kernel(a_ref, t_ref, at, tt):
    g2 = 2 * CHUNK

    def load_row(i, c):
        at[i] = a_ref[:, pl.ds(i, 1), :].reshape(SOLVE_TILES, g2).T
        return c

    lax.fori_loop(0, CHUNK, load_row, 0)
    rowid = lax.broadcasted_iota(jnp.int32, (SUB, SOLVE_TILES), 0)
    nk = CHUNK // SUB

    def solve_row(i, c):
        acc = tuple(jnp.zeros((SUB, SOLVE_TILES), F32) for _ in range(2 * nk))
        for mb in range(nk):
            def body(m, acc, mb=mb):
                a0 = jnp.broadcast_to(at[i, pl.ds(m, 1), :], (SUB, SOLVE_TILES))
                a1 = jnp.broadcast_to(at[i, pl.ds(CHUNK + m, 1), :], (SUB, SOLVE_TILES))
                new = list(acc)
                for k in range(mb + 1):
                    new[k] = acc[k] - a0 * tt[m, k * SUB:(k + 1) * SUB, :]
                    new[nk + k] = acc[nk + k] - a1 * tt[m, CHUNK + k * SUB:CHUNK + (k + 1) * SUB, :]
                return tuple(new)

            acc = lax.fori_loop(mb * SUB, jnp.minimum(mb * SUB + SUB, i), body, acc)
        for k in range(nk):
            diag = jnp.where(rowid + k * SUB == i, 1.0, 0.0).astype(F32)
            tt[i, k * SUB:(k + 1) * SUB, :] = acc[k] + diag
            tt[i, CHUNK + k * SUB:CHUNK + (k + 1) * SUB, :] = acc[nk + k] + diag
        return c

    lax.fori_loop(0, CHUNK, solve_row, 0)

    def store_row(i, c):
        t_ref[:, pl.ds(i, 1), :] = tt[i].T.reshape(SOLVE_TILES, 1, g2)
        return c

    lax.fori_loop(0, CHUNK, store_row, 0)


def _tri_solve(a_p):
    nt = a_p.shape[0]
    spec = pl.BlockSpec((SOLVE_TILES, CHUNK, 2 * CHUNK), lambda n: (n, 0, 0))
    return pl.pallas_call(
        _tri_solve_kernel,
        grid=(nt // SOLVE_TILES,),
        in_specs=[spec],
        out_specs=spec,
        out_shape=jax.ShapeDtypeStruct(a_p.shape, F32),
        scratch_shapes=[pltpu.VMEM((CHUNK, 2 * CHUNK, SOLVE_TILES), F32)] * 2,
        compiler_params=_cparams(("parallel",)),
        name="deltanet_tri_solve",
    )(a_p)


DN_HP = 2


def _dn_main_kernel(q_ref, k_ref, v_ref, beta_ref, gc_ref, gct_ref, t_ref, z_ref, nw_ref, o_ref, state):
    t = pl.program_id(2)
    hh = pl.program_id(1)
    tt = q_ref.shape[0]
    g2 = 2 * CHUNK

    @pl.when(t == 0)
    def _():
        state[...] = jnp.zeros_like(state)

    ri = lax.broadcasted_iota(jnp.int32, (g2, g2), 0)
    ci = lax.broadcasted_iota(jnp.int32, (g2, g2), 1)
    incl = jnp.logical_and((ri >= CHUNK) == (ci >= CHUNK), ri >= ci)
    lane = lax.broadcasted_iota(jnp.int32, (CHUNK, g2), 1)
    nw = nw_ref[...]
    beta = beta_ref[...]
    gc = gc_ref[...]
    bcols, gcols, grows, states = [], [], [], []
    for hp in range(DN_HP):
        h = hh * DN_HP + hp
        bcols.append(_col_of(beta, h))
        gcols.append(_col_of(gc, B_HEADS + h))
        grows.append(gct_ref[pl.ds(h, 1), :])
        states.append(state[hp])
    for gi in range(tt // g2):
        rs = slice(gi * g2, (gi + 1) * g2)
        for hp in range(DN_HP):
            hs = slice(hp * HEAD_DIM, (hp + 1) * HEAD_DIM)
            s = states[hp]
            qb = q_ref[rs, hs]
            kb = k_ref[rs, hs]
            kf = kb.astype(F32)
            b = bcols[hp][rs, :]
            gcl = gcols[hp][rs, :]
            eg = jnp.exp(gcl)
            kbeta = kf * b
            rhs = jnp.concatenate([v_ref[rs, hs].astype(F32) * b, kbeta * eg], axis=1).astype(BF16)
            tp = t_ref[hp, gi]
            tbd = jnp.concatenate([jnp.where(lane < CHUNK, tp, 0.0), jnp.where(lane >= CHUNK, tp, 0.0)], axis=0)
            uw = _dot(tbd.astype(BF16), rhs)
            u = uw[:, :HEAD_DIM]
            w = uw[:, HEAD_DIM:].astype(BF16)
            diff = gcl - grows[hp][:, rs]
            attn = (_dot_nt(qb, kb) * jnp.exp(jnp.where(incl, diff, NEG))).astype(BF16)
            qdec = (qb.astype(F32) * eg).astype(BF16)
            outs = []
            vnews = []
            for cc in range(2):
                cs = slice(cc * CHUNK, (cc + 1) * CHUNK)
                glast = gcl[cc * CHUNK + CHUNK - 1:cc * CHUNK + CHUNK, :]
                ktail = (kf[cs, :] * jnp.exp(glast - gcl[cs, :])).astype(BF16)
                sb = s.astype(BF16)
                vnew = u[cs, :] - _dot(w[cs, :], sb)
                vnews.append(vnew.astype(BF16))
                if cc == 0:
                    intra = _dot(attn[cs, 0:CHUNK], vnews[0])
                else:
                    intra = _dot(attn[cs, :], jnp.concatenate(vnews, axis=0))
                outs.append(_dot(qdec[cs, :], sb) + intra)
                s = s * jnp.exp(glast) + _dot_tn(ktail, vnews[cc])
            states[hp] = s
            o = jnp.concatenate(outs, axis=0)
            o = o * lax.rsqrt(jnp.mean(o * o, axis=1, keepdims=True) + RMS_EPS) * nw
            z = z_ref[rs, hs].astype(F32)
            o_ref[rs, hs] = (o * (z * jax.nn.sigmoid(z))).astype(o_ref.dtype)
    for hp in range(DN_HP):
        state[hp] = states[hp]


def _dn_main(qn, kn, vn, beta, gc, gct, t_p, zg, norm_w, tt):
    bsz, seq, _ = qn.shape
    nh = B_HEADS
    hw = DN_HP * HEAD_DIM
    tok = pl.BlockSpec((None, tt, hw), lambda b, h, t: (b, t, h))
    full = pl.BlockSpec((None, tt, LANES), lambda b, h, t: (b, t, 0))
    gts = pl.BlockSpec((None, nh, tt), lambda b, h, t: (b, 0, t))
    tspec = pl.BlockSpec((None, DN_HP, tt // (2 * CHUNK), CHUNK, 2 * CHUNK), lambda b, h, t: (b, h, t, 0, 0))
    nspec = pl.BlockSpec((1, HEAD_DIM), lambda b, h, t: (0, 0))
    return pl.pallas_call(
        _dn_main_kernel,
        grid=(bsz, nh // DN_HP, seq // tt),
        in_specs=[tok, tok, tok, full, full, gts, tspec, tok, nspec],
        out_specs=tok,
        out_shape=jax.ShapeDtypeStruct((bsz, seq, B_WIDTH), BF16),
        scratch_shapes=[pltpu.VMEM((DN_HP, HEAD_DIM, HEAD_DIM), F32)],
        compiler_params=_cparams(("parallel", "parallel", "arbitrary")),
        name="deltanet_main",
    )(qn, kn, vn, beta, gc, gct, t_p, zg, norm_w)


def _mix_out_kernel(ya_ref, yb_ref, ga_ref, gb_ref, x_ref,
                    wa_ref, wb_ref, wo_ref, g_ref, b_ref, xo_ref, xbo_ref):
    ma = _dot(ya_ref[...], wa_ref[...])
    mb = _dot(yb_ref[...], wb_ref[...])
    merged = jax.nn.sigmoid(ga_ref[...].astype(F32)) * ma + jax.nn.sigmoid(gb_ref[...].astype(F32)) * mb
    r = _dot(merged.astype(BF16), wo_ref[...])
    y = _layer_norm(DN_ALPHA * x_ref[...] + r, g_ref[...], b_ref[...])
    xo_ref[...] = y
    xbo_ref[...] = y.astype(BF16)


def _mix_out(ya, yb, zg, x, wa, wb, wo, g, b, tm=512):
    t = x.shape[0]
    d = D_MODEL
    aw = A_GROUP_WIDTH

    def rows(wd, c=0):
        return pl.BlockSpec((tm, wd), lambda i: (i, c))

    def whole(shape):
        return pl.BlockSpec(shape, lambda i: (0, 0))

    return pl.pallas_call(
        _mix_out_kernel,
        grid=(t // tm,),
        in_specs=[rows(aw), rows(d), rows(d, 1), rows(d, 2), rows(d),
                                   whole((aw, d)), whole((d, d)), whole((d, d)), whole((1, d)), whole((1, d))],
        out_specs=[rows(d), rows(d)],
        out_shape=[jax.ShapeDtypeStruct((t, d), F32), jax.ShapeDtypeStruct((t, d), BF16)],
        compiler_params=_cparams(("parallel",)),
        name="mix_out",
    )(ya, yb, zg, zg, x, wa, wb, wo, g, b)


def _ffn_kernel(xb_ref, x_ref, wg_ref, wu_ref, wd_ref, g_ref, b_ref, xo_ref, xbo_ref, acc):
    f = pl.program_id(1)

    @pl.when(f == 0)
    def _():
        acc[...] = jnp.zeros_like(acc)

    xb = xb_ref[...]
    gt = _dot(xb, wg_ref[...])
    up = _dot(xb, wu_ref[...])
    hh = (gt * jax.nn.sigmoid(gt) * up).astype(BF16)
    acc[...] += _dot(hh, wd_ref[...])

    @pl.when(f == pl.num_programs(1) - 1)
    def _():
        y = _layer_norm(DN_ALPHA * x_ref[...] + acc[...], g_ref[...], b_ref[...])
        xo_ref[...] = y
        xbo_ref[...] = y.astype(BF16)


def _ffn(xb, x, wg, wu, wd, g, b, tm=1024, tf=256):
    t, d = x.shape
    dff = wg.shape[1]
    rows = pl.BlockSpec((tm, d), lambda i, f: (i, 0))
    vec = pl.BlockSpec((1, d), lambda i, f: (0, 0))
    return pl.pallas_call(
        _ffn_kernel,
        grid=(t // tm, dff // tf),
        in_specs=[rows, rows,
                  pl.BlockSpec((d, tf), lambda i, f: (0, f)),
                  pl.BlockSpec((d, tf), lambda i, f: (0, f)),
                  pl.BlockSpec((tf, d), lambda i, f: (f, 0)), vec, vec],
        out_specs=[rows, rows],
        out_shape=[jax.ShapeDtypeStruct((t, d), F32), jax.ShapeDtypeStruct((t, d), BF16)],
        scratch_shapes=[pltpu.VMEM((tm, d), F32)],
        compiler_params=_cparams(("parallel", "arbitrary")),
        name="dense_swiglu",
    )(xb, x, wg, wu, wd, g, b)


def _router_kernel(x_ref, rw_ref, tri_ref, gates_ref, rank_ref, selt_ref, cnt_ref):
    logits = jnp.dot(x_ref[...], rw_ref[...], precision=lax.Precision.HIGHEST, preferred_element_type=F32)
    lane = lax.broadcasted_iota(jnp.int32, logits.shape, 1)
    lanef = lane.astype(F32)
    ninf = -jnp.inf
    lg = jnp.where(lane < N_EXPERTS, logits, ninf)
    m1 = jnp.max(lg, axis=1, keepdims=True)
    i1 = jnp.min(jnp.where(lg == m1, lanef, float(LANES)), axis=1, keepdims=True)
    lg2 = jnp.where(lanef == i1, ninf, lg)
    m2 = jnp.max(lg2, axis=1, keepdims=True)
    i2 = jnp.min(jnp.where(lg2 == m2, lanef, float(LANES)), axis=1, keepdims=True)
    e = jnp.exp(m2 - m1)
    w1 = 1.0 / (1.0 + e)
    w2 = e / (1.0 + e)
    gates = jnp.where(lanef == i1, w1, jnp.where(lanef == i2, w2, 0.0))
    gates_ref[...] = gates
    sel = gates > 0.0
    onef = jnp.where(sel, 1.0, 0.0)
    rank = _dot(tri_ref[...], onef.astype(BF16))
    rank_ref[...] = rank
    selt_ref[...] = jnp.where(sel, rank, -1.0).T[0:N_EXPERTS, :]
    cnt_ref[...] = jnp.broadcast_to(jnp.sum(onef, axis=0, keepdims=True), cnt_ref.shape).astype(jnp.int32)


def _router(x, rw_pad, tm):
    t, d = x.shape
    nt = t // tm
    tri = jnp.tril(jnp.ones((tm, tm), BF16), -1)
    rows = pl.BlockSpec((tm, LANES), lambda i: (i, 0))
    return pl.pallas_call(
        _router_kernel,
        grid=(nt,),
        in_specs=[pl.BlockSpec((tm, d), lambda i: (i, 0)), pl.BlockSpec((d, LANES), lambda i: (0, 0)),
                  pl.BlockSpec((tm, tm), lambda i: (0, 0))],
        out_specs=[rows, rows, pl.BlockSpec((None, N_EXPERTS, tm), lambda i: (i, 0, 0)),
                   pl.BlockSpec((None, 8, LANES), lambda i: (i, 0, 0))],
        out_shape=[jax.ShapeDtypeStruct((t, LANES), F32), jax.ShapeDtypeStruct((t, LANES), F32),
                   jax.ShapeDtypeStruct((nt, N_EXPERTS, tm), F32), jax.ShapeDtypeStruct((nt, 8, LANES), jnp.int32)],
        compiler_params=_cparams(("parallel",)),
        name="moe_router",
    )(x, rw_pad, tri)


MOE_TM = 1024
MOE_CAP = 320
MOE_FC = 512


def _expert_kernel(xb_ref, selt_ref, wg_ref, wu_ref, wd_ref, oc_ref):
    e = pl.program_id(0)
    tm = xb_ref.shape[0]
    sel_row = selt_ref[pl.ds(e, 1), :]
    cidx = lax.broadcasted_iota(jnp.int32, (MOE_CAP, tm), 0).astype(F32)
    onehot = jnp.where(sel_row == cidx, 1.0, 0.0).astype(BF16)
    xc = _dot(onehot, xb_ref[...]).astype(BF16)
    acc = jnp.zeros((MOE_CAP, D_MODEL), F32)
    for c in range(wg_ref.shape[1] // MOE_FC):
        cs = slice(c * MOE_FC, (c + 1) * MOE_FC)
        gt = _dot(xc, wg_ref[:, cs])
        up = _dot(xc, wu_ref[:, cs])
        hh = (gt * jax.nn.sigmoid(gt) * up).astype(BF16)
        acc = acc + _dot(hh, wd_ref[cs, :])
    oc_ref[...] = acc.astype(oc_ref.dtype)


def _experts(xb, selt, wg, wu, wd):
    t, d = xb.shape
    ne, _, dex = wg.shape
    nt = t // MOE_TM
    one = pl.Buffered(1)
    return pl.pallas_call(
        _expert_kernel,
        grid=(ne, nt),
        in_specs=[pl.BlockSpec((MOE_TM, d), lambda e, i: (i, 0)),
                  pl.BlockSpec((None, N_EXPERTS, MOE_TM), lambda e, i: (i, 0, 0)),
                  pl.BlockSpec((None, d, dex), lambda e, i: (e, 0, 0), pipeline_mode=one),
                  pl.BlockSpec((None, d, dex), lambda e, i: (e, 0, 0), pipeline_mode=one),
                  pl.BlockSpec((None, dex, d), lambda e, i: (e, 0, 0), pipeline_mode=one)],
        out_specs=pl.BlockSpec((None, None, MOE_CAP, d), lambda e, i: (e, i, 0, 0)),
        out_shape=jax.ShapeDtypeStruct((ne, nt, MOE_CAP, d), BF16),
        compiler_params=_cparams(("arbitrary", "arbitrary")),
        name="moe_experts",
    )(xb, selt, wg, wu, wd)


def _combine_kernel(x_ref, gates_ref, rank_ref, oc_ref, g_ref, b_ref, xo_ref):
    tm = x_ref.shape[0]
    gates = gates_ref[...]
    rank = rank_ref[...]
    lane = lax.broadcasted_iota(jnp.int32, (tm, MOE_CAP), 1).astype(F32)
    y = jnp.zeros((tm, D_MODEL), F32)
    for e in range(N_EXPERTS):
        gcol = gates[:, e:e + 1]
        scol = jnp.where(gcol > 0.0, rank[:, e:e + 1], -1.0)
        onehot = jnp.where(scol == lane, 1.0, 0.0).astype(BF16)
        y = y + _dot(onehot, oc_ref[e]) * gcol
    xo_ref[...] = _layer_norm(DN_ALPHA * x_ref[...] + y, g_ref[...], b_ref[...])


def _combine(x, gates, rank, oc, g, b):
    t, d = x.shape
    ne = oc.shape[0]
    rows = pl.BlockSpec((MOE_TM, d), lambda i: (i, 0))
    lrows = pl.BlockSpec((MOE_TM, LANES), lambda i: (i, 0))
    vec = pl.BlockSpec((1, d), lambda i: (0, 0))
    return pl.pallas_call(
        _combine_kernel,
        grid=(t // MOE_TM,),
        in_specs=[rows, lrows, lrows, pl.BlockSpec((ne, None, MOE_CAP, d), lambda i: (0, i, 0, 0)), vec, vec],
        out_specs=rows,
        out_shape=jax.ShapeDtypeStruct((t, d), F32),
        compiler_params=_cparams(("parallel",)),
        name="moe_combine",
    )(x, gates, rank, oc, g, b)


def _moe_kernel(xb_ref, x_ref, gates_ref, wg_ref, wu_ref, wd_ref, g_ref, b_ref, xo_ref, acc):
    e = pl.program_id(1)
    f = pl.program_id(2)

    @pl.when(jnp.logical_and(e == 0, f == 0))
    def _():
        acc[...] = jnp.zeros_like(acc)

    xb = xb_ref[...]
    gate = _col_of(gates_ref[...], e)
    gt = _dot(xb, wg_ref[...])
    up = _dot(xb, wu_ref[...])
    hh = (gt * jax.nn.sigmoid(gt) * up * gate).astype(BF16)
    acc[...] += _dot(hh, wd_ref[...])

    @pl.when(jnp.logical_and(e == pl.num_programs(1) - 1, f == pl.num_programs(2) - 1))
    def _():
        xo_ref[...] = _layer_norm(DN_ALPHA * x_ref[...] + acc[...], g_ref[...], b_ref[...])


def _moe(xb, x, gates, wg, wu, wd, g, b, tm=1024, tf=512):
    t, d = x.shape
    ne, _, dex = wg.shape
    rows = pl.BlockSpec((tm, d), lambda i, e, f: (i, 0))
    vec = pl.BlockSpec((1, d), lambda i, e, f: (0, 0))
    return pl.pallas_call(
        _moe_kernel,
        grid=(t // tm, ne, dex // tf),
        in_specs=[rows, rows, pl.BlockSpec((tm, LANES), lambda i, e, f: (i, 0)),
                  pl.BlockSpec((None, d, tf), lambda i, e, f: (e, 0, f)),
                  pl.BlockSpec((None, d, tf), lambda i, e, f: (e, 0, f)),
                  pl.BlockSpec((None, tf, d), lambda i, e, f: (e, f, 0)), vec, vec],
        out_specs=rows,
        out_shape=jax.ShapeDtypeStruct((t, d), F32),
        scratch_shapes=[pltpu.VMEM((tm, d), F32)],
        compiler_params=_cparams(("parallel", "arbitrary", "arbitrary")),
        name="moe_swiglu",
    )(xb, x, gates, wg, wu, wd, g, b)


def _rope_tables(positions):
    half = ROT_DIM // 2
    inv_freq = ROPE_THETA ** (-jnp.arange(0, ROT_DIM, 2, dtype=F32) / ROT_DIM)
    ang = positions.astype(F32)[..., None] * inv_freq
    cos, sin = jnp.cos(ang), jnp.sin(ang)
    shp = cos.shape[:-1]
    c = jnp.concatenate([cos, cos, jnp.ones(shp + (LANES - ROT_DIM,), F32)], axis=-1)
    s1 = jnp.concatenate([-sin, jnp.zeros(shp + (LANES - half,), F32)], axis=-1)
    s2 = jnp.concatenate([jnp.zeros(shp + (half,), F32), sin, jnp.zeros(shp + (LANES - ROT_DIM,), F32)], axis=-1)
    return c, s1, s2


def kernel(x, positions, w_in, conv_w, a_log, dt_bias, dn_norm_w, w_branch_a, w_branch_b, w_out, ln1_g, ln1_b,
           ffn_w_gate, ffn_w_up, ffn_w_down, router_w, moe_w_gate, moe_w_up, moe_w_down, ln2_g, ln2_b):
    bsz, seq, d = x.shape
    t = bsz * seq
    nh = B_HEADS
    qa_w = 3 * A_QKV_WIDTH
    o_qkvb = qa_w
    o_z = o_qkvb + 3 * B_WIDTH
    o_bd = o_z + B_WIDTH
    o_gates = o_bd + 2 * nh

    tabs = [tb.reshape(t, LANES) for tb in _rope_tables(positions)]

    dn_tt = 512
    xf = x.reshape(t, d)
    xb = xf.astype(BF16)
    for layer in range(DEPTH):
        w = w_in[layer]
        w_qkva = w[:, :qa_w].astype(BF16)
        w_qkvb = w[:, o_qkvb:o_z].astype(BF16)
        w_zg = jnp.concatenate([w[:, o_z:o_bd], w[:, o_gates:]], axis=1).astype(BF16)
        w_bd = jnp.pad(w[:, o_bd:o_gates], ((0, 0), (0, LANES - 2 * nh))).astype(BF16)

        subs = _proj_a(xb, w_qkva, tabs, bsz, seq)
        qkvb = _matmul(xb, w_qkvb, BF16, 1024, 1024)
        zg = _matmul(xb, w_zg, BF16, 1024, 1024)
        bd = _matmul(xb, w_bd, F32, 1024, LANES)

        ya = _attention([subs[p * N_GROUPS:(p + 1) * N_GROUPS] for p in range(3)], bsz, seq)

        zeros8 = jnp.zeros((nh,), F32)
        arow = jnp.concatenate([zeros8, -jnp.exp(a_log[layer].astype(F32)), jnp.zeros((LANES - 2 * nh,), F32)])[None, :]
        dtrow = jnp.concatenate([zeros8, dt_bias[layer].astype(F32), jnp.zeros((LANES - 2 * nh,), F32)])[None, :]
        beta, gc = _gates(bd, arow, dtrow)
        beta3 = beta.reshape(bsz, seq, LANES)
        gc3 = gc.reshape(bsz, seq, LANES)
        gct = gc3[:, :, nh:2 * nh].transpose(0, 2, 1)
        qn, kn, vn, a_p = _dn_prep(qkvb.reshape(bsz, seq, 3 * B_WIDTH), conv_w[layer].astype(F32),
                                   beta3, gc3, gct, dn_tt)
        ng = seq // (2 * CHUNK)
        t_p = _tri_solve(a_p.reshape(bsz * nh * ng, CHUNK, 2 * CHUNK)).reshape(bsz, nh, ng, CHUNK, 2 * CHUNK)
        yb = _dn_main(qn, kn, vn, beta3, gc3, gct, t_p, zg.reshape(bsz, seq, 3 * B_WIDTH),
                      dn_norm_w[layer].astype(F32)[None, :], dn_tt)

        xf, xb = _mix_out(ya.reshape(t, A_GROUP_WIDTH), yb.reshape(t, B_WIDTH), zg, xf,
                          w_branch_a[layer].astype(BF16), w_branch_b[layer].astype(BF16), w_out[layer].astype(BF16),
                          ln1_g[layer].astype(F32)[None, :], ln1_b[layer].astype(F32)[None, :])

        g2 = ln2_g[layer].astype(F32)[None, :]
        b2 = ln2_b[layer].astype(F32)[None, :]
        if layer % 2 == 0:
            i = layer // 2
            xf, xb = _ffn(xb, xf, ffn_w_gate[i].astype(BF16), ffn_w_up[i].astype(BF16), ffn_w_down[i].astype(BF16), g2, b2)
        else:
            i = layer // 2
            rw = jnp.pad(router_w[i].astype(F32), ((0, 0), (0, LANES - N_EXPERTS)))
            gates, rank, selt, cnt = _router(xf, rw, MOE_TM)
            wg = moe_w_gate[i].astype(BF16)
            wu = moe_w_up[i].astype(BF16)
            wd = moe_w_down[i].astype(BF16)

            def routed(xb, xf, gates, rank, selt):
                return _combine(xf, gates, rank, _experts(xb, selt, wg, wu, wd), g2, b2)

            def dense(xb, xf, gates, rank, selt):
                return _moe(xb, xf, gates, wg, wu, wd, g2, b2)

            xf = lax.cond(jnp.max(cnt) > MOE_CAP, dense, routed, xb, xf, gates, rank, selt)
            xb = xf.astype(BF16)
    return xf.reshape(bsz, seq, d)
```

```python
import functools
import math

import jax
import jax.numpy as jnp
from jax import lax
from jax.experimental import pallas as pl
from jax.experimental.pallas import tpu as pltpu

F32 = jnp.float32
BF16 = jnp.bfloat16

D_MODEL = 1024
DEPTH = 2
A_PAIRS = ((128, 1), (512, 4), (2048, 16))
A_HEADS = 4
HEAD_DIM = 128
A_GROUP_WIDTH = A_HEADS * HEAD_DIM
A_QKV_WIDTH = len(A_PAIRS) * A_GROUP_WIDTH
A_BLOCK = 128
ROPE_THETA = 500000.0
ROT_DIM = HEAD_DIM // 4
B_HEADS = 8
B_WIDTH = B_HEADS * HEAD_DIM
CONV_K = 4
CHUNK = 64
N_EXPERTS = 8
DN_ALPHA = (2 * DEPTH) ** 0.25
LN_EPS = 1e-5
RMS_EPS = 1e-6
NEG = -1e30
LANES = 128
VMEM_LIMIT = 56 * 1024 * 1024


def _cparams(sem):
    return pltpu.CompilerParams(dimension_semantics=sem, vmem_limit_bytes=VMEM_LIMIT)


def _dot(a, b):
    return jnp.dot(a, b, preferred_element_type=F32)


def _dot_nt(a, b):
    return lax.dot_general(a, b, (((1,), (1,)), ((), ())), preferred_element_type=F32)


def _dot_tn(a, b):
    return lax.dot_general(a, b, (((0,), (0,)), ((), ())), preferred_element_type=F32)


def _col_of(x, idx):
    lane = lax.broadcasted_iota(jnp.int32, x.shape, 1)
    return jnp.sum(jnp.where(lane == idx, x, 0.0), axis=1, keepdims=True)


def _layer_norm(v, g, b):
    mu = jnp.mean(v, axis=-1, keepdims=True)
    c = v - mu
    var = jnp.mean(c * c, axis=-1, keepdims=True)
    return c * lax.rsqrt(var + LN_EPS) * g + b


def _mm_kernel(x_ref, w_ref, o_ref):
    o_ref[...] = _dot(x_ref[...], w_ref[...]).astype(o_ref.dtype)


def _matmul(x, w, out_dtype, tm, tn):
    m, k = x.shape
    n = w.shape[1]
    return pl.pallas_call(
        _mm_kernel,
        grid=(m // tm, n // tn),
        in_specs=[pl.BlockSpec((tm, k), lambda i, j: (i, 0)),
                  pl.BlockSpec((k, tn), lambda i, j: (0, j))],
        out_specs=pl.BlockSpec((tm, tn), lambda i, j: (i, j)),
        out_shape=jax.ShapeDtypeStruct((m, n), out_dtype),
        compiler_params=_cparams(("parallel", "arbitrary")),
        name="proj_matmul",
    )(x, w)


def _rope(t, c, s1, s2):
    return t * c + pltpu.roll(t, LANES - ROT_DIM // 2, 1) * s1 + pltpu.roll(t, ROT_DIM // 2, 1) * s2


N_GROUPS = len(A_PAIRS)
DILS = tuple(d for _, d in A_PAIRS)


def _proj_a_kernel(x_ref, w_ref, c_ref, s1_ref, s2_ref, *refs):
    outs = refs[:3 * N_GROUPS]
    scr = refs[3 * N_GROUPS]
    tm = x_ref.shape[0]
    w = A_GROUP_WIDTH
    scale = 1.0 / math.sqrt(HEAD_DIM)
    x = x_ref[...]
    for jj in range(3 * N_GROUPS):
        p, g = divmod(jj, N_GROUPS)
        dil = DILS[g]
        o_ref = outs[jj]
        acc = _dot(x, w_ref[:, jj * w:(jj + 1) * w])
        for h in range(A_HEADS):
            sl = slice(h * HEAD_DIM, (h + 1) * HEAD_DIM)
            t = acc[:, sl]
            if p < 2:
                t = _rope(t, c_ref[...], s1_ref[...], s2_ref[...])
            if p == 0:
                t = t * scale
            if dil == 1:
                o_ref[:, sl] = t.astype(o_ref.dtype)
            else:
                slot = (jj % 2) * A_HEADS + h
                scr[slot] = t
                for r in range(dil):
                    o_ref[r, :, sl] = scr[slot, pl.ds(r, tm // dil, stride=dil), :].astype(o_ref.dtype)


def _proj_a(xb, w_qkva, tabs, bsz, seq, tm=512):
    t, k = xb.shape
    npb = seq // tm
    w = A_GROUP_WIDTH
    out_specs, out_shape = [], []
    for jj in range(3 * N_GROUPS):
        dil = DILS[jj % N_GROUPS]
        if dil == 1:
            out_specs.append(pl.BlockSpec((None, None, tm, w), lambda i: (i // npb, 0, i % npb, 0)))
        else:
            out_specs.append(pl.BlockSpec((None, dil, tm // dil, w), lambda i: (i // npb, 0, i % npb, 0)))
        out_shape.append(jax.ShapeDtypeStruct((bsz, dil, seq // dil, w), BF16))
    tab = pl.BlockSpec((tm, LANES), lambda i: (i, 0))
    return pl.pallas_call(
        _proj_a_kernel,
        grid=(t // tm,),
        in_specs=[pl.BlockSpec((tm, k), lambda i: (i, 0)),
                  pl.BlockSpec(w_qkva.shape, lambda i: (0, 0), pipeline_mode=pl.Buffered(1)), tab, tab, tab],
        out_specs=out_specs,
        out_shape=out_shape,
        scratch_shapes=[pltpu.VMEM((2 * A_HEADS, tm, HEAD_DIM), F32)],
        compiler_params=_cparams(("parallel",)),
        name="proj_mixer_a",
    )(xb, w_qkva, *tabs)


ATT_TT = 2048
ATT_UNROLL = 8


def _attn_kernel(*refs):
    ins = refs[:5 * N_GROUPS]
    o_ref = refs[5 * N_GROUPS]
    od, ld, on, ln = refs[5 * N_GROUPS + 1:5 * N_GROUPS + 5]
    kvbufs = refs[5 * N_GROUPS + 5:]
    it = pl.program_id(1)
    row = lax.broadcasted_iota(jnp.int32, (A_BLOCK, 2 * A_BLOCK), 0)
    col = lax.broadcasted_iota(jnp.int32, (A_BLOCK, 2 * A_BLOCK), 1)
    band = jnp.logical_and(col >= row, col <= row + A_BLOCK)
    bias_all = jnp.where(band, 0.0, NEG).astype(F32)
    bias_own = jnp.where(jnp.logical_and(band, col >= A_BLOCK), 0.0, NEG).astype(F32)
    nblocks = ATT_TT // A_BLOCK
    for g, dil in enumerate(DILS):
        q_ref, k_ref, v_ref, kh_ref, vh_ref = ins[5 * g:5 * g + 5]
        kbuf, vbuf = kvbufs[2 * g:2 * g + 2]
        nlb = nblocks // dil
        rows = ATT_TT // dil
        kbuf[:, 0:A_BLOCK, :] = kh_ref[...]
        kbuf[:, A_BLOCK:, :] = k_ref[...]
        vbuf[:, 0:A_BLOCK, :] = vh_ref[...]
        vbuf[:, A_BLOCK:, :] = v_ref[...]

        def block(c, carry, g=g, nlb=nlb, rows=rows, q_ref=q_ref, kbuf=kbuf, vbuf=vbuf):
            r = c // nlb
            nb = c % nlb
            off = pl.multiple_of(nb * A_BLOCK, A_BLOCK)
            q = q_ref[r, pl.ds(off, A_BLOCK), :]
            kk = kbuf[r, pl.ds(off, 2 * A_BLOCK), :]
            vv = vbuf[r, pl.ds(off, 2 * A_BLOCK), :]
            has_prev = jnp.logical_or(nb > 0, it > 0)
            s = _dot_nt(q, kk) + jnp.where(has_prev, bias_all, bias_own)
            m = jnp.max(s, axis=1, keepdims=True)
            p = jnp.exp(s - m)
            den = jnp.sum(p, axis=1, keepdims=True)
            o = _dot(p.astype(BF16), vv) / den
            dst = pl.multiple_of(r * rows + off, A_BLOCK)
            od[g, pl.ds(dst, A_BLOCK), :] = o
            ld[g, pl.ds(dst, A_BLOCK), :] = jnp.broadcast_to(m + jnp.log(den), (A_BLOCK, HEAD_DIM))
            return carry

        lax.fori_loop(0, nblocks, block, 0, unroll=ATT_UNROLL)
    for g, dil in enumerate(DILS):
        if dil == 1:
            continue
        rows = ATT_TT // dil
        for r in range(dil):
            on[g - 1, pl.ds(r, rows, stride=dil), :] = od[g, r * rows:(r + 1) * rows, :]
            ln[g - 1, pl.ds(r, rows, stride=dil), :] = ld[g, r * rows:(r + 1) * rows, :]
    step = 256
    for c in range(ATT_TT // step):
        sl = slice(c * step, (c + 1) * step)
        lse = [ld[0, sl, :]] + [ln[g - 1, sl, :] for g in range(1, N_GROUPS)]
        outs = [od[0, sl, :]] + [on[g - 1, sl, :] for g in range(1, N_GROUPS)]
        m = functools.reduce(jnp.maximum, lse)
        es = [jnp.exp(l - m) for l in lse]
        num = functools.reduce(lambda a, b: a + b, [e * o for e, o in zip(es, outs)])
        o_ref[sl, :] = (num / functools.reduce(lambda a, b: a + b, es)).astype(o_ref.dtype)


def _attention(qkv_sub, bsz, seq):
    specs, args = [], []
    for g, dil in enumerate(DILS):
        rows = ATT_TT // dil
        hb = rows // A_BLOCK
        cur = pl.BlockSpec((None, dil, rows, HEAD_DIM), lambda b, i, h: (b, 0, i, h))
        halo = pl.BlockSpec((None, dil, A_BLOCK, HEAD_DIM), lambda b, i, h, hb=hb: (b, 0, jnp.maximum(i * hb - 1, 0), h))
        specs += [cur, cur, cur, halo, halo]
        args += [qkv_sub[0][g], qkv_sub[1][g], qkv_sub[2][g], qkv_sub[1][g], qkv_sub[2][g]]
    return pl.pallas_call(
        _attn_kernel,
        grid=(bsz, seq // ATT_TT, A_HEADS),
        in_specs=specs,
        out_specs=pl.BlockSpec((None, ATT_TT, HEAD_DIM), lambda b, i, h: (b, i, h)),
        out_shape=jax.ShapeDtypeStruct((bsz, seq, A_GROUP_WIDTH), BF16),
        scratch_shapes=[pltpu.VMEM((N_GROUPS, ATT_TT, HEAD_DIM), F32)] * 2
        + [pltpu.VMEM((N_GROUPS - 1, ATT_TT, HEAD_DIM), F32)] * 2
        + [pltpu.VMEM((dil, ATT_TT // dil + A_BLOCK, HEAD_DIM), BF16) for dil in DILS for _ in range(2)],
        compiler_params=_cparams(("parallel", "parallel", "parallel")),
        name="dilated_attention",
    )(*args)


def _gate_kernel(bd_ref, arow_ref, dtrow_ref, beta_ref, gc_ref):
    x = bd_ref[...]
    beta_ref[...] = jax.nn.sigmoid(x)
    y = x + dtrow_ref[...]
    softplus = jnp.maximum(y, 0.0) + jnp.log(1.0 + jnp.exp(-jnp.abs(y)))
    g = arow_ref[...] * softplus
    tt = x.shape[0]
    ri = lax.broadcasted_iota(jnp.int32, (tt, tt), 0)
    ci = lax.broadcasted_iota(jnp.int32, (tt, tt), 1)
    tri = jnp.where(jnp.logical_and(ri // CHUNK == ci // CHUNK, ci <= ri), 1.0, 0.0).astype(F32)
    gc_ref[...] = jnp.dot(tri, g, precision=lax.Precision.HIGHEST, preferred_element_type=F32)


def _gates(bd, arow, dtrow, tt=512):
    t = bd.shape[0]
    spec = pl.BlockSpec((tt, LANES), lambda i: (i, 0))
    rspec = pl.BlockSpec((1, LANES), lambda i: (0, 0))
    return pl.pallas_call(
        _gate_kernel,
        grid=(t // tt,),
        in_specs=[spec, rspec, rspec],
        out_specs=[spec, spec],
        out_shape=[jax.ShapeDtypeStruct((t, LANES), F32)] * 2,
        compiler_params=_cparams(("parallel",)),
        name="deltanet_gates",
    )(bd, arow, dtrow)


HALO = 16


def _dn_prep_kernel(q_ref, k_ref, v_ref, qh_ref, kh_ref, vh_ref, wq_ref, wk_ref, wv_ref,
                    beta_ref, gc_ref, gct_ref, qo_ref, ko_ref, vo_ref, a_ref, scr):
    t = pl.program_id(1)
    h = pl.program_id(2)
    tt = q_ref.shape[0]

    def conv_silu(x_ref, halo_ref, w_ref):
        scr[0:HALO, :] = jnp.where(t > 0, halo_ref[...].astype(F32), 0.0)
        scr[HALO:, :] = x_ref[...].astype(F32)
        w = w_ref[...]
        y = scr[HALO - 3:HALO - 3 + tt, :] * w[0:1, :]
        for j in range(1, CONV_K):
            y = y + scr[HALO - 3 + j:HALO - 3 + j + tt, :] * w[j:j + 1, :]
        return y * jax.nn.sigmoid(y)

    def l2n(v):
        return v * lax.rsqrt(jnp.sum(v * v, axis=1, keepdims=True) + RMS_EPS)

    q = l2n(conv_silu(q_ref, qh_ref, wq_ref)) * (HEAD_DIM ** -0.5)
    qo_ref[...] = q.astype(qo_ref.dtype)
    k = l2n(conv_silu(k_ref, kh_ref, wk_ref)).astype(BF16)
    ko_ref[...] = k
    vo_ref[...] = conv_silu(v_ref, vh_ref, wv_ref).astype(vo_ref.dtype)

    bcol = _col_of(beta_ref[...], h)
    gcol = _col_of(gc_ref[...], B_HEADS + h)
    grow = gct_ref[pl.ds(h, 1), :]
    g2 = 2 * CHUNK
    ri = lax.broadcasted_iota(jnp.int32, (g2, g2), 0)
    ci = lax.broadcasted_iota(jnp.int32, (g2, g2), 1)
    strict = jnp.logical_and((ri >= CHUNK) == (ci >= CHUNK), ri > ci)
    lane = lax.broadcasted_iota(jnp.int32, (CHUNK, g2), 1)
    for gi in range(tt // g2):
        rs = slice(gi * g2, (gi + 1) * g2)
        kb = k[rs, :]
        kbeta = (kb.astype(F32) * bcol[rs, :]).astype(BF16)
        diff = gcol[rs, :] - grow[:, rs]
        a = _dot_nt(kbeta, kb) * jnp.exp(jnp.where(strict, diff, NEG))
        a_ref[gi] = jnp.where(lane < CHUNK, a[0:CHUNK, :], a[CHUNK:g2, :])


def _dn_prep(qkvb, conv_w, beta, gc, gct, tt):
    bsz, seq, _ = qkvb.shape
    nh = B_HEADS
    hb = tt // HALO

    def tok(off):
        return pl.BlockSpec((None, tt, HEAD_DIM), lambda b, t, h: (b, t, off + h))

    def halo(off):
        return pl.BlockSpec((None, HALO, HEAD_DIM), lambda b, t, h: (b, jnp.maximum(t * hb - 1, 0), off + h))

    def cw(off):
        return pl.BlockSpec((CONV_K, HEAD_DIM), lambda b, t, h: (0, off + h))

    full = pl.BlockSpec((None, tt, LANES), lambda b, t, h: (b, t, 0))
    gts = pl.BlockSpec((None, nh, tt), lambda b, t, h: (b, 0, t))
    ospec = pl.BlockSpec((None, tt, HEAD_DIM), lambda b, t, h: (b, t, h))
    aspec = pl.BlockSpec((None, None, tt // (2 * CHUNK), CHUNK, 2 * CHUNK), lambda b, t, h: (b, h, t, 0, 0))
    return pl.pallas_call(
        _dn_prep_kernel,
        grid=(bsz, seq // tt, nh),
        in_specs=[tok(0), tok(nh), tok(2 * nh), halo(0), halo(nh), halo(2 * nh),
                  cw(0), cw(nh), cw(2 * nh), full, full, gts],
        out_specs=[ospec, ospec, ospec, aspec],
        out_shape=[jax.ShapeDtypeStruct((bsz, seq, B_WIDTH), BF16)] * 3
        + [jax.ShapeDtypeStruct((bsz, nh, seq // (2 * CHUNK), CHUNK, 2 * CHUNK), F32)],
        scratch_shapes=[pltpu.VMEM((tt + HALO, HEAD_DIM), F32)],
        compiler_params=_cparams(("parallel", "parallel", "arbitrary")),
        name="deltanet_prep",
    )(qkvb, qkvb, qkvb, qkvb, qkvb, qkvb, conv_w, conv_w, conv_w, beta, gc, gct)


SOLVE_TILES = 128
SUB = 8


def _tri_solve_kernel(a_ref, t_ref, at, tt):
    g2 = 2 * CHUNK

    def load_row(i, c):
        at[i] = a_ref[:, pl.ds(i, 1), :].reshape(SOLVE_TILES, g2).T
        return c

    lax.fori_loop(0, CHUNK, load_row, 0)
    tt[...] = jnp.zeros_like(tt)
    rowid = lax.broadcasted_iota(jnp.int32, (SUB, SOLVE_TILES), 0)
    nk = CHUNK // SUB

    def solve_row(i, c):
        acc = tuple(jnp.zeros((SUB, SOLVE_TILES), F32) for _ in range(2 * nk))
        for mb in range(nk):
            def apply_block(acc, mb=mb):
                new = list(acc)
                for m in range(mb * SUB, (mb + 1) * SUB):
                    a0 = jnp.broadcast_to(at[i, m:m + 1, :], (SUB, SOLVE_TILES))
                    a1 = jnp.broadcast_to(at[i, CHUNK + m:CHUNK + m + 1, :], (SUB, SOLVE_TILES))
                    for k in range(mb + 1):
                        new[k] = new[k] - a0 * tt[m, k * SUB:(k + 1) * SUB, :]
                        new[nk + k] = new[nk + k] - a1 * tt[m, CHUNK + k * SUB:CHUNK + (k + 1) * SUB, :]
                return tuple(new)

            acc = lax.cond(mb * SUB < i, apply_block, lambda a: a, acc)
        for k in range(nk):
            diag = jnp.where(rowid + k * SUB == i, 1.0, 0.0).astype(F32)
            tt[i, k * SUB:(k + 1) * SUB, :] = acc[k] + diag
            tt[i, CHUNK + k * SUB:CHUNK + (k + 1) * SUB, :] = acc[nk + k] + diag
        return c

    lax.fori_loop(0, CHUNK, solve_row, 0)

    def store_row(i, c):
        t_ref[:, pl.ds(i, 1), :] = tt[i].T.reshape(SOLVE_TILES, 1, g2)
        return c

    lax.fori_loop(0, CHUNK, store_row, 0)


def _tri_solve(a_p):
    nt = a_p.shape[0]
    spec = pl.BlockSpec((SOLVE_TILES, CHUNK, 2 * CHUNK), lambda n: (n, 0, 0))
    return pl.pallas_call(
        _tri_solve_kernel,
        grid=(nt // SOLVE_TILES,),
        in_specs=[spec],
        out_specs=spec,
        out_shape=jax.ShapeDtypeStruct(a_p.shape, F32),
        scratch_shapes=[pltpu.VMEM((CHUNK, 2 * CHUNK, SOLVE_TILES), F32)] * 2,
        compiler_params=_cparams(("parallel",)),
        name="deltanet_tri_solve",
    )(a_p)


DN_HP = 2


def _dn_main_kernel(q_ref, k_ref, v_ref, beta_ref, gc_ref, gct_ref, t_ref, z_ref, nw_ref, o_ref, state):
    t = pl.program_id(2)
    hh = pl.program_id(1)
    tt = q_ref.shape[0]
    g2 = 2 * CHUNK

    @pl.when(t == 0)
    def _():
        state[...] = jnp.zeros_like(state)

    ri = lax.broadcasted_iota(jnp.int32, (g2, g2), 0)
    ci = lax.broadcasted_iota(jnp.int32, (g2, g2), 1)
    incl = jnp.logical_and((ri >= CHUNK) == (ci >= CHUNK), ri >= ci)
    lane = lax.broadcasted_iota(jnp.int32, (CHUNK, g2), 1)
    nw = nw_ref[...]
    beta = beta_ref[...]
    gc = gc_ref[...]
    bcols, gcols, grows, states = [], [], [], []
    for hp in range(DN_HP):
        h = hh * DN_HP + hp
        bcols.append(_col_of(beta, h))
        gcols.append(_col_of(gc, B_HEADS + h))
        grows.append(gct_ref[pl.ds(h, 1), :])
        states.append(state[hp])
    for gi in range(tt // g2):
        rs = slice(gi * g2, (gi + 1) * g2)
        for hp in range(DN_HP):
            hs = slice(hp * HEAD_DIM, (hp + 1) * HEAD_DIM)
            s = states[hp]
            qb = q_ref[rs, hs]
            kb = k_ref[rs, hs]
            kf = kb.astype(F32)
            b = bcols[hp][rs, :]
            gcl = gcols[hp][rs, :]
            eg = jnp.exp(gcl)
            kbeta = kf * b
            rhs = jnp.concatenate([v_ref[rs, hs].astype(F32) * b, kbeta * eg], axis=1).astype(BF16)
            tp = t_ref[hp, gi]
            tbd = jnp.concatenate([jnp.where(lane < CHUNK, tp, 0.0), jnp.where(lane >= CHUNK, tp, 0.0)], axis=0)
            uw = _dot(tbd.astype(BF16), rhs).astype(BF16)
            diff = gcl - grows[hp][:, rs]
            attn = (_dot_nt(qb, kb) * jnp.exp(jnp.where(incl, diff, NEG))).astype(BF16)
            auw = _dot(attn, uw)
            qeff = (qb.astype(F32) * eg - auw[:, HEAD_DIM:]).astype(BF16)
            outs = []
            for cc in range(2):
                cs = slice(cc * CHUNK, (cc + 1) * CHUNK)
                glast = gcl[cc * CHUNK + CHUNK - 1:cc * CHUNK + CHUNK, :]
                ktail = (kf[cs, :] * jnp.exp(glast - gcl[cs, :])).astype(BF16)
                kuw = _dot_tn(ktail, uw[cs, :])
                lhs = jnp.concatenate([kuw[:, HEAD_DIM:].astype(BF16), qeff[cs, :]], axis=0)
                xs = _dot(lhs, s.astype(BF16))
                outs.append(xs[HEAD_DIM:, :] + auw[cs, :HEAD_DIM])
                s = s * jnp.exp(glast) + kuw[:, :HEAD_DIM] - xs[:HEAD_DIM, :]
            states[hp] = s
            o = jnp.concatenate(outs, axis=0)
            o = o * lax.rsqrt(jnp.mean(o * o, axis=1, keepdims=True) + RMS_EPS) * nw
            z = z_ref[rs, hs].astype(F32)
            o_ref[rs, hs] = (o * (z * jax.nn.sigmoid(z))).astype(o_ref.dtype)
    for hp in range(DN_HP):
        state[hp] = states[hp]


def _dn_main(qn, kn, vn, beta, gc, gct, t_p, zg, norm_w, tt):
    bsz, seq, _ = qn.shape
    nh = B_HEADS
    hw = DN_HP * HEAD_DIM
    tok = pl.BlockSpec((None, tt, hw), lambda b, h, t: (b, t, h))
    full = pl.BlockSpec((None, tt, LANES), lambda b, h, t: (b, t, 0))
    gts = pl.BlockSpec((None, nh, tt), lambda b, h, t: (b, 0, t))
    tspec = pl.BlockSpec((None, DN_HP, tt // (2 * CHUNK), CHUNK, 2 * CHUNK), lambda b, h, t: (b, h, t, 0, 0))
    nspec = pl.BlockSpec((1, HEAD_DIM), lambda b, h, t: (0, 0))
    return pl.pallas_call(
        _dn_main_kernel,
        grid=(bsz, nh // DN_HP, seq // tt),
        in_specs=[tok, tok, tok, full, full, gts, tspec, tok, nspec],
        out_specs=tok,
        out_shape=jax.ShapeDtypeStruct((bsz, seq, B_WIDTH), BF16),
        scratch_shapes=[pltpu.VMEM((DN_HP, HEAD_DIM, HEAD_DIM), F32)],
        compiler_params=_cparams(("parallel", "parallel", "arbitrary")),
        name="deltanet_main",
    )(qn, kn, vn, beta, gc, gct, t_p, zg, norm_w)


def _mix_out_kernel(ya_ref, yb_ref, ga_ref, gb_ref, x_ref,
                    wa_ref, wb_ref, wo_ref, g_ref, b_ref, xo_ref, xbo_ref):
    ma = _dot(ya_ref[...], wa_ref[...])
    mb = _dot(yb_ref[...], wb_ref[...])
    merged = jax.nn.sigmoid(ga_ref[...].astype(F32)) * ma + jax.nn.sigmoid(gb_ref[...].astype(F32)) * mb
    r = _dot(merged.astype(BF16), wo_ref[...])
    y = _layer_norm(DN_ALPHA * x_ref[...] + r, g_ref[...], b_ref[...])
    xo_ref[...] = y
    xbo_ref[...] = y.astype(BF16)


def _mix_out(ya, yb, zg, x, wa, wb, wo, g, b, tm=512):
    t = x.shape[0]
    d = D_MODEL
    aw = A_GROUP_WIDTH

    def rows(wd, c=0):
        return pl.BlockSpec((tm, wd), lambda i: (i, c))

    def whole(shape):
        return pl.BlockSpec(shape, lambda i: (0, 0))

    return pl.pallas_call(
        _mix_out_kernel,
        grid=(t // tm,),
        in_specs=[rows(aw), rows(d), rows(d, 1), rows(d, 2), rows(d),
                                   whole((aw, d)), whole((d, d)), whole((d, d)), whole((1, d)), whole((1, d))],
        out_specs=[rows(d), rows(d)],
        out_shape=[jax.ShapeDtypeStruct((t, d), F32), jax.ShapeDtypeStruct((t, d), BF16)],
        compiler_params=_cparams(("parallel",)),
        name="mix_out",
    )(ya, yb, zg, zg, x, wa, wb, wo, g, b)


def _ffn_kernel(xb_ref, x_ref, wg_ref, wu_ref, wd_ref, g_ref, b_ref, xo_ref, xbo_ref, acc):
    f = pl.program_id(1)

    @pl.when(f == 0)
    def _():
        acc[...] = jnp.zeros_like(acc)

    xb = xb_ref[...]
    gt = _dot(xb, wg_ref[...])
    up = _dot(xb, wu_ref[...])
    hh = (gt * jax.nn.sigmoid(gt) * up).astype(BF16)
    acc[...] += _dot(hh, wd_ref[...])

    @pl.when(f == pl.num_programs(1) - 1)
    def _():
        y = _layer_norm(DN_ALPHA * x_ref[...] + acc[...], g_ref[...], b_ref[...])
        xo_ref[...] = y
        xbo_ref[...] = y.astype(BF16)


def _ffn(xb, x, wg, wu, wd, g, b, tm=1024, tf=256):
    t, d = x.shape
    dff = wg.shape[1]
    rows = pl.BlockSpec((tm, d), lambda i, f: (i, 0))
    vec = pl.BlockSpec((1, d), lambda i, f: (0, 0))
    return pl.pallas_call(
        _ffn_kernel,
        grid=(t // tm, dff // tf),
        in_specs=[rows, rows,
                  pl.BlockSpec((d, tf), lambda i, f: (0, f)),
                  pl.BlockSpec((d, tf), lambda i, f: (0, f)),
                  pl.BlockSpec((tf, d), lambda i, f: (f, 0)), vec, vec],
        out_specs=[rows, rows],
        out_shape=[jax.ShapeDtypeStruct((t, d), F32), jax.ShapeDtypeStruct((t, d), BF16)],
        scratch_shapes=[pltpu.VMEM((tm, d), F32)],
        compiler_params=_cparams(("parallel", "arbitrary")),
        name="dense_swiglu",
    )(xb, x, wg, wu, wd, g, b)


def _router_kernel(x_ref, rw_ref, tri_ref, gates_ref, rank_ref, selt_ref, cnt_ref):
    logits = jnp.dot(x_ref[...], rw_ref[...], precision=lax.Precision.HIGHEST, preferred_element_type=F32)
    lane = lax.broadcasted_iota(jnp.int32, logits.shape, 1)
    lanef = lane.astype(F32)
    ninf = -jnp.inf
    lg = jnp.where(lane < N_EXPERTS, logits, ninf)
    m1 = jnp.max(lg, axis=1, keepdims=True)
    i1 = jnp.min(jnp.where(lg == m1, lanef, float(LANES)), axis=1, keepdims=True)
    lg2 = jnp.where(lanef == i1, ninf, lg)
    m2 = jnp.max(lg2, axis=1, keepdims=True)
    i2 = jnp.min(jnp.where(lg2 == m2, lanef, float(LANES)), axis=1, keepdims=True)
    e = jnp.exp(m2 - m1)
    w1 = 1.0 / (1.0 + e)
    w2 = e / (1.0 + e)
    gates = jnp.where(lanef == i1, w1, jnp.where(lanef == i2, w2, 0.0))
    gates_ref[...] = gates
    sel = gates > 0.0
    onef = jnp.where(sel, 1.0, 0.0)
    rank = _dot(tri_ref[...], onef.astype(BF16))
    rank_ref[...] = rank
    selt_ref[...] = jnp.where(sel, rank, -1.0).T[0:N_EXPERTS, :]
    cnt_ref[...] = jnp.broadcast_to(jnp.sum(onef, axis=0, keepdims=True), cnt_ref.shape).astype(jnp.int32)


def _router(x, rw_pad, tm):
    t, d = x.shape
    nt = t // tm
    tri = jnp.tril(jnp.ones((tm, tm), BF16), -1)
    rows = pl.BlockSpec((tm, LANES), lambda i: (i, 0))
    return pl.pallas_call(
        _router_kernel,
        grid=(nt,),
        in_specs=[pl.BlockSpec((tm, d), lambda i: (i, 0)), pl.BlockSpec((d, LANES), lambda i: (0, 0)),
                  pl.BlockSpec((tm, tm), lambda i: (0, 0))],
        out_specs=[rows, rows, pl.BlockSpec((None, N_EXPERTS, tm), lambda i: (i, 0, 0)),
                   pl.BlockSpec((None, 8, LANES), lambda i: (i, 0, 0))],
        out_shape=[jax.ShapeDtypeStruct((t, LANES), F32), jax.ShapeDtypeStruct((t, LANES), F32),
                   jax.ShapeDtypeStruct((nt, N_EXPERTS, tm), F32), jax.ShapeDtypeStruct((nt, 8, LANES), jnp.int32)],
        compiler_params=_cparams(("parallel",)),
        name="moe_router",
    )(x, rw_pad, tri)


MOE_TM = 1024
MOE_CAP = 320
MOE_FC = 512


def _expert_kernel(xb_ref, selt_ref, wg_ref, wu_ref, wd_ref, oc_ref):
    e = pl.program_id(0)
    tm = xb_ref.shape[0]
    sel_row = selt_ref[pl.ds(e, 1), :]
    cidx = lax.broadcasted_iota(jnp.int32, (MOE_CAP, tm), 0).astype(F32)
    onehot = jnp.where(sel_row == cidx, 1.0, 0.0).astype(BF16)
    xc = _dot(onehot, xb_ref[...]).astype(BF16)
    acc = jnp.zeros((MOE_CAP, D_MODEL), F32)
    for c in range(wg_ref.shape[1] // MOE_FC):
        cs = slice(c * MOE_FC, (c + 1) * MOE_FC)
        gt = _dot(xc, wg_ref[:, cs])
        up = _dot(xc, wu_ref[:, cs])
        hh = (gt * jax.nn.sigmoid(gt) * up).astype(BF16)
        acc = acc + _dot(hh, wd_ref[cs, :])
    oc_ref[...] = acc.astype(oc_ref.dtype)


def _experts(xb, selt, wg, wu, wd):
    t, d = xb.shape
    ne, _, dex = wg.shape
    nt = t // MOE_TM
    one = pl.Buffered(1)
    return pl.pallas_call(
        _expert_kernel,
        grid=(ne, nt),
        in_specs=[pl.BlockSpec((MOE_TM, d), lambda e, i: (i, 0)),
                  pl.BlockSpec((None, N_EXPERTS, MOE_TM), lambda e, i: (i, 0, 0)),
                  pl.BlockSpec((None, d, dex), lambda e, i: (e, 0, 0), pipeline_mode=one),
                  pl.BlockSpec((None, d, dex), lambda e, i: (e, 0, 0), pipeline_mode=one),
                  pl.BlockSpec((None, dex, d), lambda e, i: (e, 0, 0), pipeline_mode=one)],
        out_specs=pl.BlockSpec((None, None, MOE_CAP, d), lambda e, i: (e, i, 0, 0)),
        out_shape=jax.ShapeDtypeStruct((ne, nt, MOE_CAP, d), BF16),
        compiler_params=_cparams(("arbitrary", "arbitrary")),
        name="moe_experts",
    )(xb, selt, wg, wu, wd)


def _combine_kernel(x_ref, gates_ref, rank_ref, oc_ref, g_ref, b_ref, xo_ref):
    tm = x_ref.shape[0]
    gates = gates_ref[...]
    rank = rank_ref[...]
    lane = lax.broadcasted_iota(jnp.int32, (tm, MOE_CAP), 1).astype(F32)
    y = jnp.zeros((tm, D_MODEL), F32)
    for e in range(N_EXPERTS):
        gcol = gates[:, e:e + 1]
        scol = jnp.where(gcol > 0.0, rank[:, e:e + 1], -1.0)
        onehot = jnp.where(scol == lane, 1.0, 0.0).astype(BF16)
        y = y + _dot(onehot, oc_ref[e]) * gcol
    xo_ref[...] = _layer_norm(DN_ALPHA * x_ref[...] + y, g_ref[...], b_ref[...])


def _combine(x, gates, rank, oc, g, b):
    t, d = x.shape
    ne = oc.shape[0]
    rows = pl.BlockSpec((MOE_TM, d), lambda i: (i, 0))
    lrows = pl.BlockSpec((MOE_TM, LANES), lambda i: (i, 0))
    vec = pl.BlockSpec((1, d), lambda i: (0, 0))
    return pl.pallas_call(
        _combine_kernel,
        grid=(t // MOE_TM,),
        in_specs=[rows, lrows, lrows, pl.BlockSpec((ne, None, MOE_CAP, d), lambda i: (0, i, 0, 0)), vec, vec],
        out_specs=rows,
        out_shape=jax.ShapeDtypeStruct((t, d), F32),
        compiler_params=_cparams(("parallel",)),
        name="moe_combine",
    )(x, gates, rank, oc, g, b)


def _moe_kernel(xb_ref, x_ref, gates_ref, wg_ref, wu_ref, wd_ref, g_ref, b_ref, xo_ref, acc):
    e = pl.program_id(1)
    f = pl.program_id(2)

    @pl.when(jnp.logical_and(e == 0, f == 0))
    def _():
        acc[...] = jnp.zeros_like(acc)

    xb = xb_ref[...]
    gate = _col_of(gates_ref[...], e)
    gt = _dot(xb, wg_ref[...])
    up = _dot(xb, wu_ref[...])
    hh = (gt * jax.nn.sigmoid(gt) * up * gate).astype(BF16)
    acc[...] += _dot(hh, wd_ref[...])

    @pl.when(jnp.logical_and(e == pl.num_programs(1) - 1, f == pl.num_programs(2) - 1))
    def _():
        xo_ref[...] = _layer_norm(DN_ALPHA * x_ref[...] + acc[...], g_ref[...], b_ref[...])


def _moe(xb, x, gates, wg, wu, wd, g, b, tm=1024, tf=512):
    t, d = x.shape
    ne, _, dex = wg.shape
    rows = pl.BlockSpec((tm, d), lambda i, e, f: (i, 0))
    vec = pl.BlockSpec((1, d), lambda i, e, f: (0, 0))
    return pl.pallas_call(
        _moe_kernel,
        grid=(t // tm, ne, dex // tf),
        in_specs=[rows, rows, pl.BlockSpec((tm, LANES), lambda i, e, f: (i, 0)),
                  pl.BlockSpec((None, d, tf), lambda i, e, f: (e, 0, f)),
                  pl.BlockSpec((None, d, tf), lambda i, e, f: (e, 0, f)),
                  pl.BlockSpec((None, tf, d), lambda i, e, f: (e, f, 0)), vec, vec],
        out_specs=rows,
        out_shape=jax.ShapeDtypeStruct((t, d), F32),
        scratch_shapes=[pltpu.VMEM((tm, d), F32)],
        compiler_params=_cparams(("parallel", "arbitrary", "arbitrary")),
        name="moe_swiglu",
    )(xb, x, gates, wg, wu, wd, g, b)


def _rope_tables(positions):
    half = ROT_DIM // 2
    inv_freq = ROPE_THETA ** (-jnp.arange(0, ROT_DIM, 2, dtype=F32) / ROT_DIM)
    ang = positions.astype(F32)[..., None] * inv_freq
    cos, sin = jnp.cos(ang), jnp.sin(ang)
    shp = cos.shape[:-1]
    c = jnp.concatenate([cos, cos, jnp.ones(shp + (LANES - ROT_DIM,), F32)], axis=-1)
    s1 = jnp.concatenate([-sin, jnp.zeros(shp + (LANES - half,), F32)], axis=-1)
    s2 = jnp.concatenate([jnp.zeros(shp + (half,), F32), sin, jnp.zeros(shp + (LANES - ROT_DIM,), F32)], axis=-1)
    return c, s1, s2


def kernel(x, positions, w_in, conv_w, a_log, dt_bias, dn_norm_w, w_branch_a, w_branch_b, w_out, ln1_g, ln1_b,
           ffn_w_gate, ffn_w_up, ffn_w_down, router_w, moe_w_gate, moe_w_up, moe_w_down, ln2_g, ln2_b):
    bsz, seq, d = x.shape
    t = bsz * seq
    nh = B_HEADS
    qa_w = 3 * A_QKV_WIDTH
    o_qkvb = qa_w
    o_z = o_qkvb + 3 * B_WIDTH
    o_bd = o_z + B_WIDTH
    o_gates = o_bd + 2 * nh

    tabs = [tb.reshape(t, LANES) for tb in _rope_tables(positions)]

    dn_tt = 512
    xf = x.reshape(t, d)
    xb = xf.astype(BF16)
    for layer in range(DEPTH):
        w = w_in[layer]
        w_qkva = w[:, :qa_w].astype(BF16)
        w_qkvb = w[:, o_qkvb:o_z].astype(BF16)
        w_zg = jnp.concatenate([w[:, o_z:o_bd], w[:, o_gates:]], axis=1).astype(BF16)
        w_bd = jnp.pad(w[:, o_bd:o_gates], ((0, 0), (0, LANES - 2 * nh))).astype(BF16)

        subs = _proj_a(xb, w_qkva, tabs, bsz, seq)
        qkvb = _matmul(xb, w_qkvb, BF16, 1024, 1024)
        zg = _matmul(xb, w_zg, BF16, 1024, 1024)
        bd = _matmul(xb, w_bd, F32, 1024, LANES)

        ya = _attention([subs[p * N_GROUPS:(p + 1) * N_GROUPS] for p in range(3)], bsz, seq)

        zeros8 = jnp.zeros((nh,), F32)
        arow = jnp.concatenate([zeros8, -jnp.exp(a_log[layer].astype(F32)), jnp.zeros((LANES - 2 * nh,), F32)])[None, :]
        dtrow = jnp.concatenate([zeros8, dt_bias[layer].astype(F32), jnp.zeros((LANES - 2 * nh,), F32)])[None, :]
        beta, gc = _gates(bd, arow, dtrow)
        beta3 = beta.reshape(bsz, seq, LANES)
        gc3 = gc.reshape(bsz, seq, LANES)
        gct = gc3[:, :, nh:2 * nh].transpose(0, 2, 1)
        qn, kn, vn, a_p = _dn_prep(qkvb.reshape(bsz, seq, 3 * B_WIDTH), conv_w[layer].astype(F32),
                                   beta3, gc3, gct, dn_tt)
        ng = seq // (2 * CHUNK)
        t_p = _tri_solve(a_p.reshape(bsz * nh * ng, CHUNK, 2 * CHUNK)).reshape(bsz, nh, ng, CHUNK, 2 * CHUNK)
        yb = _dn_main(qn, kn, vn, beta3, gc3, gct, t_p, zg.reshape(bsz, seq, 3 * B_WIDTH),
                      dn_norm_w[layer].astype(F32)[None, :], dn_tt)

        xf, xb = _mix_out(ya.reshape(t, A_GROUP_WIDTH), yb.reshape(t, B_WIDTH), zg, xf,
                          w_branch_a[layer].astype(BF16), w_branch_b[layer].astype(BF16), w_out[layer].astype(BF16),
                          ln1_g[layer].astype(F32)[None, :], ln1_b[layer].astype(F32)[None, :])

        g2 = ln2_g[layer].astype(F32)[None, :]
        b2 = ln2_b[layer].astype(F32)[None, :]
        if layer % 2 == 0:
            i = layer // 2
            xf, xb = _ffn(xb, xf, ffn_w_gate[i].astype(BF16), ffn_w_up[i].astype(BF16), ffn_w_down[i].astype(BF16), g2, b2)
        else:
            i = layer // 2
            rw = jnp.pad(router_w[i].astype(F32), ((0, 0), (0, LANES - N_EXPERTS)))
            gates, rank, selt, cnt = _router(xf, rw, MOE_TM)
            wg = moe_w_gate[i].astype(BF16)
            wu = moe_w_up[i].astype(BF16)
            wd = moe_w_down[i].astype(BF16)

            def routed(xb, xf, gates, rank, selt):
                return _combine(xf, gates, rank, _experts(xb, selt, wg, wu, wd), g2, b2)

            def dense(xb, xf, gates, rank, selt):
                return _moe(xb, xf, gates, wg, wu, wd, g2, b2)

            xf = lax.cond(jnp.max(cnt) > MOE_CAP, dense, routed, xb, xf, gates, rank, selt)
            xb = xf.astype(BF16)
    return xf.reshape(bsz, seq, d)
```

```python
import functools
import math

import jax
import jax.numpy as jnp
from jax import lax
from jax.experimental import pallas as pl
from jax.experimental.pallas import tpu as pltpu

F32 = jnp.float32
BF16 = jnp.bfloat16

D_MODEL = 1024
DEPTH = 2
A_PAIRS = ((128, 1), (512, 4), (2048, 16))
A_HEADS = 4
HEAD_DIM = 128
A_GROUP_WIDTH = A_HEADS * HEAD_DIM
A_QKV_WIDTH = len(A_PAIRS) * A_GROUP_WIDTH
A_BLOCK = 128
ROPE_THETA = 500000.0
ROT_DIM = HEAD_DIM // 4
B_HEADS = 8
B_WIDTH = B_HEADS * HEAD_DIM
CONV_K = 4
CHUNK = 64
N_EXPERTS = 8
DN_ALPHA = (2 * DEPTH) ** 0.25
LN_EPS = 1e-5
RMS_EPS = 1e-6
NEG = -1e30
LANES = 128
VMEM_LIMIT = 56 * 1024 * 1024


def _cparams(sem):
    return pltpu.CompilerParams(dimension_semantics=sem, vmem_limit_bytes=VMEM_LIMIT)


def _dot(a, b):
    return jnp.dot(a, b, preferred_element_type=F32)


def _dot_nt(a, b):
    return lax.dot_general(a, b, (((1,), (1,)), ((), ())), preferred_element_type=F32)


def _dot_tn(a, b):
    return lax.dot_general(a, b, (((0,), (0,)), ((), ())), preferred_element_type=F32)


def _col_of(x, idx):
    lane = lax.broadcasted_iota(jnp.int32, x.shape, 1)
    return jnp.sum(jnp.where(lane == idx, x, 0.0), axis=1, keepdims=True)


def _layer_norm(v, g, b):
    mu = jnp.mean(v, axis=-1, keepdims=True)
    c = v - mu
    var = jnp.mean(c * c, axis=-1, keepdims=True)
    return c * lax.rsqrt(var + LN_EPS) * g + b


def _mm_kernel(x_ref, w_ref, o_ref):
    o_ref[...] = _dot(x_ref[...], w_ref[...]).astype(o_ref.dtype)


def _matmul(x, w, out_dtype, tm, tn):
    m, k = x.shape
    n = w.shape[1]
    return pl.pallas_call(
        _mm_kernel,
        grid=(m // tm, n // tn),
        in_specs=[pl.BlockSpec((tm, k), lambda i, j: (i, 0)),
                  pl.BlockSpec((k, tn), lambda i, j: (0, j))],
        out_specs=pl.BlockSpec((tm, tn), lambda i, j: (i, j)),
        out_shape=jax.ShapeDtypeStruct((m, n), out_dtype),
        compiler_params=_cparams(("parallel", "arbitrary")),
        name="proj_matmul",
    )(x, w)


def _rope(t, c, s1, s2):
    return t * c + pltpu.roll(t, LANES - ROT_DIM // 2, 1) * s1 + pltpu.roll(t, ROT_DIM // 2, 1) * s2


N_GROUPS = len(A_PAIRS)
DILS = tuple(d for _, d in A_PAIRS)


def _proj_a_kernel(x_ref, w_ref, c_ref, s1_ref, s2_ref, *refs):
    outs = refs[:3 * N_GROUPS]
    scr = refs[3 * N_GROUPS]
    tm = x_ref.shape[0]
    w = A_GROUP_WIDTH
    scale = 1.0 / math.sqrt(HEAD_DIM)
    x = x_ref[...]
    for jj in range(3 * N_GROUPS):
        p, g = divmod(jj, N_GROUPS)
        dil = DILS[g]
        o_ref = outs[jj]
        acc = _dot(x, w_ref[:, jj * w:(jj + 1) * w])
        for h in range(A_HEADS):
            sl = slice(h * HEAD_DIM, (h + 1) * HEAD_DIM)
            t = acc[:, sl]
            if p < 2:
                t = _rope(t, c_ref[...], s1_ref[...], s2_ref[...])
            if p == 0:
                t = t * scale
            if dil == 1:
                o_ref[:, sl] = t.astype(o_ref.dtype)
            else:
                slot = (jj % 2) * A_HEADS + h
                scr[slot] = t
                for r in range(dil):
                    o_ref[r, :, sl] = scr[slot, pl.ds(r, tm // dil, stride=dil), :].astype(o_ref.dtype)


def _proj_a(xb, w_qkva, tabs, bsz, seq, tm=512):
    t, k = xb.shape
    npb = seq // tm
    w = A_GROUP_WIDTH
    out_specs, out_shape = [], []
    for jj in range(3 * N_GROUPS):
        dil = DILS[jj % N_GROUPS]
        if dil == 1:
            out_specs.append(pl.BlockSpec((None, None, tm, w), lambda i: (i // npb, 0, i % npb, 0)))
        else:
            out_specs.append(pl.BlockSpec((None, dil, tm // dil, w), lambda i: (i // npb, 0, i % npb, 0)))
        out_shape.append(jax.ShapeDtypeStruct((bsz, dil, seq // dil, w), BF16))
    tab = pl.BlockSpec((tm, LANES), lambda i: (i, 0))
    return pl.pallas_call(
        _proj_a_kernel,
        grid=(t // tm,),
        in_specs=[pl.BlockSpec((tm, k), lambda i: (i, 0)),
                  pl.BlockSpec(w_qkva.shape, lambda i: (0, 0), pipeline_mode=pl.Buffered(1)), tab, tab, tab],
        out_specs=out_specs,
        out_shape=out_shape,
        scratch_shapes=[pltpu.VMEM((2 * A_HEADS, tm, HEAD_DIM), F32)],
        compiler_params=_cparams(("parallel",)),
        name="proj_mixer_a",
    )(xb, w_qkva, *tabs)


ATT_TT = 2048
ATT_UNROLL = 8


def _attn_kernel(*refs):
    ins = refs[:5 * N_GROUPS]
    o_ref = refs[5 * N_GROUPS]
    od, ld, on, ln = refs[5 * N_GROUPS + 1:5 * N_GROUPS + 5]
    kvbufs = refs[5 * N_GROUPS + 5:]
    it = pl.program_id(1)
    row = lax.broadcasted_iota(jnp.int32, (A_BLOCK, 2 * A_BLOCK), 0)
    col = lax.broadcasted_iota(jnp.int32, (A_BLOCK, 2 * A_BLOCK), 1)
    band = jnp.logical_and(col >= row, col <= row + A_BLOCK)
    bias_all = jnp.where(band, 0.0, NEG).astype(F32)
    bias_own = jnp.where(jnp.logical_and(band, col >= A_BLOCK), 0.0, NEG).astype(F32)
    nblocks = ATT_TT // A_BLOCK
    for g, dil in enumerate(DILS):
        q_ref, k_ref, v_ref, kh_ref, vh_ref = ins[5 * g:5 * g + 5]
        kbuf, vbuf = kvbufs[2 * g:2 * g + 2]
        nlb = nblocks // dil
        rows = ATT_TT // dil
        kbuf[:, 0:A_BLOCK, :] = kh_ref[...]
        kbuf[:, A_BLOCK:, :] = k_ref[...]
        vbuf[:, 0:A_BLOCK, :] = vh_ref[...]
        vbuf[:, A_BLOCK:, :] = v_ref[...]

        def block(c, carry, g=g, nlb=nlb, rows=rows, q_ref=q_ref, kbuf=kbuf, vbuf=vbuf):
            r = c // nlb
            nb = c % nlb
            off = pl.multiple_of(nb * A_BLOCK, A_BLOCK)
            q = q_ref[r, pl.ds(off, A_BLOCK), :]
            kk = kbuf[r, pl.ds(off, 2 * A_BLOCK), :]
            vv = vbuf[r, pl.ds(off, 2 * A_BLOCK), :]
            has_prev = jnp.logical_or(nb > 0, it > 0)
            s = _dot_nt(q, kk) + jnp.where(has_prev, bias_all, bias_own)
            m = jnp.max(s, axis=1, keepdims=True)
            p = jnp.exp(s - m)
            den = jnp.sum(p, axis=1, keepdims=True)
            o = _dot(p.astype(BF16), vv) / den
            dst = pl.multiple_of(r * rows + off, A_BLOCK)
            od[g, pl.ds(dst, A_BLOCK), :] = o
            ld[g, pl.ds(dst, A_BLOCK), :] = jnp.broadcast_to(m + jnp.log(den), (A_BLOCK, HEAD_DIM))
            return carry

        lax.fori_loop(0, nblocks, block, 0, unroll=ATT_UNROLL)
    for g, dil in enumerate(DILS):
        if dil == 1:
            continue
        rows = ATT_TT // dil
        for r in range(dil):
            on[g - 1, pl.ds(r, rows, stride=dil), :] = od[g, r * rows:(r + 1) * rows, :]
            ln[g - 1, pl.ds(r, rows, stride=dil), :] = ld[g, r * rows:(r + 1) * rows, :]
    step = 256
    for c in range(ATT_TT // step):
        sl = slice(c * step, (c + 1) * step)
        lse = [ld[0, sl, :]] + [ln[g - 1, sl, :] for g in range(1, N_GROUPS)]
        outs = [od[0, sl, :]] + [on[g - 1, sl, :] for g in range(1, N_GROUPS)]
        m = functools.reduce(jnp.maximum, lse)
        es = [jnp.exp(l - m) for l in lse]
        num = functools.reduce(lambda a, b: a + b, [e * o for e, o in zip(es, outs)])
        o_ref[sl, :] = (num / functools.reduce(lambda a, b: a + b, es)).astype(o_ref.dtype)


def _attention(qkv_sub, bsz, seq):
    specs, args = [], []
    for g, dil in enumerate(DILS):
        rows = ATT_TT // dil
        hb = rows // A_BLOCK
        cur = pl.BlockSpec((None, dil, rows, HEAD_DIM), lambda b, i, h: (b, 0, i, h))
        halo = pl.BlockSpec((None, dil, A_BLOCK, HEAD_DIM), lambda b, i, h, hb=hb: (b, 0, jnp.maximum(i * hb - 1, 0), h))
        specs += [cur, cur, cur, halo, halo]
        args += [qkv_sub[0][g], qkv_sub[1][g], qkv_sub[2][g], qkv_sub[1][g], qkv_sub[2][g]]
    return pl.pallas_call(
        _attn_kernel,
        grid=(bsz, seq // ATT_TT, A_HEADS),
        in_specs=specs,
        out_specs=pl.BlockSpec((None, ATT_TT, HEAD_DIM), lambda b, i, h: (b, i, h)),
        out_shape=jax.ShapeDtypeStruct((bsz, seq, A_GROUP_WIDTH), BF16),
        scratch_shapes=[pltpu.VMEM((N_GROUPS, ATT_TT, HEAD_DIM), F32)] * 2
        + [pltpu.VMEM((N_GROUPS - 1, ATT_TT, HEAD_DIM), F32)] * 2
        + [pltpu.VMEM((dil, ATT_TT // dil + A_BLOCK, HEAD_DIM), BF16) for dil in DILS for _ in range(2)],
        compiler_params=_cparams(("parallel", "parallel", "parallel")),
        name="dilated_attention",
    )(*args)


def _gate_kernel(bd_ref, arow_ref, dtrow_ref, beta_ref, gc_ref):
    x = bd_ref[...]
    beta_ref[...] = jax.nn.sigmoid(x)
    y = x + dtrow_ref[...]
    softplus = jnp.maximum(y, 0.0) + jnp.log(1.0 + jnp.exp(-jnp.abs(y)))
    g = arow_ref[...] * softplus
    tt = x.shape[0]
    ri = lax.broadcasted_iota(jnp.int32, (tt, tt), 0)
    ci = lax.broadcasted_iota(jnp.int32, (tt, tt), 1)
    tri = jnp.where(jnp.logical_and(ri // CHUNK == ci // CHUNK, ci <= ri), 1.0, 0.0).astype(F32)
    gc_ref[...] = jnp.dot(tri, g, precision=lax.Precision.HIGHEST, preferred_element_type=F32)


def _gates(bd, arow, dtrow, tt=512):
    t = bd.shape[0]
    spec = pl.BlockSpec((tt, LANES), lambda i: (i, 0))
    rspec = pl.BlockSpec((1, LANES), lambda i: (0, 0))
    return pl.pallas_call(
        _gate_kernel,
        grid=(t // tt,),
        in_specs=[spec, rspec, rspec],
        out_specs=[spec, spec],
        out_shape=[jax.ShapeDtypeStruct((t, LANES), F32)] * 2,
        compiler_params=_cparams(("parallel",)),
        name="deltanet_gates",
    )(bd, arow, dtrow)


HALO = 16


DN_PH = 4


def _dn_prep_kernel(q_ref, k_ref, v_ref, qh_ref, kh_ref, vh_ref, wq_ref, wk_ref, wv_ref,
                    beta_ref, gc_ref, gct_ref, qo_ref, ko_ref, vo_ref, a_ref, scr):
    t = pl.program_id(1)
    hh = pl.program_id(2)
    tt = q_ref.shape[0]

    def conv_silu(slot, x_ref, halo_ref, w_ref, hs):
        scr[slot, 0:HALO, :] = jnp.where(t > 0, halo_ref[:, hs].astype(F32), 0.0)
        scr[slot, HALO:, :] = x_ref[:, hs].astype(F32)
        w = w_ref[:, hs]
        y = scr[slot, HALO - 3:HALO - 3 + tt, :] * w[0:1, :]
        for j in range(1, CONV_K):
            y = y + scr[slot, HALO - 3 + j:HALO - 3 + j + tt, :] * w[j:j + 1, :]
        return y * jax.nn.sigmoid(y)

    def l2n(v):
        return v * lax.rsqrt(jnp.sum(v * v, axis=1, keepdims=True) + RMS_EPS)

    for hp in range(DN_PH):
        hs = slice(hp * HEAD_DIM, (hp + 1) * HEAD_DIM)
        qo_ref[:, hs] = (l2n(conv_silu(3 * hp, q_ref, qh_ref, wq_ref, hs)) * (HEAD_DIM ** -0.5)).astype(qo_ref.dtype)
        vo_ref[:, hs] = conv_silu(3 * hp + 1, v_ref, vh_ref, wv_ref, hs).astype(vo_ref.dtype)

    beta = beta_ref[...]
    gc = gc_ref[...]
    g2 = 2 * CHUNK
    ri = lax.broadcasted_iota(jnp.int32, (g2, g2), 0)
    ci = lax.broadcasted_iota(jnp.int32, (g2, g2), 1)
    strict = jnp.logical_and((ri >= CHUNK) == (ci >= CHUNK), ri > ci)
    lane = lax.broadcasted_iota(jnp.int32, (CHUNK, g2), 1)
    for hp in range(DN_PH):
        hs = slice(hp * HEAD_DIM, (hp + 1) * HEAD_DIM)
        h = hh * DN_PH + hp
        k = l2n(conv_silu(3 * hp + 2, k_ref, kh_ref, wk_ref, hs)).astype(BF16)
        ko_ref[:, hs] = k
        bcol = _col_of(beta, h)
        gcol = _col_of(gc, B_HEADS + h)
        grow = gct_ref[pl.ds(h, 1), :]
        for gi in range(tt // g2):
            rs = slice(gi * g2, (gi + 1) * g2)
            kb = k[rs, :]
            kbeta = (kb.astype(F32) * bcol[rs, :]).astype(BF16)
            diff = gcol[rs, :] - grow[:, rs]
            a = _dot_nt(kbeta, kb) * jnp.exp(jnp.where(strict, diff, NEG))
            a_ref[hp, gi] = jnp.where(lane < CHUNK, a[0:CHUNK, :], a[CHUNK:g2, :])


def _dn_prep(qkvb, conv_w, beta, gc, gct, tt):
    bsz, seq, _ = qkvb.shape
    nh = B_HEADS
    hb = tt // HALO
    hw = DN_PH * HEAD_DIM
    ng = nh // DN_PH

    def tok(off):
        return pl.BlockSpec((None, tt, hw), lambda b, t, h: (b, t, off + h))

    def halo(off):
        return pl.BlockSpec((None, HALO, hw), lambda b, t, h: (b, jnp.maximum(t * hb - 1, 0), off + h))

    def cw(off):
        return pl.BlockSpec((CONV_K, hw), lambda b, t, h: (0, off + h))

    full = pl.BlockSpec((None, tt, LANES), lambda b, t, h: (b, t, 0))
    gts = pl.BlockSpec((None, nh, tt), lambda b, t, h: (b, 0, t))
    ospec = pl.BlockSpec((None, tt, hw), lambda b, t, h: (b, t, h))
    aspec = pl.BlockSpec((None, DN_PH, tt // (2 * CHUNK), CHUNK, 2 * CHUNK), lambda b, t, h: (b, h, t, 0, 0))
    return pl.pallas_call(
        _dn_prep_kernel,
        grid=(bsz, seq // tt, ng),
        in_specs=[tok(0), tok(ng), tok(2 * ng), halo(0), halo(ng), halo(2 * ng),
                  cw(0), cw(ng), cw(2 * ng), full, full, gts],
        out_specs=[ospec, ospec, ospec, aspec],
        out_shape=[jax.ShapeDtypeStruct((bsz, seq, B_WIDTH), BF16)] * 3
        + [jax.ShapeDtypeStruct((bsz, nh, seq // (2 * CHUNK), CHUNK, 2 * CHUNK), F32)],
        scratch_shapes=[pltpu.VMEM((3 * DN_PH, tt + HALO, HEAD_DIM), F32)],
        compiler_params=_cparams(("parallel", "parallel", "arbitrary")),
        name="deltanet_prep",
    )(qkvb, qkvb, qkvb, qkvb, qkvb, qkvb, conv_w, conv_w, conv_w, beta, gc, gct)


SOLVE_TILES = 128
SUB = 8


def _tri_solve_kernel(a_ref, t_ref, at, tt):
    g2 = 2 * CHUNK

    def load_row(i, c):
        at[i] = a_ref[:, pl.ds(i, 1), :].reshape(SOLVE_TILES, g2).T
        return c

    lax.fori_loop(0, CHUNK, load_row, 0)
    tt[...] = jnp.zeros_like(tt)
    rowid = lax.broadcasted_iota(jnp.int32, (SUB, SOLVE_TILES), 0)
    nk = CHUNK // SUB

    def solve_row(i, c):
        acc = tuple(jnp.zeros((SUB, SOLVE_TILES), F32) for _ in range(2 * nk))
        for mb in range(nk):
            def apply_block(acc, mb=mb):
                new = list(acc)
                for m in range(mb * SUB, (mb + 1) * SUB):
                    a0 = jnp.broadcast_to(at[i, m:m + 1, :], (SUB, SOLVE_TILES))
                    a1 = jnp.broadcast_to(at[i, CHUNK + m:CHUNK + m + 1, :], (SUB, SOLVE_TILES))
                    for k in range(mb + 1):
                        new[k] = new[k] - a0 * tt[m, k * SUB:(k + 1) * SUB, :]
                        new[nk + k] = new[nk + k] - a1 * tt[m, CHUNK + k * SUB:CHUNK + (k + 1) * SUB, :]
                return tuple(new)

            acc = lax.cond(mb * SUB < i, apply_block, lambda a: a, acc)
        for k in range(nk):
            diag = jnp.where(rowid + k * SUB == i, 1.0, 0.0).astype(F32)
            tt[i, k * SUB:(k + 1) * SUB, :] = acc[k] + diag
            tt[i, CHUNK + k * SUB:CHUNK + (k + 1) * SUB, :] = acc[nk + k] + diag
        return c

    lax.fori_loop(0, CHUNK, solve_row, 0)

    def store_row(i, c):
        t_ref[:, pl.ds(i, 1), :] = tt[i].T.reshape(SOLVE_TILES, 1, g2)
        return c

    lax.fori_loop(0, CHUNK, store_row, 0)


def _tri_solve(a_p):
    nt = a_p.shape[0]
    spec = pl.BlockSpec((SOLVE_TILES, CHUNK, 2 * CHUNK), lambda n: (n, 0, 0))
    return pl.pallas_call(
        _tri_solve_kernel,
        grid=(nt // SOLVE_TILES,),
        in_specs=[spec],
        out_specs=spec,
        out_shape=jax.ShapeDtypeStruct(a_p.shape, F32),
        scratch_shapes=[pltpu.VMEM((CHUNK, 2 * CHUNK, SOLVE_TILES), F32)] * 2,
        compiler_params=_cparams(("parallel",)),
        name="deltanet_tri_solve",
    )(a_p)


DN_HP = 4


def _dn_main_kernel(q_ref, k_ref, v_ref, beta_ref, gc_ref, gct_ref, t_ref, z_ref, nw_ref, o_ref, state):
    t = pl.program_id(2)
    hh = pl.program_id(1)
    tt = q_ref.shape[0]
    g2 = 2 * CHUNK

    @pl.when(t == 0)
    def _():
        state[...] = jnp.zeros_like(state)

    ri = lax.broadcasted_iota(jnp.int32, (g2, g2), 0)
    ci = lax.broadcasted_iota(jnp.int32, (g2, g2), 1)
    incl = jnp.logical_and((ri >= CHUNK) == (ci >= CHUNK), ri >= ci)
    lane = lax.broadcasted_iota(jnp.int32, (CHUNK, g2), 1)
    nw = nw_ref[...]
    beta = beta_ref[...]
    gc = gc_ref[...]
    bcols, gcols, grows, states = [], [], [], []
    for hp in range(DN_HP):
        h = hh * DN_HP + hp
        bcols.append(_col_of(beta, h))
        gcols.append(_col_of(gc, B_HEADS + h))
        grows.append(gct_ref[pl.ds(h, 1), :])
        states.append(state[hp])
    for gi in range(tt // g2):
        rs = slice(gi * g2, (gi + 1) * g2)
        pre = []
        for hp in range(DN_HP):
            hs = slice(hp * HEAD_DIM, (hp + 1) * HEAD_DIM)
            qb = q_ref[rs, hs]
            kb = k_ref[rs, hs]
            kf = kb.astype(F32)
            b = bcols[hp][rs, :]
            gcl = gcols[hp][rs, :]
            eg = jnp.exp(gcl)
            kbeta = kf * b
            rhs = jnp.concatenate([v_ref[rs, hs].astype(F32) * b, kbeta * eg], axis=1).astype(BF16)
            tp = t_ref[hp, gi]
            tbd = jnp.concatenate([jnp.where(lane < CHUNK, tp, 0.0), jnp.where(lane >= CHUNK, tp, 0.0)], axis=0)
            uw = _dot(tbd.astype(BF16), rhs).astype(BF16)
            diff = gcl - grows[hp][:, rs]
            attn = (_dot_nt(qb, kb) * jnp.exp(jnp.where(incl, diff, NEG))).astype(BF16)
            auw = _dot(attn, uw)
            qeff = (qb.astype(F32) * eg - auw[:, HEAD_DIM:]).astype(BF16)
            per_chunk = []
            for cc in range(2):
                cs = slice(cc * CHUNK, (cc + 1) * CHUNK)
                glast = gcl[cc * CHUNK + CHUNK - 1:cc * CHUNK + CHUNK, :]
                ktail = (kf[cs, :] * jnp.exp(glast - gcl[cs, :])).astype(BF16)
                kuw = _dot_tn(ktail, uw[cs, :])
                lhs = jnp.concatenate([kuw[:, HEAD_DIM:].astype(BF16), qeff[cs, :]], axis=0)
                per_chunk.append((lhs, kuw[:, :HEAD_DIM], jnp.exp(glast), auw[cs, :HEAD_DIM]))
            pre.append(per_chunk)
        outs = [[] for _ in range(DN_HP)]
        for cc in range(2):
            for hp in range(DN_HP):
                lhs, ku, decay, au = pre[hp][cc]
                s = states[hp]
                xs = _dot(lhs, s.astype(BF16))
                outs[hp].append(xs[HEAD_DIM:, :] + au)
                states[hp] = s * decay + ku - xs[:HEAD_DIM, :]
        for hp in range(DN_HP):
            hs = slice(hp * HEAD_DIM, (hp + 1) * HEAD_DIM)
            o = jnp.concatenate(outs[hp], axis=0)
            o = o * lax.rsqrt(jnp.mean(o * o, axis=1, keepdims=True) + RMS_EPS) * nw
            z = z_ref[rs, hs].astype(F32)
            o_ref[rs, hs] = (o * (z * jax.nn.sigmoid(z))).astype(o_ref.dtype)
    for hp in range(DN_HP):
        state[hp] = states[hp]


def _dn_main(qn, kn, vn, beta, gc, gct, t_p, zg, norm_w, tt):
    bsz, seq, _ = qn.shape
    nh = B_HEADS
    hw = DN_HP * HEAD_DIM
    tok = pl.BlockSpec((None, tt, hw), lambda b, h, t: (b, t, h))
    full = pl.BlockSpec((None, tt, LANES), lambda b, h, t: (b, t, 0))
    gts = pl.BlockSpec((None, nh, tt), lambda b, h, t: (b, 0, t))
    tspec = pl.BlockSpec((None, DN_HP, tt // (2 * CHUNK), CHUNK, 2 * CHUNK), lambda b, h, t: (b, h, t, 0, 0))
    nspec = pl.BlockSpec((1, HEAD_DIM), lambda b, h, t: (0, 0))
    return pl.pallas_call(
        _dn_main_kernel,
        grid=(bsz, nh // DN_HP, seq // tt),
        in_specs=[tok, tok, tok, full, full, gts, tspec, tok, nspec],
        out_specs=tok,
        out_shape=jax.ShapeDtypeStruct((bsz, seq, B_WIDTH), BF16),
        scratch_shapes=[pltpu.VMEM((DN_HP, HEAD_DIM, HEAD_DIM), F32)],
        compiler_params=_cparams(("parallel", "parallel", "arbitrary")),
        name="deltanet_main",
    )(qn, kn, vn, beta, gc, gct, t_p, zg, norm_w)


def _mix_out_kernel(ya_ref, yb_ref, ga_ref, gb_ref, x_ref,
                    wa_ref, wb_ref, wo_ref, g_ref, b_ref, xo_ref, xbo_ref):
    ma = _dot(ya_ref[...], wa_ref[...])
    mb = _dot(yb_ref[...], wb_ref[...])
    merged = jax.nn.sigmoid(ga_ref[...].astype(F32)) * ma + jax.nn.sigmoid(gb_ref[...].astype(F32)) * mb
    r = _dot(merged.astype(BF16), wo_ref[...])
    y = _layer_norm(DN_ALPHA * x_ref[...] + r, g_ref[...], b_ref[...])
    xo_ref[...] = y
    xbo_ref[...] = y.astype(BF16)


def _mix_out(ya, yb, zg, x, wa, wb, wo, g, b, tm=512):
    t = x.shape[0]
    d = D_MODEL
    aw = A_GROUP_WIDTH

    def rows(wd, c=0):
        return pl.BlockSpec((tm, wd), lambda i: (i, c))

    def whole(shape):
        return pl.BlockSpec(shape, lambda i: (0, 0))

    return pl.pallas_call(
        _mix_out_kernel,
        grid=(t // tm,),
        in_specs=[rows(aw), rows(d), rows(d, 1), rows(d, 2), rows(d),
                                   whole((aw, d)), whole((d, d)), whole((d, d)), whole((1, d)), whole((1, d))],
        out_specs=[rows(d), rows(d)],
        out_shape=[jax.ShapeDtypeStruct((t, d), F32), jax.ShapeDtypeStruct((t, d), BF16)],
        compiler_params=_cparams(("parallel",)),
        name="mix_out",
    )(ya, yb, zg, zg, x, wa, wb, wo, g, b)


FFN_FC = 256


def _ffn_kernel(xb_ref, x_ref, wg_ref, wu_ref, wd_ref, g_ref, b_ref, xo_ref, xbo_ref):
    xb = xb_ref[...]
    acc = jnp.zeros(x_ref.shape, F32)
    for c in range(wg_ref.shape[1] // FFN_FC):
        cs = slice(c * FFN_FC, (c + 1) * FFN_FC)
        gt = _dot(xb, wg_ref[:, cs])
        up = _dot(xb, wu_ref[:, cs])
        hh = (gt * jax.nn.sigmoid(gt) * up).astype(BF16)
        acc = acc + _dot(hh, wd_ref[cs, :])
    y = _layer_norm(DN_ALPHA * x_ref[...] + acc, g_ref[...], b_ref[...])
    xo_ref[...] = y
    xbo_ref[...] = y.astype(BF16)


def _ffn(xb, x, wg, wu, wd, g, b, tm=512):
    t, d = x.shape
    rows = pl.BlockSpec((tm, d), lambda i: (i, 0))
    vec = pl.BlockSpec((1, d), lambda i: (0, 0))
    one = pl.Buffered(1)
    return pl.pallas_call(
        _ffn_kernel,
        grid=(t // tm,),
        in_specs=[rows, rows,
                  pl.BlockSpec(wg.shape, lambda i: (0, 0), pipeline_mode=one),
                  pl.BlockSpec(wu.shape, lambda i: (0, 0), pipeline_mode=one),
                  pl.BlockSpec(wd.shape, lambda i: (0, 0), pipeline_mode=one), vec, vec],
        out_specs=[rows, rows],
        out_shape=[jax.ShapeDtypeStruct((t, d), F32), jax.ShapeDtypeStruct((t, d), BF16)],
        compiler_params=_cparams(("parallel",)),
        name="dense_swiglu",
    )(xb, x, wg, wu, wd, g, b)


def _router_kernel(x_ref, rw_ref, tri_ref, gates_ref, rank_ref, selt_ref, cnt_ref):
    logits = jnp.dot(x_ref[...], rw_ref[...], precision=lax.Precision.HIGHEST, preferred_element_type=F32)
    lane = lax.broadcasted_iota(jnp.int32, logits.shape, 1)
    lanef = lane.astype(F32)
    ninf = -jnp.inf
    lg = jnp.where(lane < N_EXPERTS, logits, ninf)
    m1 = jnp.max(lg, axis=1, keepdims=True)
    i1 = jnp.min(jnp.where(lg == m1, lanef, float(LANES)), axis=1, keepdims=True)
    lg2 = jnp.where(lanef == i1, ninf, lg)
    m2 = jnp.max(lg2, axis=1, keepdims=True)
    i2 = jnp.min(jnp.where(lg2 == m2, lanef, float(LANES)), axis=1, keepdims=True)
    e = jnp.exp(m2 - m1)
    w1 = 1.0 / (1.0 + e)
    w2 = e / (1.0 + e)
    gates = jnp.where(lanef == i1, w1, jnp.where(lanef == i2, w2, 0.0))
    gates_ref[...] = gates
    sel = gates > 0.0
    onef = jnp.where(sel, 1.0, 0.0)
    rank = _dot(tri_ref[...], onef.astype(BF16))
    rank_ref[...] = rank
    selt_ref[...] = jnp.where(sel, rank, -1.0).T[0:N_EXPERTS, :]
    cnt_ref[...] = jnp.broadcast_to(jnp.sum(onef, axis=0, keepdims=True), cnt_ref.shape).astype(jnp.int32)


def _router(x, rw_pad, tm):
    t, d = x.shape
    nt = t // tm
    tri = jnp.tril(jnp.ones((tm, tm), BF16), -1)
    rows = pl.BlockSpec((tm, LANES), lambda i: (i, 0))
    return pl.pallas_call(
        _router_kernel,
        grid=(nt,),
        in_specs=[pl.BlockSpec((tm, d), lambda i: (i, 0)), pl.BlockSpec((d, LANES), lambda i: (0, 0)),
                  pl.BlockSpec((tm, tm), lambda i: (0, 0))],
        out_specs=[rows, rows, pl.BlockSpec((None, N_EXPERTS, tm), lambda i: (i, 0, 0)),
                   pl.BlockSpec((None, 8, LANES), lambda i: (i, 0, 0))],
        out_shape=[jax.ShapeDtypeStruct((t, LANES), F32), jax.ShapeDtypeStruct((t, LANES), F32),
                   jax.ShapeDtypeStruct((nt, N_EXPERTS, tm), F32), jax.ShapeDtypeStruct((nt, 8, LANES), jnp.int32)],
        compiler_params=_cparams(("parallel",)),
        name="moe_router",
    )(x, rw_pad, tri)


MOE_TM = 1024
MOE_CAP = 320
MOE_FC = 512


def _expert_kernel(xb_ref, selt_ref, wg_ref, wu_ref, wd_ref, oc_ref):
    e = pl.program_id(0)
    tm = xb_ref.shape[0]
    sel_row = selt_ref[pl.ds(e, 1), :]
    cidx = lax.broadcasted_iota(jnp.int32, (MOE_CAP, tm), 0).astype(F32)
    onehot = jnp.where(sel_row == cidx, 1.0, 0.0).astype(BF16)
    xc = _dot(onehot, xb_ref[...]).astype(BF16)
    acc = jnp.zeros((MOE_CAP, D_MODEL), F32)
    for c in range(wg_ref.shape[1] // MOE_FC):
        cs = slice(c * MOE_FC, (c + 1) * MOE_FC)
        gt = _dot(xc, wg_ref[:, cs])
        up = _dot(xc, wu_ref[:, cs])
        hh = (gt * jax.nn.sigmoid(gt) * up).astype(BF16)
        acc = acc + _dot(hh, wd_ref[cs, :])
    oc_ref[...] = acc.astype(oc_ref.dtype)


def _experts(xb, selt, wg, wu, wd):
    t, d = xb.shape
    ne, _, dex = wg.shape
    nt = t // MOE_TM
    one = pl.Buffered(1)
    return pl.pallas_call(
        _expert_kernel,
        grid=(ne, nt),
        in_specs=[pl.BlockSpec((MOE_TM, d), lambda e, i: (i, 0)),
                  pl.BlockSpec((None, N_EXPERTS, MOE_TM), lambda e, i: (i, 0, 0)),
                  pl.BlockSpec((None, d, dex), lambda e, i: (e, 0, 0), pipeline_mode=one),
                  pl.BlockSpec((None, d, dex), lambda e, i: (e, 0, 0), pipeline_mode=one),
                  pl.BlockSpec((None, dex, d), lambda e, i: (e, 0, 0), pipeline_mode=one)],
        out_specs=pl.BlockSpec((None, None, MOE_CAP, d), lambda e, i: (e, i, 0, 0)),
        out_shape=jax.ShapeDtypeStruct((ne, nt, MOE_CAP, d), BF16),
        compiler_params=_cparams(("arbitrary", "arbitrary")),
        name="moe_experts",
    )(xb, selt, wg, wu, wd)


def _combine_kernel(x_ref, gates_ref, rank_ref, oc_ref, g_ref, b_ref, xo_ref):
    tm = x_ref.shape[0]
    gates = gates_ref[...]
    rank = rank_ref[...]
    lane = lax.broadcasted_iota(jnp.int32, (tm, MOE_CAP), 1).astype(F32)
    y = jnp.zeros((tm, D_MODEL), F32)
    for e in range(N_EXPERTS):
        gcol = gates[:, e:e + 1]
        scol = jnp.where(gcol > 0.0, rank[:, e:e + 1], -1.0)
        onehot = jnp.where(scol == lane, 1.0, 0.0).astype(BF16)
        y = y + _dot(onehot, oc_ref[e]) * gcol
    xo_ref[...] = _layer_norm(DN_ALPHA * x_ref[...] + y, g_ref[...], b_ref[...])


def _combine(x, gates, rank, oc, g, b):
    t, d = x.shape
    ne = oc.shape[0]
    rows = pl.BlockSpec((MOE_TM, d), lambda i: (i, 0))
    lrows = pl.BlockSpec((MOE_TM, LANES), lambda i: (i, 0))
    vec = pl.BlockSpec((1, d), lambda i: (0, 0))
    return pl.pallas_call(
        _combine_kernel,
        grid=(t // MOE_TM,),
        in_specs=[rows, lrows, lrows, pl.BlockSpec((ne, None, MOE_CAP, d), lambda i: (0, i, 0, 0)), vec, vec],
        out_specs=rows,
        out_shape=jax.ShapeDtypeStruct((t, d), F32),
        compiler_params=_cparams(("parallel",)),
        name="moe_combine",
    )(x, gates, rank, oc, g, b)


def _moe_kernel(xb_ref, x_ref, gates_ref, wg_ref, wu_ref, wd_ref, g_ref, b_ref, xo_ref, acc):
    e = pl.program_id(1)
    f = pl.program_id(2)

    @pl.when(jnp.logical_and(e == 0, f == 0))
    def _():
        acc[...] = jnp.zeros_like(acc)

    xb = xb_ref[...]
    gate = _col_of(gates_ref[...], e)
    gt = _dot(xb, wg_ref[...])
    up = _dot(xb, wu_ref[...])
    hh = (gt * jax.nn.sigmoid(gt) * up * gate).astype(BF16)
    acc[...] += _dot(hh, wd_ref[...])

    @pl.when(jnp.logical_and(e == pl.num_programs(1) - 1, f == pl.num_programs(2) - 1))
    def _():
        xo_ref[...] = _layer_norm(DN_ALPHA * x_ref[...] + acc[...], g_ref[...], b_ref[...])


def _moe(xb, x, gates, wg, wu, wd, g, b, tm=1024, tf=512):
    t, d = x.shape
    ne, _, dex = wg.shape
    rows = pl.BlockSpec((tm, d), lambda i, e, f: (i, 0))
    vec = pl.BlockSpec((1, d), lambda i, e, f: (0, 0))
    return pl.pallas_call(
        _moe_kernel,
        grid=(t // tm, ne, dex // tf),
        in_specs=[rows, rows, pl.BlockSpec((tm, LANES), lambda i, e, f: (i, 0)),
                  pl.BlockSpec((None, d, tf), lambda i, e, f: (e, 0, f)),
                  pl.BlockSpec((None, d, tf), lambda i, e, f: (e, 0, f)),
                  pl.BlockSpec((None, tf, d), lambda i, e, f: (e, f, 0)), vec, vec],
        out_specs=rows,
        out_shape=jax.ShapeDtypeStruct((t, d), F32),
        scratch_shapes=[pltpu.VMEM((tm, d), F32)],
        compiler_params=_cparams(("parallel", "arbitrary", "arbitrary")),
        name="moe_swiglu",
    )(xb, x, gates, wg, wu, wd, g, b)


def _rope_tables(positions):
    half = ROT_DIM // 2
    inv_freq = ROPE_THETA ** (-jnp.arange(0, ROT_DIM, 2, dtype=F32) / ROT_DIM)
    ang = positions.astype(F32)[..., None] * inv_freq
    cos, sin = jnp.cos(ang), jnp.sin(ang)
    shp = cos.shape[:-1]
    c = jnp.concatenate([cos, cos, jnp.ones(shp + (LANES - ROT_DIM,), F32)], axis=-1)
    s1 = jnp.concatenate([-sin, jnp.zeros(shp + (LANES - half,), F32)], axis=-1)
    s2 = jnp.concatenate([jnp.zeros(shp + (half,), F32), sin, jnp.zeros(shp + (LANES - ROT_DIM,), F32)], axis=-1)
    return c, s1, s2


def kernel(x, positions, w_in, conv_w, a_log, dt_bias, dn_norm_w, w_branch_a, w_branch_b, w_out, ln1_g, ln1_b,
           ffn_w_gate, ffn_w_up, ffn_w_down, router_w, moe_w_gate, moe_w_up, moe_w_down, ln2_g, ln2_b):
    bsz, seq, d = x.shape
    t = bsz * seq
    nh = B_HEADS
    qa_w = 3 * A_QKV_WIDTH
    o_qkvb = qa_w
    o_z = o_qkvb + 3 * B_WIDTH
    o_bd = o_z + B_WIDTH
    o_gates = o_bd + 2 * nh

    tabs = [tb.reshape(t, LANES) for tb in _rope_tables(positions)]

    dn_tt = 512
    xf = x.reshape(t, d)
    xb = xf.astype(BF16)
    for layer in range(DEPTH):
        w = w_in[layer]
        w_qkva = w[:, :qa_w].astype(BF16)
        w_qkvb = w[:, o_qkvb:o_z].astype(BF16)
        w_zg = jnp.concatenate([w[:, o_z:o_bd], w[:, o_gates:]], axis=1).astype(BF16)
        w_bd = jnp.pad(w[:, o_bd:o_gates], ((0, 0), (0, LANES - 2 * nh))).astype(BF16)

        subs = _proj_a(xb, w_qkva, tabs, bsz, seq)
        qkvb = _matmul(xb, w_qkvb, BF16, 1024, 1024)
        zg = _matmul(xb, w_zg, BF16, 1024, 1024)
        bd = _matmul(xb, w_bd, F32, 1024, LANES)

        ya = _attention([subs[p * N_GROUPS:(p + 1) * N_GROUPS] for p in range(3)], bsz, seq)

        zeros8 = jnp.zeros((nh,), F32)
        arow = jnp.concatenate([zeros8, -jnp.exp(a_log[layer].astype(F32)), jnp.zeros((LANES - 2 * nh,), F32)])[None, :]
        dtrow = jnp.concatenate([zeros8, dt_bias[layer].astype(F32), jnp.zeros((LANES - 2 * nh,), F32)])[None, :]
        beta, gc = _gates(bd, arow, dtrow)
        beta3 = beta.reshape(bsz, seq, LANES)
        gc3 = gc.reshape(bsz, seq, LANES)
        gct = gc3[:, :, nh:2 * nh].transpose(0, 2, 1)
        qn, kn, vn, a_p = _dn_prep(qkvb.reshape(bsz, seq, 3 * B_WIDTH), conv_w[layer].astype(F32),
                                   beta3, gc3, gct, dn_tt)
        ng = seq // (2 * CHUNK)
        t_p = _tri_solve(a_p.reshape(bsz * nh * ng, CHUNK, 2 * CHUNK)).reshape(bsz, nh, ng, CHUNK, 2 * CHUNK)
        yb = _dn_main(qn, kn, vn, beta3, gc3, gct, t_p, zg.reshape(bsz, seq, 3 * B_WIDTH),
                      dn_norm_w[layer].astype(F32)[None, :], dn_tt)

        xf, xb = _mix_out(ya.reshape(t, A_GROUP_WIDTH), yb.reshape(t, B_WIDTH), zg, xf,
                          w_branch_a[layer].astype(BF16), w_branch_b[layer].astype(BF16), w_out[layer].astype(BF16),
                          ln1_g[layer].astype(F32)[None, :], ln1_b[layer].astype(F32)[None, :])

        g2 = ln2_g[layer].astype(F32)[None, :]
        b2 = ln2_b[layer].astype(F32)[None, :]
        if layer % 2 == 0:
            i = layer // 2
            xf, xb = _ffn(xb, xf, ffn_w_gate[i].astype(BF16), ffn_w_up[i].astype(BF16), ffn_w_down[i].astype(BF16), g2, b2)
        else:
            i = layer // 2
            rw = jnp.pad(router_w[i].astype(F32), ((0, 0), (0, LANES - N_EXPERTS)))
            gates, rank, selt, cnt = _router(xf, rw, MOE_TM)
            wg = moe_w_gate[i].astype(BF16)
            wu = moe_w_up[i].astype(BF16)
            wd = moe_w_down[i].astype(BF16)

            def routed(xb, xf, gates, rank, selt):
                return _combine(xf, gates, rank, _experts(xb, selt, wg, wu, wd), g2, b2)

            def dense(xb, xf, gates, rank, selt):
                return _moe(xb, xf, gates, wg, wu, wd, g2, b2)

            xf = lax.cond(jnp.max(cnt) > MOE_CAP, dense, routed, xb, xf, gates, rank, selt)
            xb = xf.astype(BF16)
    return xf.reshape(bsz, seq, d)
```

```python
import functools
import math

import jax
import jax.numpy as jnp
from jax import lax
from jax.experimental import pallas as pl
from jax.experimental.pallas import tpu as pltpu

F32 = jnp.float32
BF16 = jnp.bfloat16

D_MODEL = 1024
DEPTH = 2
A_PAIRS = ((128, 1), (512, 4), (2048, 16))
A_HEADS = 4
HEAD_DIM = 128
A_GROUP_WIDTH = A_HEADS * HEAD_DIM
A_QKV_WIDTH = len(A_PAIRS) * A_GROUP_WIDTH
A_BLOCK = 128
ROPE_THETA = 500000.0
ROT_DIM = HEAD_DIM // 4
B_HEADS = 8
B_WIDTH = B_HEADS * HEAD_DIM
CONV_K = 4
CHUNK = 64
N_EXPERTS = 8
DN_ALPHA = (2 * DEPTH) ** 0.25
LN_EPS = 1e-5
RMS_EPS = 1e-6
NEG = -1e30
LANES = 128
VMEM_LIMIT = 56 * 1024 * 1024


def _cparams(sem):
    return pltpu.CompilerParams(dimension_semantics=sem, vmem_limit_bytes=VMEM_LIMIT)


def _dot(a, b):
    return jnp.dot(a, b, preferred_element_type=F32)


def _dot_nt(a, b):
    return lax.dot_general(a, b, (((1,), (1,)), ((), ())), preferred_element_type=F32)


def _dot_tn(a, b):
    return lax.dot_general(a, b, (((0,), (0,)), ((), ())), preferred_element_type=F32)


def _col_of(x, idx):
    lane = lax.broadcasted_iota(jnp.int32, x.shape, 1)
    return jnp.sum(jnp.where(lane == idx, x, 0.0), axis=1, keepdims=True)


def _layer_norm(v, g, b):
    mu = jnp.mean(v, axis=-1, keepdims=True)
    c = v - mu
    var = jnp.mean(c * c, axis=-1, keepdims=True)
    return c * lax.rsqrt(var + LN_EPS) * g + b


def _rope(t, c, s1, s2):
    return t * c + pltpu.roll(t, LANES - ROT_DIM // 2, 1) * s1 + pltpu.roll(t, ROT_DIM // 2, 1) * s2


PB_FC = 1024


def _proj_b_kernel(x_ref, w_ref, wbd_ref, arow_ref, dtrow_ref, h_ref, beta_ref, gc_ref):
    x = x_ref[...]
    for c in range(w_ref.shape[1] // PB_FC):
        cs = slice(c * PB_FC, (c + 1) * PB_FC)
        h_ref[:, cs] = _dot(x, w_ref[:, cs]).astype(h_ref.dtype)
    bd = _dot(x, wbd_ref[...])
    beta_ref[...] = jax.nn.sigmoid(bd)
    y = bd + dtrow_ref[...]
    softplus = jnp.maximum(y, 0.0) + jnp.log(1.0 + jnp.exp(-jnp.abs(y)))
    g = arow_ref[...] * softplus
    tm = bd.shape[0]
    ri = lax.broadcasted_iota(jnp.int32, (tm, tm), 0)
    ci = lax.broadcasted_iota(jnp.int32, (tm, tm), 1)
    tri = jnp.where(jnp.logical_and(ri // CHUNK == ci // CHUNK, ci <= ri), 1.0, 0.0).astype(F32)
    gc_ref[...] = jnp.dot(tri, g, precision=lax.Precision.HIGHEST, preferred_element_type=F32)


def _proj_b(xb, w, w_bd, arow, dtrow, tm=512):
    t, k = xb.shape
    n = w.shape[1]
    one = pl.Buffered(1)
    lrow = pl.BlockSpec((tm, LANES), lambda i: (i, 0))
    rspec = pl.BlockSpec((1, LANES), lambda i: (0, 0))
    return pl.pallas_call(
        _proj_b_kernel,
        grid=(t // tm,),
        in_specs=[pl.BlockSpec((tm, k), lambda i: (i, 0)),
                  pl.BlockSpec(w.shape, lambda i: (0, 0), pipeline_mode=one),
                  pl.BlockSpec(w_bd.shape, lambda i: (0, 0), pipeline_mode=one), rspec, rspec],
        out_specs=[pl.BlockSpec((tm, n), lambda i: (i, 0)), lrow, lrow],
        out_shape=[jax.ShapeDtypeStruct((t, n), BF16)] + [jax.ShapeDtypeStruct((t, LANES), F32)] * 2,
        compiler_params=_cparams(("parallel",)),
        name="proj_mixer_b",
    )(xb, w, w_bd, arow, dtrow)


N_GROUPS = len(A_PAIRS)
DILS = tuple(d for _, d in A_PAIRS)


def _proj_a_kernel(x_ref, w_ref, c_ref, s1_ref, s2_ref, *refs):
    outs = refs[:3 * N_GROUPS]
    scr = refs[3 * N_GROUPS]
    tm = x_ref.shape[0]
    w = A_GROUP_WIDTH
    scale = 1.0 / math.sqrt(HEAD_DIM)
    x = x_ref[...]
    for jj in range(3 * N_GROUPS):
        p, g = divmod(jj, N_GROUPS)
        dil = DILS[g]
        o_ref = outs[jj]
        acc = _dot(x, w_ref[:, jj * w:(jj + 1) * w])
        for h in range(A_HEADS):
            sl = slice(h * HEAD_DIM, (h + 1) * HEAD_DIM)
            t = acc[:, sl]
            if p < 2:
                t = _rope(t, c_ref[...], s1_ref[...], s2_ref[...])
            if p == 0:
                t = t * scale
            if dil == 1:
                o_ref[:, sl] = t.astype(o_ref.dtype)
            else:
                slot = (jj % 2) * A_HEADS + h
                scr[slot] = t
                for r in range(dil):
                    o_ref[r, :, sl] = scr[slot, pl.ds(r, tm // dil, stride=dil), :].astype(o_ref.dtype)


def _proj_a(xb, w_qkva, tabs, bsz, seq, tm=512):
    t, k = xb.shape
    npb = seq // tm
    w = A_GROUP_WIDTH
    out_specs, out_shape = [], []
    for jj in range(3 * N_GROUPS):
        dil = DILS[jj % N_GROUPS]
        if dil == 1:
            out_specs.append(pl.BlockSpec((None, None, tm, w), lambda i: (i // npb, 0, i % npb, 0)))
        else:
            out_specs.append(pl.BlockSpec((None, dil, tm // dil, w), lambda i: (i // npb, 0, i % npb, 0)))
        out_shape.append(jax.ShapeDtypeStruct((bsz, dil, seq // dil, w), BF16))
    tab = pl.BlockSpec((tm, LANES), lambda i: (i, 0))
    return pl.pallas_call(
        _proj_a_kernel,
        grid=(t // tm,),
        in_specs=[pl.BlockSpec((tm, k), lambda i: (i, 0)),
                  pl.BlockSpec(w_qkva.shape, lambda i: (0, 0), pipeline_mode=pl.Buffered(1)), tab, tab, tab],
        out_specs=out_specs,
        out_shape=out_shape,
        scratch_shapes=[pltpu.VMEM((2 * A_HEADS, tm, HEAD_DIM), F32)],
        compiler_params=_cparams(("parallel",)),
        name="proj_mixer_a",
    )(xb, w_qkva, *tabs)


ATT_TT = 2048
ATT_UNROLL = 8


def _attn_kernel(*refs):
    ins = refs[:5 * N_GROUPS]
    o_ref = refs[5 * N_GROUPS]
    od, ld, on, ln = refs[5 * N_GROUPS + 1:5 * N_GROUPS + 5]
    kvbufs = refs[5 * N_GROUPS + 5:]
    it = pl.program_id(1)
    row = lax.broadcasted_iota(jnp.int32, (A_BLOCK, 2 * A_BLOCK), 0)
    col = lax.broadcasted_iota(jnp.int32, (A_BLOCK, 2 * A_BLOCK), 1)
    band = jnp.logical_and(col >= row, col <= row + A_BLOCK)
    bias_all = jnp.where(band, 0.0, NEG).astype(F32)
    bias_own = jnp.where(jnp.logical_and(band, col >= A_BLOCK), 0.0, NEG).astype(F32)
    nblocks = ATT_TT // A_BLOCK
    for g, dil in enumerate(DILS):
        q_ref, k_ref, v_ref, kh_ref, vh_ref = ins[5 * g:5 * g + 5]
        kbuf, vbuf = kvbufs[2 * g:2 * g + 2]
        nlb = nblocks // dil
        rows = ATT_TT // dil
        kbuf[:, 0:A_BLOCK, :] = kh_ref[...]
        kbuf[:, A_BLOCK:, :] = k_ref[...]
        vbuf[:, 0:A_BLOCK, :] = vh_ref[...]
        vbuf[:, A_BLOCK:, :] = v_ref[...]

        def block(c, carry, g=g, nlb=nlb, rows=rows, q_ref=q_ref, kbuf=kbuf, vbuf=vbuf):
            r = c // nlb
            nb = c % nlb
            off = pl.multiple_of(nb * A_BLOCK, A_BLOCK)
            q = q_ref[r, pl.ds(off, A_BLOCK), :]
            kk = kbuf[r, pl.ds(off, 2 * A_BLOCK), :]
            vv = vbuf[r, pl.ds(off, 2 * A_BLOCK), :]
            has_prev = jnp.logical_or(nb > 0, it > 0)
            s = _dot_nt(q, kk) + jnp.where(has_prev, bias_all, bias_own)
            m = jnp.max(s, axis=1, keepdims=True)
            p = jnp.exp(s - m)
            den = jnp.sum(p, axis=1, keepdims=True)
            o = _dot(p.astype(BF16), vv) / den
            dst = pl.multiple_of(r * rows + off, A_BLOCK)
            od[g, pl.ds(dst, A_BLOCK), :] = o
            ld[g, pl.ds(dst, A_BLOCK), :] = jnp.broadcast_to(m + jnp.log(den), (A_BLOCK, HEAD_DIM))
            return carry

        lax.fori_loop(0, nblocks, block, 0, unroll=ATT_UNROLL)
    for g, dil in enumerate(DILS):
        if dil == 1:
            continue
        rows = ATT_TT // dil
        for r in range(dil):
            on[g - 1, pl.ds(r, rows, stride=dil), :] = od[g, r * rows:(r + 1) * rows, :]
            ln[g - 1, pl.ds(r, rows, stride=dil), :] = ld[g, r * rows:(r + 1) * rows, :]
    step = 256
    for c in range(ATT_TT // step):
        sl = slice(c * step, (c + 1) * step)
        lse = [ld[0, sl, :]] + [ln[g - 1, sl, :] for g in range(1, N_GROUPS)]
        outs = [od[0, sl, :]] + [on[g - 1, sl, :] for g in range(1, N_GROUPS)]
        m = functools.reduce(jnp.maximum, lse)
        es = [jnp.exp(l - m) for l in lse]
        num = functools.reduce(lambda a, b: a + b, [e * o for e, o in zip(es, outs)])
        o_ref[sl, :] = (num / functools.reduce(lambda a, b: a + b, es)).astype(o_ref.dtype)


def _attention(qkv_sub, bsz, seq):
    specs, args = [], []
    for g, dil in enumerate(DILS):
        rows = ATT_TT // dil
        hb = rows // A_BLOCK
        cur = pl.BlockSpec((None, dil, rows, HEAD_DIM), lambda b, i, h: (b, 0, i, h))
        halo = pl.BlockSpec((None, dil, A_BLOCK, HEAD_DIM), lambda b, i, h, hb=hb: (b, 0, jnp.maximum(i * hb - 1, 0), h))
        specs += [cur, cur, cur, halo, halo]
        args += [qkv_sub[0][g], qkv_sub[1][g], qkv_sub[2][g], qkv_sub[1][g], qkv_sub[2][g]]
    return pl.pallas_call(
        _attn_kernel,
        grid=(bsz, seq // ATT_TT, A_HEADS),
        in_specs=specs,
        out_specs=pl.BlockSpec((None, ATT_TT, HEAD_DIM), lambda b, i, h: (b, i, h)),
        out_shape=jax.ShapeDtypeStruct((bsz, seq, A_GROUP_WIDTH), BF16),
        scratch_shapes=[pltpu.VMEM((N_GROUPS, ATT_TT, HEAD_DIM), F32)] * 2
        + [pltpu.VMEM((N_GROUPS - 1, ATT_TT, HEAD_DIM), F32)] * 2
        + [pltpu.VMEM((dil, ATT_TT // dil + A_BLOCK, HEAD_DIM), BF16) for dil in DILS for _ in range(2)],
        compiler_params=_cparams(("parallel", "parallel", "parallel")),
        name="dilated_attention",
    )(*args)


HALO = 16


DN_PH = 4


def _dn_prep_kernel(q_ref, k_ref, v_ref, qh_ref, kh_ref, vh_ref, wq_ref, wk_ref, wv_ref,
                    beta_ref, gc_ref, gct_ref, qo_ref, ko_ref, vo_ref, a_ref, scr):
    t = pl.program_id(1)
    hh = pl.program_id(2)
    tt = q_ref.shape[0]

    def conv_silu(slot, x_ref, halo_ref, w_ref, hs):
        scr[slot, 0:HALO, :] = jnp.where(t > 0, halo_ref[:, hs].astype(F32), 0.0)
        scr[slot, HALO:, :] = x_ref[:, hs].astype(F32)
        w = w_ref[:, hs]
        y = scr[slot, HALO - 3:HALO - 3 + tt, :] * w[0:1, :]
        for j in range(1, CONV_K):
            y = y + scr[slot, HALO - 3 + j:HALO - 3 + j + tt, :] * w[j:j + 1, :]
        return y * jax.nn.sigmoid(y)

    def l2n(v):
        return v * lax.rsqrt(jnp.sum(v * v, axis=1, keepdims=True) + RMS_EPS)

    for hp in range(DN_PH):
        hs = slice(hp * HEAD_DIM, (hp + 1) * HEAD_DIM)
        qo_ref[:, hs] = (l2n(conv_silu(3 * hp, q_ref, qh_ref, wq_ref, hs)) * (HEAD_DIM ** -0.5)).astype(qo_ref.dtype)
        vo_ref[:, hs] = conv_silu(3 * hp + 1, v_ref, vh_ref, wv_ref, hs).astype(vo_ref.dtype)

    beta = beta_ref[...]
    gc = gc_ref[...]
    g2 = 2 * CHUNK
    ri = lax.broadcasted_iota(jnp.int32, (g2, g2), 0)
    ci = lax.broadcasted_iota(jnp.int32, (g2, g2), 1)
    strict = jnp.logical_and((ri >= CHUNK) == (ci >= CHUNK), ri > ci)
    lane = lax.broadcasted_iota(jnp.int32, (CHUNK, g2), 1)
    for hp in range(DN_PH):
        hs = slice(hp * HEAD_DIM, (hp + 1) * HEAD_DIM)
        h = hh * DN_PH + hp
        k = l2n(conv_silu(3 * hp + 2, k_ref, kh_ref, wk_ref, hs)).astype(BF16)
        ko_ref[:, hs] = k
        bcol = _col_of(beta, h)
        gcol = _col_of(gc, B_HEADS + h)
        grow = gct_ref[pl.ds(h, 1), :]
        for gi in range(tt // g2):
            rs = slice(gi * g2, (gi + 1) * g2)
            kb = k[rs, :]
            kbeta = (kb.astype(F32) * bcol[rs, :]).astype(BF16)
            diff = gcol[rs, :] - grow[:, rs]
            a = _dot_nt(kbeta, kb) * jnp.exp(jnp.where(strict, diff, NEG))
            a_ref[:, hp * (tt // g2) + gi, :] = jnp.where(lane < CHUNK, a[0:CHUNK, :], a[CHUNK:g2, :])


def _dn_prep(qkvb, conv_w, beta, gc, gct, tt):
    bsz, seq, _ = qkvb.shape
    nh = B_HEADS
    hb = tt // HALO
    hw = DN_PH * HEAD_DIM
    ng = nh // DN_PH

    def tok(off):
        return pl.BlockSpec((None, tt, hw), lambda b, t, h: (b, t, off + h))

    def halo(off):
        return pl.BlockSpec((None, HALO, hw), lambda b, t, h: (b, jnp.maximum(t * hb - 1, 0), off + h))

    def cw(off):
        return pl.BlockSpec((CONV_K, hw), lambda b, t, h: (0, off + h))

    full = pl.BlockSpec((None, tt, LANES), lambda b, t, h: (b, t, 0))
    gts = pl.BlockSpec((None, nh, tt), lambda b, t, h: (b, 0, t))
    ospec = pl.BlockSpec((None, tt, hw), lambda b, t, h: (b, t, h))
    slots = DN_PH * tt // (2 * CHUNK)
    aspec = pl.BlockSpec((None, CHUNK, slots, 2 * CHUNK), lambda b, t, h: (b * ng + h, 0, t, 0))
    return pl.pallas_call(
        _dn_prep_kernel,
        grid=(bsz, seq // tt, ng),
        in_specs=[tok(0), tok(ng), tok(2 * ng), halo(0), halo(ng), halo(2 * ng),
                  cw(0), cw(ng), cw(2 * ng), full, full, gts],
        out_specs=[ospec, ospec, ospec, aspec],
        out_shape=[jax.ShapeDtypeStruct((bsz, seq, B_WIDTH), BF16)] * 3
        + [jax.ShapeDtypeStruct((bsz * ng, CHUNK, slots * (seq // tt), 2 * CHUNK), F32)],
        scratch_shapes=[pltpu.VMEM((3 * DN_PH, tt + HALO, HEAD_DIM), F32)],
        compiler_params=_cparams(("parallel", "parallel", "arbitrary")),
        name="deltanet_prep",
    )(qkvb, qkvb, qkvb, qkvb, qkvb, qkvb, conv_w, conv_w, conv_w, beta, gc, gct)


SOLVE_TILES = 128
SUB = 8


def _tri_solve_kernel(a_ref, t_ref, at, tt):
    def load_row(i, c):
        at[i] = a_ref[i].T
        return c

    lax.fori_loop(0, CHUNK, load_row, 0, unroll=4)
    tt[...] = jnp.zeros_like(tt)
    rowid = lax.broadcasted_iota(jnp.int32, (SUB, SOLVE_TILES), 0)
    nk = CHUNK // SUB

    def solve_row(i, c):
        acc = tuple(jnp.zeros((SUB, SOLVE_TILES), F32) for _ in range(2 * nk))
        for mb in range(nk):
            def apply_block(acc, mb=mb):
                new = list(acc)
                for m in range(mb * SUB, (mb + 1) * SUB):
                    a0 = jnp.broadcast_to(at[i, m:m + 1, :], (SUB, SOLVE_TILES))
                    a1 = jnp.broadcast_to(at[i, CHUNK + m:CHUNK + m + 1, :], (SUB, SOLVE_TILES))
                    for k in range(mb + 1):
                        new[k] = new[k] - a0 * tt[m, k * SUB:(k + 1) * SUB, :]
                        new[nk + k] = new[nk + k] - a1 * tt[m, CHUNK + k * SUB:CHUNK + (k + 1) * SUB, :]
                return tuple(new)

            acc = lax.cond(mb * SUB < i, apply_block, lambda a: a, acc)
        for k in range(nk):
            diag = jnp.where(rowid + k * SUB == i, 1.0, 0.0).astype(F32)
            tt[i, k * SUB:(k + 1) * SUB, :] = acc[k] + diag
            tt[i, CHUNK + k * SUB:CHUNK + (k + 1) * SUB, :] = acc[nk + k] + diag
        return c

    lax.fori_loop(0, CHUNK, solve_row, 0)

    def store_row(i, c):
        t_ref[i] = tt[i].T
        return c

    lax.fori_loop(0, CHUNK, store_row, 0, unroll=4)


def _tri_solve(a_p):
    nb, _, slots, _ = a_p.shape
    assert slots == SOLVE_TILES
    spec = pl.BlockSpec((None, CHUNK, SOLVE_TILES, 2 * CHUNK), lambda n: (n, 0, 0, 0))
    return pl.pallas_call(
        _tri_solve_kernel,
        grid=(nb,),
        in_specs=[spec],
        out_specs=spec,
        out_shape=jax.ShapeDtypeStruct(a_p.shape, F32),
        scratch_shapes=[pltpu.VMEM((CHUNK, 2 * CHUNK, SOLVE_TILES), F32)] * 2,
        compiler_params=_cparams(("parallel",)),
        name="deltanet_tri_solve",
    )(a_p)


DN_HP = 4


def _dn_main_kernel(q_ref, k_ref, v_ref, beta_ref, gc_ref, gct_ref, t_ref, z_ref, nw_ref, o_ref, state):
    t = pl.program_id(2)
    hh = pl.program_id(1)
    tt = q_ref.shape[0]
    g2 = 2 * CHUNK

    @pl.when(t == 0)
    def _():
        state[...] = jnp.zeros_like(state)

    ri = lax.broadcasted_iota(jnp.int32, (g2, g2), 0)
    ci = lax.broadcasted_iota(jnp.int32, (g2, g2), 1)
    incl = jnp.logical_and((ri >= CHUNK) == (ci >= CHUNK), ri >= ci)
    lane = lax.broadcasted_iota(jnp.int32, (CHUNK, g2), 1)
    nw = nw_ref[...]
    beta = beta_ref[...]
    gc = gc_ref[...]
    bcols, gcols, grows, states = [], [], [], []
    for hp in range(DN_HP):
        h = hh * DN_HP + hp
        bcols.append(_col_of(beta, h))
        gcols.append(_col_of(gc, B_HEADS + h))
        grows.append(gct_ref[pl.ds(h, 1), :])
        states.append(state[hp])
    for gi in range(tt // g2):
        rs = slice(gi * g2, (gi + 1) * g2)
        pre = []
        for hp in range(DN_HP):
            hs = slice(hp * HEAD_DIM, (hp + 1) * HEAD_DIM)
            qb = q_ref[rs, hs]
            kb = k_ref[rs, hs]
            kf = kb.astype(F32)
            b = bcols[hp][rs, :]
            gcl = gcols[hp][rs, :]
            eg = jnp.exp(gcl)
            kbeta = kf * b
            rhs = jnp.concatenate([v_ref[rs, hs].astype(F32) * b, kbeta * eg], axis=1).astype(BF16)
            tp = t_ref[:, hp * (tt // g2) + gi, :]
            tbd = jnp.concatenate([jnp.where(lane < CHUNK, tp, 0.0), jnp.where(lane >= CHUNK, tp, 0.0)], axis=0)
            uw = _dot(tbd.astype(BF16), rhs).astype(BF16)
            diff = gcl - grows[hp][:, rs]
            attn = (_dot_nt(qb, kb) * jnp.exp(jnp.where(incl, diff, NEG))).astype(BF16)
            auw = _dot(attn, uw)
            qeff = (qb.astype(F32) * eg - auw[:, HEAD_DIM:]).astype(BF16)
            per_chunk = []
            for cc in range(2):
                cs = slice(cc * CHUNK, (cc + 1) * CHUNK)
                glast = gcl[cc * CHUNK + CHUNK - 1:cc * CHUNK + CHUNK, :]
                ktail = (kf[cs, :] * jnp.exp(glast - gcl[cs, :])).astype(BF16)
                kuw = _dot_tn(ktail, uw[cs, :])
                lhs = jnp.concatenate([kuw[:, HEAD_DIM:].astype(BF16), qeff[cs, :]], axis=0)
                per_chunk.append((lhs, kuw[:, :HEAD_DIM], jnp.exp(glast), auw[cs, :HEAD_DIM]))
            pre.append(per_chunk)
        outs = [[] for _ in range(DN_HP)]
        for cc in range(2):
            for hp in range(DN_HP):
                lhs, ku, decay, au = pre[hp][cc]
                s = states[hp]
                xs = _dot(lhs, s.astype(BF16))
                outs[hp].append(xs[HEAD_DIM:, :] + au)
                states[hp] = s * decay + ku - xs[:HEAD_DIM, :]
        for hp in range(DN_HP):
            hs = slice(hp * HEAD_DIM, (hp + 1) * HEAD_DIM)
            o = jnp.concatenate(outs[hp], axis=0)
            o = o * lax.rsqrt(jnp.mean(o * o, axis=1, keepdims=True) + RMS_EPS) * nw
            z = z_ref[rs, hs].astype(F32)
            o_ref[rs, hs] = (o * (z * jax.nn.sigmoid(z))).astype(o_ref.dtype)
    for hp in range(DN_HP):
        state[hp] = states[hp]


def _dn_main(qn, kn, vn, beta, gc, gct, t_p, zg, z_off, norm_w, tt):
    bsz, seq, _ = qn.shape
    nh = B_HEADS
    hw = DN_HP * HEAD_DIM
    tok = pl.BlockSpec((None, tt, hw), lambda b, h, t: (b, t, h))
    full = pl.BlockSpec((None, tt, LANES), lambda b, h, t: (b, t, 0))
    gts = pl.BlockSpec((None, nh, tt), lambda b, h, t: (b, 0, t))
    assert DN_HP == DN_PH
    tspec = pl.BlockSpec((None, CHUNK, DN_HP * tt // (2 * CHUNK), 2 * CHUNK),
                         lambda b, h, t: (b * (nh // DN_HP) + h, 0, t, 0))
    nspec = pl.BlockSpec((1, HEAD_DIM), lambda b, h, t: (0, 0))
    return pl.pallas_call(
        _dn_main_kernel,
        grid=(bsz, nh // DN_HP, seq // tt),
        in_specs=[tok, tok, tok, full, full, gts, tspec,
                  pl.BlockSpec((None, tt, hw), lambda b, h, t: (b, t, z_off // hw + h)), nspec],
        out_specs=tok,
        out_shape=jax.ShapeDtypeStruct((bsz, seq, B_WIDTH), BF16),
        scratch_shapes=[pltpu.VMEM((DN_HP, HEAD_DIM, HEAD_DIM), F32)],
        compiler_params=_cparams(("parallel", "parallel", "arbitrary")),
        name="deltanet_main",
    )(qn, kn, vn, beta, gc, gct, t_p, zg, norm_w)


def _mix_out_kernel(ya_ref, yb_ref, ga_ref, gb_ref, x_ref,
                    wa_ref, wb_ref, wo_ref, g_ref, b_ref, xo_ref, xbo_ref):
    ma = _dot(ya_ref[...], wa_ref[...])
    mb = _dot(yb_ref[...], wb_ref[...])
    merged = jax.nn.sigmoid(ga_ref[...].astype(F32)) * ma + jax.nn.sigmoid(gb_ref[...].astype(F32)) * mb
    r = _dot(merged.astype(BF16), wo_ref[...])
    y = _layer_norm(DN_ALPHA * x_ref[...] + r, g_ref[...], b_ref[...])
    xo_ref[...] = y
    xbo_ref[...] = y.astype(BF16)


def _mix_out(ya, yb, zg, gate_off, x, wa, wb, wo, g, b, tm=512):
    t = x.shape[0]
    d = D_MODEL
    aw = A_GROUP_WIDTH
    gblk = gate_off // d

    def rows(wd, c=0):
        return pl.BlockSpec((tm, wd), lambda i: (i, c))

    def whole(shape):
        return pl.BlockSpec(shape, lambda i: (0, 0))

    return pl.pallas_call(
        _mix_out_kernel,
        grid=(t // tm,),
        in_specs=[rows(aw), rows(d), rows(d, gblk), rows(d, gblk + 1), rows(d),
                                   whole((aw, d)), whole((d, d)), whole((d, d)), whole((1, d)), whole((1, d))],
        out_specs=[rows(d), rows(d)],
        out_shape=[jax.ShapeDtypeStruct((t, d), F32), jax.ShapeDtypeStruct((t, d), BF16)],
        compiler_params=_cparams(("parallel",)),
        name="mix_out",
    )(ya, yb, zg, zg, x, wa, wb, wo, g, b)


FFN_FC = 256


def _ffn_kernel(xb_ref, x_ref, wg_ref, wu_ref, wd_ref, g_ref, b_ref, xo_ref, xbo_ref):
    xb = xb_ref[...]
    acc = jnp.zeros(x_ref.shape, F32)
    for c in range(wg_ref.shape[1] // FFN_FC):
        cs = slice(c * FFN_FC, (c + 1) * FFN_FC)
        gt = _dot(xb, wg_ref[:, cs])
        up = _dot(xb, wu_ref[:, cs])
        hh = (gt * jax.nn.sigmoid(gt) * up).astype(BF16)
        acc = acc + _dot(hh, wd_ref[cs, :])
    y = _layer_norm(DN_ALPHA * x_ref[...] + acc, g_ref[...], b_ref[...])
    xo_ref[...] = y
    xbo_ref[...] = y.astype(BF16)


def _ffn(xb, x, wg, wu, wd, g, b, tm=512):
    t, d = x.shape
    rows = pl.BlockSpec((tm, d), lambda i: (i, 0))
    vec = pl.BlockSpec((1, d), lambda i: (0, 0))
    one = pl.Buffered(1)
    return pl.pallas_call(
        _ffn_kernel,
        grid=(t // tm,),
        in_specs=[rows, rows,
                  pl.BlockSpec(wg.shape, lambda i: (0, 0), pipeline_mode=one),
                  pl.BlockSpec(wu.shape, lambda i: (0, 0), pipeline_mode=one),
                  pl.BlockSpec(wd.shape, lambda i: (0, 0), pipeline_mode=one), vec, vec],
        out_specs=[rows, rows],
        out_shape=[jax.ShapeDtypeStruct((t, d), F32), jax.ShapeDtypeStruct((t, d), BF16)],
        compiler_params=_cparams(("parallel",)),
        name="dense_swiglu",
    )(xb, x, wg, wu, wd, g, b)


def _router_kernel(x_ref, rw_ref, tri_ref, gates_ref, rank_ref, selt_ref, cnt_ref):
    logits = jnp.dot(x_ref[...], rw_ref[...], precision=lax.Precision.HIGHEST, preferred_element_type=F32)
    lane = lax.broadcasted_iota(jnp.int32, logits.shape, 1)
    lanef = lane.astype(F32)
    ninf = -jnp.inf
    lg = jnp.where(lane < N_EXPERTS, logits, ninf)
    m1 = jnp.max(lg, axis=1, keepdims=True)
    i1 = jnp.min(jnp.where(lg == m1, lanef, float(LANES)), axis=1, keepdims=True)
    lg2 = jnp.where(lanef == i1, ninf, lg)
    m2 = jnp.max(lg2, axis=1, keepdims=True)
    i2 = jnp.min(jnp.where(lg2 == m2, lanef, float(LANES)), axis=1, keepdims=True)
    e = jnp.exp(m2 - m1)
    w1 = 1.0 / (1.0 + e)
    w2 = e / (1.0 + e)
    gates = jnp.where(lanef == i1, w1, jnp.where(lanef == i2, w2, 0.0))
    gates_ref[...] = gates
    sel = gates > 0.0
    onef = jnp.where(sel, 1.0, 0.0)
    rank = _dot(tri_ref[...], onef.astype(BF16))
    rank_ref[...] = rank
    selt_ref[...] = jnp.where(sel, rank, -1.0).T[0:N_EXPERTS, :]
    cnt_ref[...] = jnp.broadcast_to(jnp.sum(onef, axis=0, keepdims=True), cnt_ref.shape).astype(jnp.int32)


def _router(x, rw_pad, tm):
    t, d = x.shape
    nt = t // tm
    tri = jnp.tril(jnp.ones((tm, tm), BF16), -1)
    rows = pl.BlockSpec((tm, LANES), lambda i: (i, 0))
    return pl.pallas_call(
        _router_kernel,
        grid=(nt,),
        in_specs=[pl.BlockSpec((tm, d), lambda i: (i, 0)), pl.BlockSpec((d, LANES), lambda i: (0, 0)),
                  pl.BlockSpec((tm, tm), lambda i: (0, 0))],
        out_specs=[rows, rows, pl.BlockSpec((None, N_EXPERTS, tm), lambda i: (i, 0, 0)),
                   pl.BlockSpec((None, 8, LANES), lambda i: (i, 0, 0))],
        out_shape=[jax.ShapeDtypeStruct((t, LANES), F32), jax.ShapeDtypeStruct((t, LANES), F32),
                   jax.ShapeDtypeStruct((nt, N_EXPERTS, tm), F32), jax.ShapeDtypeStruct((nt, 8, LANES), jnp.int32)],
        compiler_params=_cparams(("parallel",)),
        name="moe_router",
    )(x, rw_pad, tri)


MOE_TM = 1024
MOE_CAP = 320
MOE_FC = 512


def _expert_kernel(xb_ref, selt_ref, wg_ref, wu_ref, wd_ref, oc_ref):
    e = pl.program_id(0)
    tm = xb_ref.shape[0]
    sel_row = selt_ref[pl.ds(e, 1), :]
    cidx = lax.broadcasted_iota(jnp.int32, (MOE_CAP, tm), 0).astype(F32)
    onehot = jnp.where(sel_row == cidx, 1.0, 0.0).astype(BF16)
    xc = _dot(onehot, xb_ref[...]).astype(BF16)
    acc = jnp.zeros((MOE_CAP, D_MODEL), F32)
    for c in range(wg_ref.shape[1] // MOE_FC):
        cs = slice(c * MOE_FC, (c + 1) * MOE_FC)
        gt = _dot(xc, wg_ref[:, cs])
        up = _dot(xc, wu_ref[:, cs])
        hh = (gt * jax.nn.sigmoid(gt) * up).astype(BF16)
        acc = acc + _dot(hh, wd_ref[cs, :])
    oc_ref[...] = acc.astype(oc_ref.dtype)


def _experts(xb, selt, wg, wu, wd):
    t, d = xb.shape
    ne, _, dex = wg.shape
    nt = t // MOE_TM
    one = pl.Buffered(1)
    return pl.pallas_call(
        _expert_kernel,
        grid=(ne, nt),
        in_specs=[pl.BlockSpec((MOE_TM, d), lambda e, i: (i, 0)),
                  pl.BlockSpec((None, N_EXPERTS, MOE_TM), lambda e, i: (i, 0, 0)),
                  pl.BlockSpec((None, d, dex), lambda e, i: (e, 0, 0), pipeline_mode=one),
                  pl.BlockSpec((None, d, dex), lambda e, i: (e, 0, 0), pipeline_mode=one),
                  pl.BlockSpec((None, dex, d), lambda e, i: (e, 0, 0), pipeline_mode=one)],
        out_specs=pl.BlockSpec((None, None, MOE_CAP, d), lambda e, i: (e, i, 0, 0)),
        out_shape=jax.ShapeDtypeStruct((ne, nt, MOE_CAP, d), BF16),
        compiler_params=_cparams(("arbitrary", "arbitrary")),
        name="moe_experts",
    )(xb, selt, wg, wu, wd)


def _combine_kernel(x_ref, gates_ref, rank_ref, oc_ref, g_ref, b_ref, xo_ref):
    tm = x_ref.shape[0]
    gates = gates_ref[...]
    rank = rank_ref[...]
    lane = lax.broadcasted_iota(jnp.int32, (tm, MOE_CAP), 1).astype(F32)
    y = jnp.zeros((tm, D_MODEL), F32)
    for e in range(N_EXPERTS):
        gcol = gates[:, e:e + 1]
        scol = jnp.where(gcol > 0.0, rank[:, e:e + 1], -1.0)
        onehot = jnp.where(scol == lane, 1.0, 0.0).astype(BF16)
        y = y + _dot(onehot, oc_ref[e]) * gcol
    xo_ref[...] = _layer_norm(DN_ALPHA * x_ref[...] + y, g_ref[...], b_ref[...])


def _combine(x, gates, rank, oc, g, b):
    t, d = x.shape
    ne = oc.shape[0]
    rows = pl.BlockSpec((MOE_TM, d), lambda i: (i, 0))
    lrows = pl.BlockSpec((MOE_TM, LANES), lambda i: (i, 0))
    vec = pl.BlockSpec((1, d), lambda i: (0, 0))
    return pl.pallas_call(
        _combine_kernel,
        grid=(t // MOE_TM,),
        in_specs=[rows, lrows, lrows, pl.BlockSpec((ne, None, MOE_CAP, d), lambda i: (0, i, 0, 0)), vec, vec],
        out_specs=rows,
        out_shape=jax.ShapeDtypeStruct((t, d), F32),
        compiler_params=_cparams(("parallel",)),
        name="moe_combine",
    )(x, gates, rank, oc, g, b)


def _moe_kernel(xb_ref, x_ref, gates_ref, wg_ref, wu_ref, wd_ref, g_ref, b_ref, xo_ref, acc):
    e = pl.program_id(1)
    f = pl.program_id(2)

    @pl.when(jnp.logical_and(e == 0, f == 0))
    def _():
        acc[...] = jnp.zeros_like(acc)

    xb = xb_ref[...]
    gate = _col_of(gates_ref[...], e)
    gt = _dot(xb, wg_ref[...])
    up = _dot(xb, wu_ref[...])
    hh = (gt * jax.nn.sigmoid(gt) * up * gate).astype(BF16)
    acc[...] += _dot(hh, wd_ref[...])

    @pl.when(jnp.logical_and(e == pl.num_programs(1) - 1, f == pl.num_programs(2) - 1))
    def _():
        xo_ref[...] = _layer_norm(DN_ALPHA * x_ref[...] + acc[...], g_ref[...], b_ref[...])


def _moe(xb, x, gates, wg, wu, wd, g, b, tm=1024, tf=512):
    t, d = x.shape
    ne, _, dex = wg.shape
    rows = pl.BlockSpec((tm, d), lambda i, e, f: (i, 0))
    vec = pl.BlockSpec((1, d), lambda i, e, f: (0, 0))
    return pl.pallas_call(
        _moe_kernel,
        grid=(t // tm, ne, dex // tf),
        in_specs=[rows, rows, pl.BlockSpec((tm, LANES), lambda i, e, f: (i, 0)),
                  pl.BlockSpec((None, d, tf), lambda i, e, f: (e, 0, f)),
                  pl.BlockSpec((None, d, tf), lambda i, e, f: (e, 0, f)),
                  pl.BlockSpec((None, tf, d), lambda i, e, f: (e, f, 0)), vec, vec],
        out_specs=rows,
        out_shape=jax.ShapeDtypeStruct((t, d), F32),
        scratch_shapes=[pltpu.VMEM((tm, d), F32)],
        compiler_params=_cparams(("parallel", "arbitrary", "arbitrary")),
        name="moe_swiglu",
    )(xb, x, gates, wg, wu, wd, g, b)


def _rope_tables(positions):
    half = ROT_DIM // 2
    inv_freq = ROPE_THETA ** (-jnp.arange(0, ROT_DIM, 2, dtype=F32) / ROT_DIM)
    ang = positions.astype(F32)[..., None] * inv_freq
    cos, sin = jnp.cos(ang), jnp.sin(ang)
    shp = cos.shape[:-1]
    c = jnp.concatenate([cos, cos, jnp.ones(shp + (LANES - ROT_DIM,), F32)], axis=-1)
    s1 = jnp.concatenate([-sin, jnp.zeros(shp + (LANES - half,), F32)], axis=-1)
    s2 = jnp.concatenate([jnp.zeros(shp + (half,), F32), sin, jnp.zeros(shp + (LANES - ROT_DIM,), F32)], axis=-1)
    return c, s1, s2


def kernel(x, positions, w_in, conv_w, a_log, dt_bias, dn_norm_w, w_branch_a, w_branch_b, w_out, ln1_g, ln1_b,
           ffn_w_gate, ffn_w_up, ffn_w_down, router_w, moe_w_gate, moe_w_up, moe_w_down, ln2_g, ln2_b):
    bsz, seq, d = x.shape
    t = bsz * seq
    nh = B_HEADS
    qa_w = 3 * A_QKV_WIDTH
    o_qkvb = qa_w
    o_z = o_qkvb + 3 * B_WIDTH
    o_bd = o_z + B_WIDTH
    o_gates = o_bd + 2 * nh

    tabs = [tb.reshape(t, LANES) for tb in _rope_tables(positions)]

    dn_tt = 1024
    xf = x.reshape(t, d)
    xb = xf.astype(BF16)
    for layer in range(DEPTH):
        w = w_in[layer]
        w_qkva = w[:, :qa_w].astype(BF16)
        w_b = jnp.concatenate([w[:, o_qkvb:o_bd], w[:, o_gates:]], axis=1).astype(BF16)
        w_bd = jnp.pad(w[:, o_bd:o_gates], ((0, 0), (0, LANES - 2 * nh))).astype(BF16)
        z_off = 3 * B_WIDTH
        gate_off = z_off + B_WIDTH

        zeros8 = jnp.zeros((nh,), F32)
        arow = jnp.concatenate([zeros8, -jnp.exp(a_log[layer].astype(F32)), jnp.zeros((LANES - 2 * nh,), F32)])[None, :]
        dtrow = jnp.concatenate([zeros8, dt_bias[layer].astype(F32), jnp.zeros((LANES - 2 * nh,), F32)])[None, :]
        subs = _proj_a(xb, w_qkva, tabs, bsz, seq)
        hb, beta, gc = _proj_b(xb, w_b, w_bd, arow, dtrow)
        hb3 = hb.reshape(bsz, seq, hb.shape[1])

        ya = _attention([subs[p * N_GROUPS:(p + 1) * N_GROUPS] for p in range(3)], bsz, seq)

        beta3 = beta.reshape(bsz, seq, LANES)
        gc3 = gc.reshape(bsz, seq, LANES)
        gct = gc3[:, :, nh:2 * nh].transpose(0, 2, 1)
        qn, kn, vn, a_p = _dn_prep(hb3, conv_w[layer].astype(F32), beta3, gc3, gct, dn_tt)
        t_p = _tri_solve(a_p)
        yb = _dn_main(qn, kn, vn, beta3, gc3, gct, t_p, hb3, z_off, dn_norm_w[layer].astype(F32)[None, :], dn_tt)

        xf, xb = _mix_out(ya.reshape(t, A_GROUP_WIDTH), yb.reshape(t, B_WIDTH), hb, gate_off, xf,
                          w_branch_a[layer].astype(BF16), w_branch_b[layer].astype(BF16), w_out[layer].astype(BF16),
                          ln1_g[layer].astype(F32)[None, :], ln1_b[layer].astype(F32)[None, :])

        g2 = ln2_g[layer].astype(F32)[None, :]
        b2 = ln2_b[layer].astype(F32)[None, :]
        if layer % 2 == 0:
            i = layer // 2
            xf, xb = _ffn(xb, xf, ffn_w_gate[i].astype(BF16), ffn_w_up[i].astype(BF16), ffn_w_down[i].astype(BF16), g2, b2)
        else:
            i = layer // 2
            rw = jnp.pad(router_w[i].astype(F32), ((0, 0), (0, LANES - N_EXPERTS)))
            gates, rank, selt, cnt = _router(xf, rw, MOE_TM)
            wg = moe_w_gate[i].astype(BF16)
            wu = moe_w_up[i].astype(BF16)
            wd = moe_w_down[i].astype(BF16)

            def routed(xb, xf, gates, rank, selt):
                return _combine(xf, gates, rank, _experts(xb, selt, wg, wu, wd), g2, b2)

            def dense(xb, xf, gates, rank, selt):
                return _moe(xb, xf, gates, wg, wu, wd, g2, b2)

            xf = lax.cond(jnp.max(cnt) > MOE_CAP, dense, routed, xb, xf, gates, rank, selt)
            xb = xf.astype(BF16)
    return xf.reshape(bsz, seq, d)
```

```python
import functools
import math

import jax
import jax.numpy as jnp
from jax import lax
from jax.experimental import pallas as pl
from jax.experimental.pallas import tpu as pltpu

F32 = jnp.float32
BF16 = jnp.bfloat16

D_MODEL = 1024
DEPTH = 2
A_PAIRS = ((128, 1), (512, 4), (2048, 16))
A_HEADS = 4
HEAD_DIM = 128
A_GROUP_WIDTH = A_HEADS * HEAD_DIM
A_QKV_WIDTH = len(A_PAIRS) * A_GROUP_WIDTH
A_BLOCK = 128
ROPE_THETA = 500000.0
ROT_DIM = HEAD_DIM // 4
B_HEADS = 8
B_WIDTH = B_HEADS * HEAD_DIM
CONV_K = 4
CHUNK = 64
N_EXPERTS = 8
DN_ALPHA = (2 * DEPTH) ** 0.25
LN_EPS = 1e-5
RMS_EPS = 1e-6
NEG = -1e30
LANES = 128
VMEM_LIMIT = 56 * 1024 * 1024


def _cparams(sem):
    return pltpu.CompilerParams(dimension_semantics=sem, vmem_limit_bytes=VMEM_LIMIT)


def _dot(a, b):
    return jnp.dot(a, b, preferred_element_type=F32)


def _dot_nt(a, b):
    return lax.dot_general(a, b, (((1,), (1,)), ((), ())), preferred_element_type=F32)


def _dot_tn(a, b):
    return lax.dot_general(a, b, (((0,), (0,)), ((), ())), preferred_element_type=F32)


def _col_of(x, idx):
    lane = lax.broadcasted_iota(jnp.int32, x.shape, 1)
    return jnp.sum(jnp.where(lane == idx, x, 0.0), axis=1, keepdims=True)


def _layer_norm(v, g, b):
    mu = jnp.mean(v, axis=-1, keepdims=True)
    c = v - mu
    var = jnp.mean(c * c, axis=-1, keepdims=True)
    return c * lax.rsqrt(var + LN_EPS) * g + b


def _rope(t, c, s1, s2):
    return t * c + pltpu.roll(t, LANES - ROT_DIM // 2, 1) * s1 + pltpu.roll(t, ROT_DIM // 2, 1) * s2


PB_FC = 1024
HALO = 16
DN_HG = 4


def _proj_b_kernel(npb, x_ref, w_ref, wbd_ref, arow_ref, dtrow_ref, cw_ref,
                   qo_ref, ko_ref, vo_ref, zg_ref, beta_ref, gc_ref, gct_ref, a_ref, scr, carry):
    tm = x_ref.shape[0]
    first = pl.program_id(0) % npb == 0
    x = x_ref[...]

    @pl.when(pl.program_id(0) == 0)
    def _():
        carry[...] = jnp.zeros_like(carry)

    bd = _dot(x, wbd_ref[...])
    beta = jax.nn.sigmoid(bd)
    beta_ref[...] = beta
    y = bd + dtrow_ref[...]
    softplus = jnp.maximum(y, 0.0) + jnp.log(1.0 + jnp.exp(-jnp.abs(y)))
    g = arow_ref[...] * softplus
    ri = lax.broadcasted_iota(jnp.int32, (tm, tm), 0)
    ci = lax.broadcasted_iota(jnp.int32, (tm, tm), 1)
    tri = jnp.where(jnp.logical_and(ri // CHUNK == ci // CHUNK, ci <= ri), 1.0, 0.0).astype(F32)
    gc = jnp.dot(tri, g, precision=lax.Precision.HIGHEST, preferred_element_type=F32)
    gc_ref[...] = gc
    gct = gc.T
    gct_ref[...] = gct[B_HEADS:2 * B_HEADS, :]

    def conv_silu(col, raw):
        scr[col, 0:HALO, :] = jnp.where(first, 0.0, carry[col])
        scr[col, HALO:, :] = raw
        carry[col] = raw[tm - HALO:, :]
        w = cw_ref[:, col * HEAD_DIM:(col + 1) * HEAD_DIM]
        yy = scr[col, HALO - 3:HALO - 3 + tm, :] * w[0:1, :]
        for j in range(1, CONV_K):
            yy = yy + scr[col, HALO - 3 + j:HALO - 3 + j + tm, :] * w[j:j + 1, :]
        return yy * jax.nn.sigmoid(yy)

    def l2n(v):
        return v * lax.rsqrt(jnp.sum(v * v, axis=1, keepdims=True) + RMS_EPS)

    g2 = 2 * CHUNK
    r2 = lax.broadcasted_iota(jnp.int32, (g2, g2), 0)
    c2 = lax.broadcasted_iota(jnp.int32, (g2, g2), 1)
    strict = jnp.logical_and((r2 >= CHUNK) == (c2 >= CHUNK), r2 > c2)
    lane = lax.broadcasted_iota(jnp.int32, (CHUNK, g2), 1)
    tps = tm // g2

    acc = _dot(x, w_ref[:, 0:PB_FC])
    for h in range(B_HEADS):
        hs = slice(h * HEAD_DIM, (h + 1) * HEAD_DIM)
        qo_ref[:, hs] = (l2n(conv_silu(h, acc[:, hs])) * (HEAD_DIM ** -0.5)).astype(qo_ref.dtype)
    acc = _dot(x, w_ref[:, 2 * PB_FC:3 * PB_FC])
    for h in range(B_HEADS):
        hs = slice(h * HEAD_DIM, (h + 1) * HEAD_DIM)
        vo_ref[:, hs] = conv_silu(2 * B_HEADS + h, acc[:, hs]).astype(vo_ref.dtype)
    acc = _dot(x, w_ref[:, PB_FC:2 * PB_FC])
    for h in range(B_HEADS):
        hs = slice(h * HEAD_DIM, (h + 1) * HEAD_DIM)
        k = l2n(conv_silu(B_HEADS + h, acc[:, hs])).astype(BF16)
        ko_ref[:, hs] = k
        bcol = beta[:, h:h + 1]
        gcol = gc[:, B_HEADS + h:B_HEADS + h + 1]
        grow = gct[B_HEADS + h:B_HEADS + h + 1, :]
        for gi in range(tps):
            rs = slice(gi * g2, (gi + 1) * g2)
            kb = k[rs, :]
            kbeta = (kb.astype(F32) * bcol[rs, :]).astype(BF16)
            diff = gcol[rs, :] - grow[:, rs]
            a = _dot_nt(kbeta, kb) * jnp.exp(jnp.where(strict, diff, NEG))
            a_ref[h // DN_HG, :, (h % DN_HG) * tps + gi, :] = jnp.where(lane < CHUNK, a[0:CHUNK, :], a[CHUNK:g2, :])
    for c in range(3, w_ref.shape[1] // PB_FC):
        zg_ref[:, (c - 3) * PB_FC:(c - 2) * PB_FC] = _dot(x, w_ref[:, c * PB_FC:(c + 1) * PB_FC]).astype(zg_ref.dtype)


def _proj_b(xb, w, w_bd, arow, dtrow, conv_w, bsz, seq, tm):
    t, k = xb.shape
    n = w.shape[1]
    npb = seq // tm
    tps = tm // (2 * CHUNK)
    slots = DN_HG * tps
    one = pl.Buffered(1)
    rows = pl.BlockSpec((tm, B_WIDTH), lambda i: (i, 0))
    lrow = pl.BlockSpec((tm, LANES), lambda i: (i, 0))
    rspec = pl.BlockSpec((1, LANES), lambda i: (0, 0))
    ng = B_HEADS // DN_HG
    return pl.pallas_call(
        functools.partial(_proj_b_kernel, npb),
        grid=(t // tm,),
        in_specs=[pl.BlockSpec((tm, k), lambda i: (i, 0)),
                  pl.BlockSpec(w.shape, lambda i: (0, 0), pipeline_mode=one),
                  pl.BlockSpec(w_bd.shape, lambda i: (0, 0), pipeline_mode=one), rspec, rspec,
                  pl.BlockSpec(conv_w.shape, lambda i: (0, 0))],
        out_specs=[rows, rows, rows, pl.BlockSpec((tm, n - 3 * B_WIDTH), lambda i: (i, 0)), lrow, lrow,
                   pl.BlockSpec((B_HEADS, tm), lambda i: (0, i)),
                   pl.BlockSpec((None, ng, CHUNK, slots, 2 * CHUNK), lambda i: (i // npb, 0, 0, i % npb, 0))],
        out_shape=[jax.ShapeDtypeStruct((t, B_WIDTH), BF16)] * 3
        + [jax.ShapeDtypeStruct((t, n - 3 * B_WIDTH), BF16)]
        + [jax.ShapeDtypeStruct((t, LANES), F32)] * 2
        + [jax.ShapeDtypeStruct((B_HEADS, t), F32),
           jax.ShapeDtypeStruct((bsz, ng, CHUNK, slots * npb, 2 * CHUNK), F32)],
        scratch_shapes=[pltpu.VMEM((3 * B_HEADS, tm + HALO, HEAD_DIM), F32),
                        pltpu.VMEM((3 * B_HEADS, HALO, HEAD_DIM), F32)],
        compiler_params=_cparams(("arbitrary",)),
        name="proj_mixer_b",
    )(xb, w, w_bd, arow, dtrow, conv_w)


N_GROUPS = len(A_PAIRS)
DILS = tuple(d for _, d in A_PAIRS)


def _proj_a_kernel(x_ref, w_ref, c_ref, s1_ref, s2_ref, *refs):
    outs = refs[:3 * N_GROUPS]
    scr = refs[3 * N_GROUPS]
    tm = x_ref.shape[0]
    w = A_GROUP_WIDTH
    scale = 1.0 / math.sqrt(HEAD_DIM)
    x = x_ref[...]
    for jj in range(3 * N_GROUPS):
        p, g = divmod(jj, N_GROUPS)
        dil = DILS[g]
        o_ref = outs[jj]
        acc = _dot(x, w_ref[:, jj * w:(jj + 1) * w])
        for h in range(A_HEADS):
            sl = slice(h * HEAD_DIM, (h + 1) * HEAD_DIM)
            t = acc[:, sl]
            if p < 2:
                t = _rope(t, c_ref[...], s1_ref[...], s2_ref[...])
            if p == 0:
                t = t * scale
            if dil == 1:
                o_ref[:, sl] = t.astype(o_ref.dtype)
            else:
                slot = (jj % 2) * A_HEADS + h
                scr[slot] = t
                for r in range(dil):
                    o_ref[r, :, sl] = scr[slot, pl.ds(r, tm // dil, stride=dil), :].astype(o_ref.dtype)


def _proj_a(xb, w_qkva, tabs, bsz, seq, tm=512):
    t, k = xb.shape
    npb = seq // tm
    w = A_GROUP_WIDTH
    out_specs, out_shape = [], []
    for jj in range(3 * N_GROUPS):
        dil = DILS[jj % N_GROUPS]
        if dil == 1:
            out_specs.append(pl.BlockSpec((None, None, tm, w), lambda i: (i // npb, 0, i % npb, 0)))
        else:
            out_specs.append(pl.BlockSpec((None, dil, tm // dil, w), lambda i: (i // npb, 0, i % npb, 0)))
        out_shape.append(jax.ShapeDtypeStruct((bsz, dil, seq // dil, w), BF16))
    tab = pl.BlockSpec((tm, LANES), lambda i: (i, 0))
    return pl.pallas_call(
        _proj_a_kernel,
        grid=(t // tm,),
        in_specs=[pl.BlockSpec((tm, k), lambda i: (i, 0)),
                  pl.BlockSpec(w_qkva.shape, lambda i: (0, 0), pipeline_mode=pl.Buffered(1)), tab, tab, tab],
        out_specs=out_specs,
        out_shape=out_shape,
        scratch_shapes=[pltpu.VMEM((2 * A_HEADS, tm, HEAD_DIM), F32)],
        compiler_params=_cparams(("parallel",)),
        name="proj_mixer_a",
    )(xb, w_qkva, *tabs)


ATT_TT = 2048
ATT_UNROLL = 8


def _attn_kernel(*refs):
    ins = refs[:5 * N_GROUPS]
    o_ref = refs[5 * N_GROUPS]
    od, ld, on, ln = refs[5 * N_GROUPS + 1:5 * N_GROUPS + 5]
    kvbufs = refs[5 * N_GROUPS + 5:]
    it = pl.program_id(1)
    row = lax.broadcasted_iota(jnp.int32, (A_BLOCK, 2 * A_BLOCK), 0)
    col = lax.broadcasted_iota(jnp.int32, (A_BLOCK, 2 * A_BLOCK), 1)
    band = jnp.logical_and(col >= row, col <= row + A_BLOCK)
    bias_all = jnp.where(band, 0.0, NEG).astype(F32)
    bias_own = jnp.where(jnp.logical_and(band, col >= A_BLOCK), 0.0, NEG).astype(F32)
    nblocks = ATT_TT // A_BLOCK
    for g, dil in enumerate(DILS):
        q_ref, k_ref, v_ref, kh_ref, vh_ref = ins[5 * g:5 * g + 5]
        kbuf, vbuf = kvbufs[2 * g:2 * g + 2]
        nlb = nblocks // dil
        rows = ATT_TT // dil
        kbuf[:, 0:A_BLOCK, :] = kh_ref[...]
        kbuf[:, A_BLOCK:, :] = k_ref[...]
        vbuf[:, 0:A_BLOCK, :] = vh_ref[...]
        vbuf[:, A_BLOCK:, :] = v_ref[...]

        def block(c, carry, g=g, nlb=nlb, rows=rows, q_ref=q_ref, kbuf=kbuf, vbuf=vbuf):
            r = c // nlb
            nb = c % nlb
            off = pl.multiple_of(nb * A_BLOCK, A_BLOCK)
            q = q_ref[r, pl.ds(off, A_BLOCK), :]
            kk = kbuf[r, pl.ds(off, 2 * A_BLOCK), :]
            vv = vbuf[r, pl.ds(off, 2 * A_BLOCK), :]
            has_prev = jnp.logical_or(nb > 0, it > 0)
            s = _dot_nt(q, kk) + jnp.where(has_prev, bias_all, bias_own)
            m = jnp.max(s, axis=1, keepdims=True)
            p = jnp.exp(s - m)
            den = jnp.sum(p, axis=1, keepdims=True)
            o = _dot(p.astype(BF16), vv) / den
            dst = pl.multiple_of(r * rows + off, A_BLOCK)
            od[g, pl.ds(dst, A_BLOCK), :] = o
            ld[g, pl.ds(dst, A_BLOCK), :] = jnp.broadcast_to(m + jnp.log(den), (A_BLOCK, HEAD_DIM))
            return carry

        lax.fori_loop(0, nblocks, block, 0, unroll=ATT_UNROLL)
    for g, dil in enumerate(DILS):
        if dil == 1:
            continue
        rows = ATT_TT // dil
        for r in range(dil):
            on[g - 1, pl.ds(r, rows, stride=dil), :] = od[g, r * rows:(r + 1) * rows, :]
            ln[g - 1, pl.ds(r, rows, stride=dil), :] = ld[g, r * rows:(r + 1) * rows, :]
    step = 256
    for c in range(ATT_TT // step):
        sl = slice(c * step, (c + 1) * step)
        lse = [ld[0, sl, :]] + [ln[g - 1, sl, :] for g in range(1, N_GROUPS)]
        outs = [od[0, sl, :]] + [on[g - 1, sl, :] for g in range(1, N_GROUPS)]
        m = functools.reduce(jnp.maximum, lse)
        es = [jnp.exp(l - m) for l in lse]
        num = functools.reduce(lambda a, b: a + b, [e * o for e, o in zip(es, outs)])
        o_ref[sl, :] = (num / functools.reduce(lambda a, b: a + b, es)).astype(o_ref.dtype)


def _attention(qkv_sub, bsz, seq):
    specs, args = [], []
    for g, dil in enumerate(DILS):
        rows = ATT_TT // dil
        hb = rows // A_BLOCK
        cur = pl.BlockSpec((None, dil, rows, HEAD_DIM), lambda b, i, h: (b, 0, i, h))
        halo = pl.BlockSpec((None, dil, A_BLOCK, HEAD_DIM), lambda b, i, h, hb=hb: (b, 0, jnp.maximum(i * hb - 1, 0), h))
        specs += [cur, cur, cur, halo, halo]
        args += [qkv_sub[0][g], qkv_sub[1][g], qkv_sub[2][g], qkv_sub[1][g], qkv_sub[2][g]]
    return pl.pallas_call(
        _attn_kernel,
        grid=(bsz, seq // ATT_TT, A_HEADS),
        in_specs=specs,
        out_specs=pl.BlockSpec((None, ATT_TT, HEAD_DIM), lambda b, i, h: (b, i, h)),
        out_shape=jax.ShapeDtypeStruct((bsz, seq, A_GROUP_WIDTH), BF16),
        scratch_shapes=[pltpu.VMEM((N_GROUPS, ATT_TT, HEAD_DIM), F32)] * 2
        + [pltpu.VMEM((N_GROUPS - 1, ATT_TT, HEAD_DIM), F32)] * 2
        + [pltpu.VMEM((dil, ATT_TT // dil + A_BLOCK, HEAD_DIM), BF16) for dil in DILS for _ in range(2)],
        compiler_params=_cparams(("parallel", "parallel", "parallel")),
        name="dilated_attention",
    )(*args)


SOLVE_TILES = 128
SUB = 8


def _tri_solve_kernel(a_ref, t_ref, at, tt):
    def load_row(i, c):
        at[i] = a_ref[i].T
        return c

    lax.fori_loop(0, CHUNK, load_row, 0, unroll=4)
    tt[...] = jnp.zeros_like(tt)
    rowid = lax.broadcasted_iota(jnp.int32, (SUB, SOLVE_TILES), 0)
    nk = CHUNK // SUB

    def solve_row(i, c):
        acc = tuple(jnp.zeros((SUB, SOLVE_TILES), F32) for _ in range(2 * nk))
        for mb in range(nk):
            def apply_block(acc, mb=mb):
                new = list(acc)
                for m in range(mb * SUB, (mb + 1) * SUB):
                    a0 = jnp.broadcast_to(at[i, m:m + 1, :], (SUB, SOLVE_TILES))
                    a1 = jnp.broadcast_to(at[i, CHUNK + m:CHUNK + m + 1, :], (SUB, SOLVE_TILES))
                    for k in range(mb + 1):
                        new[k] = new[k] - a0 * tt[m, k * SUB:(k + 1) * SUB, :]
                        new[nk + k] = new[nk + k] - a1 * tt[m, CHUNK + k * SUB:CHUNK + (k + 1) * SUB, :]
                return tuple(new)

            acc = lax.cond(mb * SUB < i, apply_block, lambda a: a, acc)
        for k in range(nk):
            diag = jnp.where(rowid + k * SUB == i, 1.0, 0.0).astype(F32)
            tt[i, k * SUB:(k + 1) * SUB, :] = acc[k] + diag
            tt[i, CHUNK + k * SUB:CHUNK + (k + 1) * SUB, :] = acc[nk + k] + diag
        return c

    lax.fori_loop(0, CHUNK, solve_row, 0)

    def store_row(i, c):
        t_ref[i] = tt[i].T
        return c

    lax.fori_loop(0, CHUNK, store_row, 0, unroll=4)


def _tri_solve(a_p):
    nb, _, slots, _ = a_p.shape
    assert slots == SOLVE_TILES
    spec = pl.BlockSpec((None, CHUNK, SOLVE_TILES, 2 * CHUNK), lambda n: (n, 0, 0, 0))
    return pl.pallas_call(
        _tri_solve_kernel,
        grid=(nb,),
        in_specs=[spec],
        out_specs=spec,
        out_shape=jax.ShapeDtypeStruct(a_p.shape, F32),
        scratch_shapes=[pltpu.VMEM((CHUNK, 2 * CHUNK, SOLVE_TILES), F32)] * 2,
        compiler_params=_cparams(("parallel",)),
        name="deltanet_tri_solve",
    )(a_p)


DN_HP = DN_HG
PROJ_B_TM = 512
PROD_TPS = PROJ_B_TM // (2 * CHUNK)


def _dn_main_kernel(q_ref, k_ref, v_ref, beta_ref, gc_ref, gct_ref, t_ref, z_ref, nw_ref, o_ref, state):
    t = pl.program_id(2)
    hh = pl.program_id(1)
    tt = q_ref.shape[0]
    g2 = 2 * CHUNK

    @pl.when(t == 0)
    def _():
        state[...] = jnp.zeros_like(state)

    ri = lax.broadcasted_iota(jnp.int32, (g2, g2), 0)
    ci = lax.broadcasted_iota(jnp.int32, (g2, g2), 1)
    incl = jnp.logical_and((ri >= CHUNK) == (ci >= CHUNK), ri >= ci)
    lane = lax.broadcasted_iota(jnp.int32, (CHUNK, g2), 1)
    nw = nw_ref[...]
    beta = beta_ref[...]
    gc = gc_ref[...]
    bcols, gcols, grows, states = [], [], [], []
    for hp in range(DN_HP):
        h = hh * DN_HP + hp
        bcols.append(_col_of(beta, h))
        gcols.append(_col_of(gc, B_HEADS + h))
        grows.append(gct_ref[pl.ds(h, 1), :])
        states.append(state[hp])
    for gi in range(tt // g2):
        rs = slice(gi * g2, (gi + 1) * g2)
        pre = []
        for hp in range(DN_HP):
            hs = slice(hp * HEAD_DIM, (hp + 1) * HEAD_DIM)
            qb = q_ref[rs, hs]
            kb = k_ref[rs, hs]
            kf = kb.astype(F32)
            b = bcols[hp][rs, :]
            gcl = gcols[hp][rs, :]
            eg = jnp.exp(gcl)
            kbeta = kf * b
            rhs = jnp.concatenate([v_ref[rs, hs].astype(F32) * b, kbeta * eg], axis=1).astype(BF16)
            tp = t_ref[:, (gi // PROD_TPS) * DN_HP * PROD_TPS + hp * PROD_TPS + gi % PROD_TPS, :]
            tbd = jnp.concatenate([jnp.where(lane < CHUNK, tp, 0.0), jnp.where(lane >= CHUNK, tp, 0.0)], axis=0)
            uw = _dot(tbd.astype(BF16), rhs).astype(BF16)
            diff = gcl - grows[hp][:, rs]
            attn = (_dot_nt(qb, kb) * jnp.exp(jnp.where(incl, diff, NEG))).astype(BF16)
            auw = _dot(attn, uw)
            qeff = (qb.astype(F32) * eg - auw[:, HEAD_DIM:]).astype(BF16)
            per_chunk = []
            for cc in range(2):
                cs = slice(cc * CHUNK, (cc + 1) * CHUNK)
                glast = gcl[cc * CHUNK + CHUNK - 1:cc * CHUNK + CHUNK, :]
                ktail = (kf[cs, :] * jnp.exp(glast - gcl[cs, :])).astype(BF16)
                kuw = _dot_tn(ktail, uw[cs, :])
                lhs = jnp.concatenate([kuw[:, HEAD_DIM:].astype(BF16), qeff[cs, :]], axis=0)
                per_chunk.append((lhs, kuw[:, :HEAD_DIM], jnp.exp(glast), auw[cs, :HEAD_DIM]))
            pre.append(per_chunk)
        outs = [[] for _ in range(DN_HP)]
        for cc in range(2):
            for hp in range(DN_HP):
                lhs, ku, decay, au = pre[hp][cc]
                s = states[hp]
                xs = _dot(lhs, s.astype(BF16))
                outs[hp].append(xs[HEAD_DIM:, :] + au)
                states[hp] = s * decay + ku - xs[:HEAD_DIM, :]
        for hp in range(DN_HP):
            hs = slice(hp * HEAD_DIM, (hp + 1) * HEAD_DIM)
            o = jnp.concatenate(outs[hp], axis=0)
            o = o * lax.rsqrt(jnp.mean(o * o, axis=1, keepdims=True) + RMS_EPS) * nw
            z = z_ref[rs, hs].astype(F32)
            o_ref[rs, hs] = (o * (z * jax.nn.sigmoid(z))).astype(o_ref.dtype)
    for hp in range(DN_HP):
        state[hp] = states[hp]


def _dn_main(qn, kn, vn, beta, gc, gct, t_p, zg, z_off, norm_w, tt):
    bsz, seq, _ = qn.shape
    nh = B_HEADS
    hw = DN_HP * HEAD_DIM
    tok = pl.BlockSpec((None, tt, hw), lambda b, h, t: (b, t, h))
    full = pl.BlockSpec((None, tt, LANES), lambda b, h, t: (b, t, 0))
    nst = seq // tt
    gts = pl.BlockSpec((nh, tt), lambda b, h, t: (0, b * nst + t))
    tspec = pl.BlockSpec((None, CHUNK, DN_HP * tt // (2 * CHUNK), 2 * CHUNK),
                         lambda b, h, t: (b * (nh // DN_HP) + h, 0, t, 0))
    nspec = pl.BlockSpec((1, HEAD_DIM), lambda b, h, t: (0, 0))
    return pl.pallas_call(
        _dn_main_kernel,
        grid=(bsz, nh // DN_HP, seq // tt),
        in_specs=[tok, tok, tok, full, full, gts, tspec,
                  pl.BlockSpec((None, tt, hw), lambda b, h, t: (b, t, z_off // hw + h)), nspec],
        out_specs=tok,
        out_shape=jax.ShapeDtypeStruct((bsz, seq, B_WIDTH), BF16),
        scratch_shapes=[pltpu.VMEM((DN_HP, HEAD_DIM, HEAD_DIM), F32)],
        compiler_params=_cparams(("parallel", "parallel", "arbitrary")),
        name="deltanet_main",
    )(qn, kn, vn, beta, gc, gct, t_p, zg, norm_w)


def _mix_out_kernel(ya_ref, yb_ref, ga_ref, gb_ref, x_ref,
                    wa_ref, wb_ref, wo_ref, g_ref, b_ref, xo_ref, xbo_ref):
    ma = _dot(ya_ref[...], wa_ref[...])
    mb = _dot(yb_ref[...], wb_ref[...])
    merged = jax.nn.sigmoid(ga_ref[...].astype(F32)) * ma + jax.nn.sigmoid(gb_ref[...].astype(F32)) * mb
    r = _dot(merged.astype(BF16), wo_ref[...])
    y = _layer_norm(DN_ALPHA * x_ref[...] + r, g_ref[...], b_ref[...])
    xo_ref[...] = y
    xbo_ref[...] = y.astype(BF16)


def _mix_out(ya, yb, zg, gate_off, x, wa, wb, wo, g, b, tm=512):
    t = x.shape[0]
    d = D_MODEL
    aw = A_GROUP_WIDTH
    gblk = gate_off // d

    def rows(wd, c=0):
        return pl.BlockSpec((tm, wd), lambda i: (i, c))

    def whole(shape):
        return pl.BlockSpec(shape, lambda i: (0, 0))

    return pl.pallas_call(
        _mix_out_kernel,
        grid=(t // tm,),
        in_specs=[rows(aw), rows(d), rows(d, gblk), rows(d, gblk + 1), rows(d),
                                   whole((aw, d)), whole((d, d)), whole((d, d)), whole((1, d)), whole((1, d))],
        out_specs=[rows(d), rows(d)],
        out_shape=[jax.ShapeDtypeStruct((t, d), F32), jax.ShapeDtypeStruct((t, d), BF16)],
        compiler_params=_cparams(("parallel",)),
        name="mix_out",
    )(ya, yb, zg, zg, x, wa, wb, wo, g, b)


FFN_FC = 256


def _ffn_kernel(xb_ref, x_ref, wg_ref, wu_ref, wd_ref, g_ref, b_ref, xo_ref, xbo_ref):
    xb = xb_ref[...]
    acc = jnp.zeros(x_ref.shape, F32)
    for c in range(wg_ref.shape[1] // FFN_FC):
        cs = slice(c * FFN_FC, (c + 1) * FFN_FC)
        gt = _dot(xb, wg_ref[:, cs])
        up = _dot(xb, wu_ref[:, cs])
        hh = (gt * jax.nn.sigmoid(gt) * up).astype(BF16)
        acc = acc + _dot(hh, wd_ref[cs, :])
    y = _layer_norm(DN_ALPHA * x_ref[...] + acc, g_ref[...], b_ref[...])
    xo_ref[...] = y
    xbo_ref[...] = y.astype(BF16)


def _ffn(xb, x, wg, wu, wd, g, b, tm=512):
    t, d = x.shape
    rows = pl.BlockSpec((tm, d), lambda i: (i, 0))
    vec = pl.BlockSpec((1, d), lambda i: (0, 0))
    one = pl.Buffered(1)
    return pl.pallas_call(
        _ffn_kernel,
        grid=(t // tm,),
        in_specs=[rows, rows,
                  pl.BlockSpec(wg.shape, lambda i: (0, 0), pipeline_mode=one),
                  pl.BlockSpec(wu.shape, lambda i: (0, 0), pipeline_mode=one),
                  pl.BlockSpec(wd.shape, lambda i: (0, 0), pipeline_mode=one), vec, vec],
        out_specs=[rows, rows],
        out_shape=[jax.ShapeDtypeStruct((t, d), F32), jax.ShapeDtypeStruct((t, d), BF16)],
        compiler_params=_cparams(("parallel",)),
        name="dense_swiglu",
    )(xb, x, wg, wu, wd, g, b)


def _router_kernel(x_ref, rw_ref, tri_ref, gates_ref, rank_ref, selt_ref, cnt_ref):
    logits = jnp.dot(x_ref[...], rw_ref[...], precision=lax.Precision.HIGHEST, preferred_element_type=F32)
    lane = lax.broadcasted_iota(jnp.int32, logits.shape, 1)
    lanef = lane.astype(F32)
    ninf = -jnp.inf
    lg = jnp.where(lane < N_EXPERTS, logits, ninf)
    m1 = jnp.max(lg, axis=1, keepdims=True)
    i1 = jnp.min(jnp.where(lg == m1, lanef, float(LANES)), axis=1, keepdims=True)
    lg2 = jnp.where(lanef == i1, ninf, lg)
    m2 = jnp.max(lg2, axis=1, keepdims=True)
    i2 = jnp.min(jnp.where(lg2 == m2, lanef, float(LANES)), axis=1, keepdims=True)
    e = jnp.exp(m2 - m1)
    w1 = 1.0 / (1.0 + e)
    w2 = e / (1.0 + e)
    gates = jnp.where(lanef == i1, w1, jnp.where(lanef == i2, w2, 0.0))
    gates_ref[...] = gates
    sel = gates > 0.0
    onef = jnp.where(sel, 1.0, 0.0)
    rank = _dot(tri_ref[...], onef.astype(BF16))
    rank_ref[...] = rank
    selt_ref[...] = jnp.where(sel, rank, -1.0).T[0:N_EXPERTS, :]
    cnt_ref[...] = jnp.broadcast_to(jnp.sum(onef, axis=0, keepdims=True), cnt_ref.shape).astype(jnp.int32)


def _router(x, rw_pad, tm):
    t, d = x.shape
    nt = t // tm
    tri = jnp.tril(jnp.ones((tm, tm), BF16), -1)
    rows = pl.BlockSpec((tm, LANES), lambda i: (i, 0))
    return pl.pallas_call(
        _router_kernel,
        grid=(nt,),
        in_specs=[pl.BlockSpec((tm, d), lambda i: (i, 0)), pl.BlockSpec((d, LANES), lambda i: (0, 0)),
                  pl.BlockSpec((tm, tm), lambda i: (0, 0))],
        out_specs=[rows, rows, pl.BlockSpec((None, N_EXPERTS, tm), lambda i: (i, 0, 0)),
                   pl.BlockSpec((None, 8, LANES), lambda i: (i, 0, 0))],
        out_shape=[jax.ShapeDtypeStruct((t, LANES), F32), jax.ShapeDtypeStruct((t, LANES), F32),
                   jax.ShapeDtypeStruct((nt, N_EXPERTS, tm), F32), jax.ShapeDtypeStruct((nt, 8, LANES), jnp.int32)],
        compiler_params=_cparams(("parallel",)),
        name="moe_router",
    )(x, rw_pad, tri)


MOE_TM = 1024
MOE_CAP = 320
MOE_FC = 512


def _expert_kernel(xb_ref, selt_ref, wg_ref, wu_ref, wd_ref, oc_ref):
    e = pl.program_id(0)
    tm = xb_ref.shape[0]
    sel_row = selt_ref[pl.ds(e, 1), :]
    cidx = lax.broadcasted_iota(jnp.int32, (MOE_CAP, tm), 0).astype(F32)
    onehot = jnp.where(sel_row == cidx, 1.0, 0.0).astype(BF16)
    xc = _dot(onehot, xb_ref[...]).astype(BF16)
    acc = jnp.zeros((MOE_CAP, D_MODEL), F32)
    for c in range(wg_ref.shape[1] // MOE_FC):
        cs = slice(c * MOE_FC, (c + 1) * MOE_FC)
        gt = _dot(xc, wg_ref[:, cs])
        up = _dot(xc, wu_ref[:, cs])
        hh = (gt * jax.nn.sigmoid(gt) * up).astype(BF16)
        acc = acc + _dot(hh, wd_ref[cs, :])
    oc_ref[...] = acc.astype(oc_ref.dtype)


def _experts(xb, selt, wg, wu, wd):
    t, d = xb.shape
    ne, _, dex = wg.shape
    nt = t // MOE_TM
    one = pl.Buffered(1)
    return pl.pallas_call(
        _expert_kernel,
        grid=(ne, nt),
        in_specs=[pl.BlockSpec((MOE_TM, d), lambda e, i: (i, 0)),
                  pl.BlockSpec((None, N_EXPERTS, MOE_TM), lambda e, i: (i, 0, 0)),
                  pl.BlockSpec((None, d, dex), lambda e, i: (e, 0, 0), pipeline_mode=one),
                  pl.BlockSpec((None, d, dex), lambda e, i: (e, 0, 0), pipeline_mode=one),
                  pl.BlockSpec((None, dex, d), lambda e, i: (e, 0, 0), pipeline_mode=one)],
        out_specs=pl.BlockSpec((None, None, MOE_CAP, d), lambda e, i: (e, i, 0, 0)),
        out_shape=jax.ShapeDtypeStruct((ne, nt, MOE_CAP, d), BF16),
        compiler_params=_cparams(("arbitrary", "arbitrary")),
        name="moe_experts",
    )(xb, selt, wg, wu, wd)


def _combine_kernel(x_ref, gates_ref, rank_ref, oc_ref, g_ref, b_ref, xo_ref):
    tm = x_ref.shape[0]
    gates = gates_ref[...]
    rank = rank_ref[...]
    lane = lax.broadcasted_iota(jnp.int32, (tm, MOE_CAP), 1).astype(F32)
    y = jnp.zeros((tm, D_MODEL), F32)
    for e in range(N_EXPERTS):
        gcol = gates[:, e:e + 1]
        scol = jnp.where(gcol > 0.0, rank[:, e:e + 1], -1.0)
        onehot = jnp.where(scol == lane, 1.0, 0.0).astype(BF16)
        y = y + _dot(onehot, oc_ref[e]) * gcol
    xo_ref[...] = _layer_norm(DN_ALPHA * x_ref[...] + y, g_ref[...], b_ref[...])


def _combine(x, gates, rank, oc, g, b):
    t, d = x.shape
    ne = oc.shape[0]
    rows = pl.BlockSpec((MOE_TM, d), lambda i: (i, 0))
    lrows = pl.BlockSpec((MOE_TM, LANES), lambda i: (i, 0))
    vec = pl.BlockSpec((1, d), lambda i: (0, 0))
    return pl.pallas_call(
        _combine_kernel,
        grid=(t // MOE_TM,),
        in_specs=[rows, lrows, lrows, pl.BlockSpec((ne, None, MOE_CAP, d), lambda i: (0, i, 0, 0)), vec, vec],
        out_specs=rows,
        out_shape=jax.ShapeDtypeStruct((t, d), F32),
        compiler_params=_cparams(("parallel",)),
        name="moe_combine",
    )(x, gates, rank, oc, g, b)


def _moe_kernel(xb_ref, x_ref, gates_ref, wg_ref, wu_ref, wd_ref, g_ref, b_ref, xo_ref, acc):
    e = pl.program_id(1)
    f = pl.program_id(2)

    @pl.when(jnp.logical_and(e == 0, f == 0))
    def _():
        acc[...] = jnp.zeros_like(acc)

    xb = xb_ref[...]
    gate = _col_of(gates_ref[...], e)
    gt = _dot(xb, wg_ref[...])
    up = _dot(xb, wu_ref[...])
    hh = (gt * jax.nn.sigmoid(gt) * up * gate).astype(BF16)
    acc[...] += _dot(hh, wd_ref[...])

    @pl.when(jnp.logical_and(e == pl.num_programs(1) - 1, f == pl.num_programs(2) - 1))
    def _():
        xo_ref[...] = _layer_norm(DN_ALPHA * x_ref[...] + acc[...], g_ref[...], b_ref[...])


def _moe(xb, x, gates, wg, wu, wd, g, b, tm=1024, tf=512):
    t, d = x.shape
    ne, _, dex = wg.shape
    rows = pl.BlockSpec((tm, d), lambda i, e, f: (i, 0))
    vec = pl.BlockSpec((1, d), lambda i, e, f: (0, 0))
    return pl.pallas_call(
        _moe_kernel,
        grid=(t // tm, ne, dex // tf),
        in_specs=[rows, rows, pl.BlockSpec((tm, LANES), lambda i, e, f: (i, 0)),
                  pl.BlockSpec((None, d, tf), lambda i, e, f: (e, 0, f)),
                  pl.BlockSpec((None, d, tf), lambda i, e, f: (e, 0, f)),
                  pl.BlockSpec((None, tf, d), lambda i, e, f: (e, f, 0)), vec, vec],
        out_specs=rows,
        out_shape=jax.ShapeDtypeStruct((t, d), F32),
        scratch_shapes=[pltpu.VMEM((tm, d), F32)],
        compiler_params=_cparams(("parallel", "arbitrary", "arbitrary")),
        name="moe_swiglu",
    )(xb, x, gates, wg, wu, wd, g, b)


def _rope_tables(positions):
    half = ROT_DIM // 2
    inv_freq = ROPE_THETA ** (-jnp.arange(0, ROT_DIM, 2, dtype=F32) / ROT_DIM)
    ang = positions.astype(F32)[..., None] * inv_freq
    cos, sin = jnp.cos(ang), jnp.sin(ang)
    shp = cos.shape[:-1]
    c = jnp.concatenate([cos, cos, jnp.ones(shp + (LANES - ROT_DIM,), F32)], axis=-1)
    s1 = jnp.concatenate([-sin, jnp.zeros(shp + (LANES - half,), F32)], axis=-1)
    s2 = jnp.concatenate([jnp.zeros(shp + (half,), F32), sin, jnp.zeros(shp + (LANES - ROT_DIM,), F32)], axis=-1)
    return c, s1, s2


def kernel(x, positions, w_in, conv_w, a_log, dt_bias, dn_norm_w, w_branch_a, w_branch_b, w_out, ln1_g, ln1_b,
           ffn_w_gate, ffn_w_up, ffn_w_down, router_w, moe_w_gate, moe_w_up, moe_w_down, ln2_g, ln2_b):
    bsz, seq, d = x.shape
    t = bsz * seq
    nh = B_HEADS
    qa_w = 3 * A_QKV_WIDTH
    o_qkvb = qa_w
    o_z = o_qkvb + 3 * B_WIDTH
    o_bd = o_z + B_WIDTH
    o_gates = o_bd + 2 * nh

    tabs = [tb.reshape(t, LANES) for tb in _rope_tables(positions)]

    dn_tt = 1024
    xf = x.reshape(t, d)
    xb = xf.astype(BF16)
    for layer in range(DEPTH):
        w = w_in[layer]
        w_qkva = w[:, :qa_w].astype(BF16)
        w_b = jnp.concatenate([w[:, o_qkvb:o_bd], w[:, o_gates:]], axis=1).astype(BF16)
        w_bd = jnp.pad(w[:, o_bd:o_gates], ((0, 0), (0, LANES - 2 * nh))).astype(BF16)

        zeros8 = jnp.zeros((nh,), F32)
        arow = jnp.concatenate([zeros8, -jnp.exp(a_log[layer].astype(F32)), jnp.zeros((LANES - 2 * nh,), F32)])[None, :]
        dtrow = jnp.concatenate([zeros8, dt_bias[layer].astype(F32), jnp.zeros((LANES - 2 * nh,), F32)])[None, :]
        subs = _proj_a(xb, w_qkva, tabs, bsz, seq)
        qn, kn, vn, zg, beta, gc, gct, a_p = _proj_b(xb, w_b, w_bd, arow, dtrow, conv_w[layer].astype(F32),
                                                      bsz, seq, PROJ_B_TM)

        ya = _attention([subs[p * N_GROUPS:(p + 1) * N_GROUPS] for p in range(3)], bsz, seq)

        sh3 = (bsz, seq, -1)
        t_p = _tri_solve(a_p.reshape((-1,) + a_p.shape[2:]))
        yb = _dn_main(qn.reshape(sh3), kn.reshape(sh3), vn.reshape(sh3), beta.reshape(sh3), gc.reshape(sh3), gct,
                      t_p, zg.reshape(sh3), 0, dn_norm_w[layer].astype(F32)[None, :], dn_tt)

        xf, xb = _mix_out(ya.reshape(t, A_GROUP_WIDTH), yb.reshape(t, B_WIDTH), zg, B_WIDTH, xf,
                          w_branch_a[layer].astype(BF16), w_branch_b[layer].astype(BF16), w_out[layer].astype(BF16),
                          ln1_g[layer].astype(F32)[None, :], ln1_b[layer].astype(F32)[None, :])

        g2 = ln2_g[layer].astype(F32)[None, :]
        b2 = ln2_b[layer].astype(F32)[None, :]
        if layer % 2 == 0:
            i = layer // 2
            xf, xb = _ffn(xb, xf, ffn_w_gate[i].astype(BF16), ffn_w_up[i].astype(BF16), ffn_w_down[i].astype(BF16), g2, b2)
        else:
            i = layer // 2
            rw = jnp.pad(router_w[i].astype(F32), ((0, 0), (0, LANES - N_EXPERTS)))
            gates, rank, selt, cnt = _router(xf, rw, MOE_TM)
            wg = moe_w_gate[i].astype(BF16)
            wu = moe_w_up[i].astype(BF16)
            wd = moe_w_down[i].astype(BF16)

            def routed(xb, xf, gates, rank, selt):
                return _combine(xf, gates, rank, _experts(xb, selt, wg, wu, wd), g2, b2)

            def dense(xb, xf, gates, rank, selt):
                return _moe(xb, xf, gates, wg, wu, wd, g2, b2)

            xf = lax.cond(jnp.max(cnt) > MOE_CAP, dense, routed, xb, xf, gates, rank, selt)
            xb = xf.astype(BF16)
    return xf.reshape(bsz, seq, d)
```

```python
import functools
import math

import jax
import jax.numpy as jnp
from jax import lax
from jax.experimental import pallas as pl
from jax.experimental.pallas import tpu as pltpu

F32 = jnp.float32
BF16 = jnp.bfloat16

D_MODEL = 1024
DEPTH = 2
A_PAIRS = ((128, 1), (512, 4), (2048, 16))
A_HEADS = 4
HEAD_DIM = 128
A_GROUP_WIDTH = A_HEADS * HEAD_DIM
A_QKV_WIDTH = len(A_PAIRS) * A_GROUP_WIDTH
A_BLOCK = 128
ROPE_THETA = 500000.0
ROT_DIM = HEAD_DIM // 4
B_HEADS = 8
B_WIDTH = B_HEADS * HEAD_DIM
CONV_K = 4
CHUNK = 64
N_EXPERTS = 8
DN_ALPHA = (2 * DEPTH) ** 0.25
LN_EPS = 1e-5
RMS_EPS = 1e-6
NEG = -1e30
LANES = 128
VMEM_LIMIT = 56 * 1024 * 1024


def _cparams(sem):
    return pltpu.CompilerParams(dimension_semantics=sem, vmem_limit_bytes=VMEM_LIMIT)


def _dot(a, b):
    return jnp.dot(a, b, preferred_element_type=F32)


def _dot_nt(a, b):
    return lax.dot_general(a, b, (((1,), (1,)), ((), ())), preferred_element_type=F32)


def _dot_tn(a, b):
    return lax.dot_general(a, b, (((0,), (0,)), ((), ())), preferred_element_type=F32)


def _col_of(x, idx):
    lane = lax.broadcasted_iota(jnp.int32, x.shape, 1)
    return jnp.sum(jnp.where(lane == idx, x, 0.0), axis=1, keepdims=True)


def _layer_norm(v, g, b):
    mu = jnp.mean(v, axis=-1, keepdims=True)
    c = v - mu
    var = jnp.mean(c * c, axis=-1, keepdims=True)
    return c * lax.rsqrt(var + LN_EPS) * g + b


def _rope(t, c, s1, s2):
    return t * c + pltpu.roll(t, LANES - ROT_DIM // 2, 1) * s1 + pltpu.roll(t, ROT_DIM // 2, 1) * s2


PB_FC = 1024
HALO = 16
DN_HG = 4


def _proj_b_kernel(npb, x_ref, w_ref, wbd_ref, arow_ref, dtrow_ref, cw_ref,
                   qo_ref, ko_ref, vo_ref, zg_ref, beta_ref, gc_ref, gct_ref, a_ref, scr, carry):
    tm = x_ref.shape[0]
    first = pl.program_id(0) % npb == 0
    x = x_ref[...]

    @pl.when(pl.program_id(0) == 0)
    def _():
        carry[...] = jnp.zeros_like(carry)

    bd = _dot(x, wbd_ref[...])
    beta = jax.nn.sigmoid(bd)
    beta_ref[...] = beta
    y = bd + dtrow_ref[...]
    softplus = jnp.maximum(y, 0.0) + jnp.log(1.0 + jnp.exp(-jnp.abs(y)))
    g = arow_ref[...] * softplus
    ri = lax.broadcasted_iota(jnp.int32, (tm, tm), 0)
    ci = lax.broadcasted_iota(jnp.int32, (tm, tm), 1)
    tri = jnp.where(jnp.logical_and(ri // CHUNK == ci // CHUNK, ci <= ri), 1.0, 0.0).astype(F32)
    gc = jnp.dot(tri, g, precision=lax.Precision.HIGHEST, preferred_element_type=F32)
    gc_ref[...] = gc
    gct = gc.T
    gct_ref[...] = gct[B_HEADS:2 * B_HEADS, :]

    def conv_silu(col, raw):
        scr[col, 0:HALO, :] = jnp.where(first, 0.0, carry[col])
        scr[col, HALO:, :] = raw
        carry[col] = raw[tm - HALO:, :]
        w = cw_ref[:, col * HEAD_DIM:(col + 1) * HEAD_DIM]
        yy = scr[col, HALO - 3:HALO - 3 + tm, :] * w[0:1, :]
        for j in range(1, CONV_K):
            yy = yy + scr[col, HALO - 3 + j:HALO - 3 + j + tm, :] * w[j:j + 1, :]
        return yy * jax.nn.sigmoid(yy)

    def l2n(v):
        return v * lax.rsqrt(jnp.sum(v * v, axis=1, keepdims=True) + RMS_EPS)

    g2 = 2 * CHUNK
    r2 = lax.broadcasted_iota(jnp.int32, (g2, g2), 0)
    c2 = lax.broadcasted_iota(jnp.int32, (g2, g2), 1)
    strict = jnp.logical_and((r2 >= CHUNK) == (c2 >= CHUNK), r2 > c2)
    lane = lax.broadcasted_iota(jnp.int32, (CHUNK, g2), 1)
    tps = tm // g2

    def plain_chunk(c):
        zg_ref[:, (c - 3) * PB_FC:(c - 2) * PB_FC] = _dot(x, w_ref[:, c * PB_FC:(c + 1) * PB_FC]).astype(zg_ref.dtype)

    acc = _dot(x, w_ref[:, 0:PB_FC])
    plain_chunk(3)
    for h in range(B_HEADS):
        hs = slice(h * HEAD_DIM, (h + 1) * HEAD_DIM)
        qo_ref[:, hs] = (l2n(conv_silu(h, acc[:, hs])) * (HEAD_DIM ** -0.5)).astype(qo_ref.dtype)
    acc = _dot(x, w_ref[:, 2 * PB_FC:3 * PB_FC])
    plain_chunk(4)
    for h in range(B_HEADS):
        hs = slice(h * HEAD_DIM, (h + 1) * HEAD_DIM)
        vo_ref[:, hs] = conv_silu(2 * B_HEADS + h, acc[:, hs]).astype(vo_ref.dtype)
    acc = _dot(x, w_ref[:, PB_FC:2 * PB_FC])
    plain_chunk(5)
    for h in range(B_HEADS):
        hs = slice(h * HEAD_DIM, (h + 1) * HEAD_DIM)
        k = l2n(conv_silu(B_HEADS + h, acc[:, hs])).astype(BF16)
        ko_ref[:, hs] = k
        bcol = beta[:, h:h + 1]
        gcol = gc[:, B_HEADS + h:B_HEADS + h + 1]
        grow = gct[B_HEADS + h:B_HEADS + h + 1, :]
        for gi in range(tps):
            rs = slice(gi * g2, (gi + 1) * g2)
            kb = k[rs, :]
            kbeta = (kb.astype(F32) * bcol[rs, :]).astype(BF16)
            diff = gcol[rs, :] - grow[:, rs]
            a = _dot_nt(kbeta, kb) * jnp.exp(jnp.where(strict, diff, NEG))
            a_ref[h // DN_HG, :, (h % DN_HG) * tps + gi, :] = jnp.where(lane < CHUNK, a[0:CHUNK, :], a[CHUNK:g2, :])


def _proj_b(xb, w, w_bd, arow, dtrow, conv_w, bsz, seq, tm):
    t, k = xb.shape
    n = w.shape[1]
    npb = seq // tm
    tps = tm // (2 * CHUNK)
    slots = DN_HG * tps
    one = pl.Buffered(1)
    rows = pl.BlockSpec((tm, B_WIDTH), lambda i: (i, 0))
    lrow = pl.BlockSpec((tm, LANES), lambda i: (i, 0))
    rspec = pl.BlockSpec((1, LANES), lambda i: (0, 0))
    ng = B_HEADS // DN_HG
    return pl.pallas_call(
        functools.partial(_proj_b_kernel, npb),
        grid=(t // tm,),
        in_specs=[pl.BlockSpec((tm, k), lambda i: (i, 0)),
                  pl.BlockSpec(w.shape, lambda i: (0, 0), pipeline_mode=one),
                  pl.BlockSpec(w_bd.shape, lambda i: (0, 0), pipeline_mode=one), rspec, rspec,
                  pl.BlockSpec(conv_w.shape, lambda i: (0, 0))],
        out_specs=[rows, rows, rows, pl.BlockSpec((tm, n - 3 * B_WIDTH), lambda i: (i, 0)), lrow, lrow,
                   pl.BlockSpec((B_HEADS, tm), lambda i: (0, i)),
                   pl.BlockSpec((None, ng, CHUNK, slots, 2 * CHUNK), lambda i: (i // npb, 0, 0, i % npb, 0))],
        out_shape=[jax.ShapeDtypeStruct((t, B_WIDTH), BF16)] * 3
        + [jax.ShapeDtypeStruct((t, n - 3 * B_WIDTH), BF16)]
        + [jax.ShapeDtypeStruct((t, LANES), F32)] * 2
        + [jax.ShapeDtypeStruct((B_HEADS, t), F32),
           jax.ShapeDtypeStruct((bsz, ng, CHUNK, slots * npb, 2 * CHUNK), F32)],
        scratch_shapes=[pltpu.VMEM((3 * B_HEADS, tm + HALO, HEAD_DIM), F32),
                        pltpu.VMEM((3 * B_HEADS, HALO, HEAD_DIM), F32)],
        compiler_params=_cparams(("arbitrary",)),
        name="proj_mixer_b",
    )(xb, w, w_bd, arow, dtrow, conv_w)


N_GROUPS = len(A_PAIRS)
DILS = tuple(d for _, d in A_PAIRS)


def _proj_a_kernel(x_ref, w_ref, c_ref, s1_ref, s2_ref, *refs):
    outs = refs[:3 * N_GROUPS]
    scr = refs[3 * N_GROUPS]
    tm = x_ref.shape[0]
    w = A_GROUP_WIDTH
    scale = 1.0 / math.sqrt(HEAD_DIM)
    x = x_ref[...]
    for jj in range(3 * N_GROUPS):
        p, g = divmod(jj, N_GROUPS)
        dil = DILS[g]
        o_ref = outs[jj]
        acc = _dot(x, w_ref[:, jj * w:(jj + 1) * w])
        for h in range(A_HEADS):
            sl = slice(h * HEAD_DIM, (h + 1) * HEAD_DIM)
            t = acc[:, sl]
            if p < 2:
                t = _rope(t, c_ref[...], s1_ref[...], s2_ref[...])
            if p == 0:
                t = t * scale
            if dil == 1:
                o_ref[:, sl] = t.astype(o_ref.dtype)
            else:
                slot = (jj % 2) * A_HEADS + h
                scr[slot] = t
                for r in range(dil):
                    o_ref[r, :, sl] = scr[slot, pl.ds(r, tm // dil, stride=dil), :].astype(o_ref.dtype)


def _proj_a(xb, w_qkva, tabs, bsz, seq, tm=512):
    t, k = xb.shape
    npb = seq // tm
    w = A_GROUP_WIDTH
    out_specs, out_shape = [], []
    for jj in range(3 * N_GROUPS):
        dil = DILS[jj % N_GROUPS]
        if dil == 1:
            out_specs.append(pl.BlockSpec((None, None, tm, w), lambda i: (i // npb, 0, i % npb, 0)))
        else:
            out_specs.append(pl.BlockSpec((None, dil, tm // dil, w), lambda i: (i // npb, 0, i % npb, 0)))
        out_shape.append(jax.ShapeDtypeStruct((bsz, dil, seq // dil, w), BF16))
    tab = pl.BlockSpec((tm, LANES), lambda i: (i, 0))
    return pl.pallas_call(
        _proj_a_kernel,
        grid=(t // tm,),
        in_specs=[pl.BlockSpec((tm, k), lambda i: (i, 0)),
                  pl.BlockSpec(w_qkva.shape, lambda i: (0, 0), pipeline_mode=pl.Buffered(1)), tab, tab, tab],
        out_specs=out_specs,
        out_shape=out_shape,
        scratch_shapes=[pltpu.VMEM((2 * A_HEADS, tm, HEAD_DIM), F32)],
        compiler_params=_cparams(("parallel",)),
        name="proj_mixer_a",
    )(xb, w_qkva, *tabs)


ATT_TT = 2048
ATT_UNROLL = 16


def _attn_kernel(*refs):
    ins = refs[:5 * N_GROUPS]
    o_ref = refs[5 * N_GROUPS]
    od, ld, on, ln = refs[5 * N_GROUPS + 1:5 * N_GROUPS + 5]
    kvbufs = refs[5 * N_GROUPS + 5:]
    it = pl.program_id(1)
    row = lax.broadcasted_iota(jnp.int32, (A_BLOCK, 2 * A_BLOCK), 0)
    col = lax.broadcasted_iota(jnp.int32, (A_BLOCK, 2 * A_BLOCK), 1)
    band = jnp.logical_and(col >= row, col <= row + A_BLOCK)
    bias_all = jnp.where(band, 0.0, NEG).astype(F32)
    bias_own = jnp.where(jnp.logical_and(band, col >= A_BLOCK), 0.0, NEG).astype(F32)
    nblocks = ATT_TT // A_BLOCK
    for g, dil in enumerate(DILS):
        q_ref, k_ref, v_ref, kh_ref, vh_ref = ins[5 * g:5 * g + 5]
        kbuf, vbuf = kvbufs[2 * g:2 * g + 2]
        nlb = nblocks // dil
        rows = ATT_TT // dil
        kbuf[:, 0:A_BLOCK, :] = kh_ref[...]
        kbuf[:, A_BLOCK:, :] = k_ref[...]
        vbuf[:, 0:A_BLOCK, :] = vh_ref[...]
        vbuf[:, A_BLOCK:, :] = v_ref[...]

        def block(c, carry, g=g, nlb=nlb, rows=rows, q_ref=q_ref, kbuf=kbuf, vbuf=vbuf):
            r = c // nlb
            nb = c % nlb
            off = pl.multiple_of(nb * A_BLOCK, A_BLOCK)
            q = q_ref[r, pl.ds(off, A_BLOCK), :]
            kk = kbuf[r, pl.ds(off, 2 * A_BLOCK), :]
            vv = vbuf[r, pl.ds(off, 2 * A_BLOCK), :]
            has_prev = jnp.logical_or(nb > 0, it > 0)
            s = _dot_nt(q, kk) + jnp.where(has_prev, bias_all, bias_own)
            m = jnp.max(s, axis=1, keepdims=True)
            p = jnp.exp(s - m)
            den = jnp.sum(p, axis=1, keepdims=True)
            o = _dot(p.astype(BF16), vv) / den
            dst = pl.multiple_of(r * rows + off, A_BLOCK)
            od[g, pl.ds(dst, A_BLOCK), :] = o
            ld[g, pl.ds(dst, A_BLOCK), :] = jnp.broadcast_to(m + jnp.log(den), (A_BLOCK, HEAD_DIM))
            return carry

        lax.fori_loop(0, nblocks, block, 0, unroll=ATT_UNROLL)
    for g, dil in enumerate(DILS):
        if dil == 1:
            continue
        rows = ATT_TT // dil
        for r in range(dil):
            on[g - 1, pl.ds(r, rows, stride=dil), :] = od[g, r * rows:(r + 1) * rows, :]
            ln[g - 1, pl.ds(r, rows, stride=dil), :] = ld[g, r * rows:(r + 1) * rows, :]
    step = 256
    for c in range(ATT_TT // step):
        sl = slice(c * step, (c + 1) * step)
        lse = [ld[0, sl, :]] + [ln[g - 1, sl, :] for g in range(1, N_GROUPS)]
        outs = [od[0, sl, :]] + [on[g - 1, sl, :] for g in range(1, N_GROUPS)]
        m = functools.reduce(jnp.maximum, lse)
        es = [jnp.exp(l - m) for l in lse]
        num = functools.reduce(lambda a, b: a + b, [e * o for e, o in zip(es, outs)])
        o_ref[sl, :] = (num / functools.reduce(lambda a, b: a + b, es)).astype(o_ref.dtype)


def _attention(qkv_sub, bsz, seq):
    specs, args = [], []
    for g, dil in enumerate(DILS):
        rows = ATT_TT // dil
        hb = rows // A_BLOCK
        cur = pl.BlockSpec((None, dil, rows, HEAD_DIM), lambda b, i, h: (b, 0, i, h))
        halo = pl.BlockSpec((None, dil, A_BLOCK, HEAD_DIM), lambda b, i, h, hb=hb: (b, 0, jnp.maximum(i * hb - 1, 0), h))
        specs += [cur, cur, cur, halo, halo]
        args += [qkv_sub[0][g], qkv_sub[1][g], qkv_sub[2][g], qkv_sub[1][g], qkv_sub[2][g]]
    return pl.pallas_call(
        _attn_kernel,
        grid=(bsz, seq // ATT_TT, A_HEADS),
        in_specs=specs,
        out_specs=pl.BlockSpec((None, ATT_TT, HEAD_DIM), lambda b, i, h: (b, i, h)),
        out_shape=jax.ShapeDtypeStruct((bsz, seq, A_GROUP_WIDTH), BF16),
        scratch_shapes=[pltpu.VMEM((N_GROUPS, ATT_TT, HEAD_DIM), F32)] * 2
        + [pltpu.VMEM((N_GROUPS - 1, ATT_TT, HEAD_DIM), F32)] * 2
        + [pltpu.VMEM((dil, ATT_TT // dil + A_BLOCK, HEAD_DIM), BF16) for dil in DILS for _ in range(2)],
        compiler_params=_cparams(("parallel", "parallel", "parallel")),
        name="dilated_attention",
    )(*args)


SOLVE_TILES = 128
SUB = 8


def _tri_solve_kernel(a_ref, t_ref, at, tt):
    def load_row(i, c):
        at[i] = a_ref[i].T
        return c

    lax.fori_loop(0, CHUNK, load_row, 0, unroll=4)
    tt[...] = jnp.zeros_like(tt)
    rowid = lax.broadcasted_iota(jnp.int32, (SUB, SOLVE_TILES), 0)
    nk = CHUNK // SUB

    def solve_row(i, c):
        acc = tuple(jnp.zeros((SUB, SOLVE_TILES), F32) for _ in range(2 * nk))
        for mb in range(nk):
            def apply_block(acc, mb=mb):
                new = list(acc)
                for m in range(mb * SUB, (mb + 1) * SUB):
                    a0 = jnp.broadcast_to(at[i, m:m + 1, :], (SUB, SOLVE_TILES))
                    a1 = jnp.broadcast_to(at[i, CHUNK + m:CHUNK + m + 1, :], (SUB, SOLVE_TILES))
                    for k in range(mb + 1):
                        new[k] = new[k] - a0 * tt[m, k * SUB:(k + 1) * SUB, :]
                        new[nk + k] = new[nk + k] - a1 * tt[m, CHUNK + k * SUB:CHUNK + (k + 1) * SUB, :]
                return tuple(new)

            acc = lax.cond(mb * SUB < i, apply_block, lambda a: a, acc)
        for k in range(nk):
            diag = jnp.where(rowid + k * SUB == i, 1.0, 0.0).astype(F32)
            tt[i, k * SUB:(k + 1) * SUB, :] = acc[k] + diag
            tt[i, CHUNK + k * SUB:CHUNK + (k + 1) * SUB, :] = acc[nk + k] + diag
        return c

    lax.fori_loop(0, CHUNK, solve_row, 0)

    def store_row(i, c):
        t_ref[i] = tt[i].T
        return c

    lax.fori_loop(0, CHUNK, store_row, 0, unroll=4)


def _tri_solve(a_p):
    nb, _, slots, _ = a_p.shape
    assert slots == SOLVE_TILES
    spec = pl.BlockSpec((None, CHUNK, SOLVE_TILES, 2 * CHUNK), lambda n: (n, 0, 0, 0))
    return pl.pallas_call(
        _tri_solve_kernel,
        grid=(nb,),
        in_specs=[spec],
        out_specs=spec,
        out_shape=jax.ShapeDtypeStruct(a_p.shape, F32),
        scratch_shapes=[pltpu.VMEM((CHUNK, 2 * CHUNK, SOLVE_TILES), F32)] * 2,
        compiler_params=_cparams(("parallel",)),
        name="deltanet_tri_solve",
    )(a_p)


DN_HP = DN_HG
PROJ_B_TM = 512
PROD_TPS = PROJ_B_TM // (2 * CHUNK)


def _dn_main_kernel(q_ref, k_ref, v_ref, beta_ref, gc_ref, gct_ref, t_ref, z_ref, nw_ref, o_ref, state):
    t = pl.program_id(2)
    hh = pl.program_id(1)
    tt = q_ref.shape[0]
    g2 = 2 * CHUNK

    @pl.when(t == 0)
    def _():
        state[...] = jnp.zeros_like(state)

    ri = lax.broadcasted_iota(jnp.int32, (g2, g2), 0)
    ci = lax.broadcasted_iota(jnp.int32, (g2, g2), 1)
    incl = jnp.logical_and((ri >= CHUNK) == (ci >= CHUNK), ri >= ci)
    lane = lax.broadcasted_iota(jnp.int32, (CHUNK, g2), 1)
    nw = nw_ref[...]
    beta = beta_ref[...]
    gc = gc_ref[...]
    bcols, gcols, grows, states = [], [], [], []
    for hp in range(DN_HP):
        h = hh * DN_HP + hp
        bcols.append(_col_of(beta, h))
        gcols.append(_col_of(gc, B_HEADS + h))
        grows.append(gct_ref[pl.ds(h, 1), :])
        states.append(state[hp])
    for gi in range(tt // g2):
        rs = slice(gi * g2, (gi + 1) * g2)
        pre = []
        for hp in range(DN_HP):
            hs = slice(hp * HEAD_DIM, (hp + 1) * HEAD_DIM)
            qb = q_ref[rs, hs]
            kb = k_ref[rs, hs]
            kf = kb.astype(F32)
            b = bcols[hp][rs, :]
            gcl = gcols[hp][rs, :]
            eg = jnp.exp(gcl)
            kbeta = kf * b
            rhs = jnp.concatenate([v_ref[rs, hs].astype(F32) * b, kbeta * eg], axis=1).astype(BF16)
            tp = t_ref[:, (gi // PROD_TPS) * DN_HP * PROD_TPS + hp * PROD_TPS + gi % PROD_TPS, :]
            tbd = jnp.concatenate([jnp.where(lane < CHUNK, tp, 0.0), jnp.where(lane >= CHUNK, tp, 0.0)], axis=0)
            uw = _dot(tbd.astype(BF16), rhs).astype(BF16)
            diff = gcl - grows[hp][:, rs]
            attn = (_dot_nt(qb, kb) * jnp.exp(jnp.where(incl, diff, NEG))).astype(BF16)
            auw = _dot(attn, uw)
            qeff = (qb.astype(F32) * eg - auw[:, HEAD_DIM:]).astype(BF16)
            per_chunk = []
            for cc in range(2):
                cs = slice(cc * CHUNK, (cc + 1) * CHUNK)
                glast = gcl[cc * CHUNK + CHUNK - 1:cc * CHUNK + CHUNK, :]
                ktail = (kf[cs, :] * jnp.exp(glast - gcl[cs, :])).astype(BF16)
                kuw = _dot_tn(ktail, uw[cs, :])
                lhs = jnp.concatenate([kuw[:, HEAD_DIM:].astype(BF16), qeff[cs, :]], axis=0)
                per_chunk.append((lhs, kuw[:, :HEAD_DIM], jnp.exp(glast), auw[cs, :HEAD_DIM]))
            pre.append(per_chunk)
        outs = [[] for _ in range(DN_HP)]
        for cc in range(2):
            for hp in range(DN_HP):
                lhs, ku, decay, au = pre[hp][cc]
                s = states[hp]
                xs = _dot(lhs, s.astype(BF16))
                outs[hp].append(xs[HEAD_DIM:, :] + au)
                states[hp] = s * decay + ku - xs[:HEAD_DIM, :]
        for hp in range(DN_HP):
            hs = slice(hp * HEAD_DIM, (hp + 1) * HEAD_DIM)
            o = jnp.concatenate(outs[hp], axis=0)
            o = o * lax.rsqrt(jnp.mean(o * o, axis=1, keepdims=True) + RMS_EPS) * nw
            z = z_ref[rs, hs].astype(F32)
            o_ref[rs, hs] = (o * (z * jax.nn.sigmoid(z))).astype(o_ref.dtype)
    for hp in range(DN_HP):
        state[hp] = states[hp]


def _dn_main(qn, kn, vn, beta, gc, gct, t_p, zg, z_off, norm_w, tt):
    bsz, seq, _ = qn.shape
    nh = B_HEADS
    hw = DN_HP * HEAD_DIM
    tok = pl.BlockSpec((None, tt, hw), lambda b, h, t: (b, t, h))
    full = pl.BlockSpec((None, tt, LANES), lambda b, h, t: (b, t, 0))
    nst = seq // tt
    gts = pl.BlockSpec((nh, tt), lambda b, h, t: (0, b * nst + t))
    tspec = pl.BlockSpec((None, CHUNK, DN_HP * tt // (2 * CHUNK), 2 * CHUNK),
                         lambda b, h, t: (b * (nh // DN_HP) + h, 0, t, 0))
    nspec = pl.BlockSpec((1, HEAD_DIM), lambda b, h, t: (0, 0))
    return pl.pallas_call(
        _dn_main_kernel,
        grid=(bsz, nh // DN_HP, seq // tt),
        in_specs=[tok, tok, tok, full, full, gts, tspec,
                  pl.BlockSpec((None, tt, hw), lambda b, h, t: (b, t, z_off // hw + h)), nspec],
        out_specs=tok,
        out_shape=jax.ShapeDtypeStruct((bsz, seq, B_WIDTH), BF16),
        scratch_shapes=[pltpu.VMEM((DN_HP, HEAD_DIM, HEAD_DIM), F32)],
        compiler_params=_cparams(("parallel", "parallel", "arbitrary")),
        name="deltanet_main",
    )(qn, kn, vn, beta, gc, gct, t_p, zg, norm_w)


def _mix_out_kernel(ya_ref, yb_ref, ga_ref, gb_ref, x_ref,
                    wa_ref, wb_ref, wo_ref, g_ref, b_ref, xo_ref, xbo_ref):
    ma = _dot(ya_ref[...], wa_ref[...])
    mb = _dot(yb_ref[...], wb_ref[...])
    merged = jax.nn.sigmoid(ga_ref[...].astype(F32)) * ma + jax.nn.sigmoid(gb_ref[...].astype(F32)) * mb
    r = _dot(merged.astype(BF16), wo_ref[...])
    y = _layer_norm(DN_ALPHA * x_ref[...] + r, g_ref[...], b_ref[...])
    xo_ref[...] = y
    xbo_ref[...] = y.astype(BF16)


def _mix_out(ya, yb, zg, gate_off, x, wa, wb, wo, g, b, tm=512):
    t = x.shape[0]
    d = D_MODEL
    aw = A_GROUP_WIDTH
    gblk = gate_off // d

    def rows(wd, c=0):
        return pl.BlockSpec((tm, wd), lambda i: (i, c))

    def whole(shape):
        return pl.BlockSpec(shape, lambda i: (0, 0))

    return pl.pallas_call(
        _mix_out_kernel,
        grid=(t // tm,),
        in_specs=[rows(aw), rows(d), rows(d, gblk), rows(d, gblk + 1), rows(d),
                                   whole((aw, d)), whole((d, d)), whole((d, d)), whole((1, d)), whole((1, d))],
        out_specs=[rows(d), rows(d)],
        out_shape=[jax.ShapeDtypeStruct((t, d), F32), jax.ShapeDtypeStruct((t, d), BF16)],
        compiler_params=_cparams(("parallel",)),
        name="mix_out",
    )(ya, yb, zg, zg, x, wa, wb, wo, g, b)


FFN_FC = 256


def _ffn_kernel(xb_ref, x_ref, wg_ref, wu_ref, wd_ref, g_ref, b_ref, xo_ref, xbo_ref):
    xb = xb_ref[...]
    acc = jnp.zeros(x_ref.shape, F32)
    for c in range(wg_ref.shape[1] // FFN_FC):
        cs = slice(c * FFN_FC, (c + 1) * FFN_FC)
        gt = _dot(xb, wg_ref[:, cs])
        up = _dot(xb, wu_ref[:, cs])
        hh = (gt * jax.nn.sigmoid(gt) * up).astype(BF16)
        acc = acc + _dot(hh, wd_ref[cs, :])
    y = _layer_norm(DN_ALPHA * x_ref[...] + acc, g_ref[...], b_ref[...])
    xo_ref[...] = y
    xbo_ref[...] = y.astype(BF16)


def _ffn(xb, x, wg, wu, wd, g, b, tm=512):
    t, d = x.shape
    rows = pl.BlockSpec((tm, d), lambda i: (i, 0))
    vec = pl.BlockSpec((1, d), lambda i: (0, 0))
    one = pl.Buffered(1)
    return pl.pallas_call(
        _ffn_kernel,
        grid=(t // tm,),
        in_specs=[rows, rows,
                  pl.BlockSpec(wg.shape, lambda i: (0, 0), pipeline_mode=one),
                  pl.BlockSpec(wu.shape, lambda i: (0, 0), pipeline_mode=one),
                  pl.BlockSpec(wd.shape, lambda i: (0, 0), pipeline_mode=one), vec, vec],
        out_specs=[rows, rows],
        out_shape=[jax.ShapeDtypeStruct((t, d), F32), jax.ShapeDtypeStruct((t, d), BF16)],
        compiler_params=_cparams(("parallel",)),
        name="dense_swiglu",
    )(xb, x, wg, wu, wd, g, b)


def _router_kernel(x_ref, rw_ref, tri_ref, gates_ref, rank_ref, selt_ref, cnt_ref):
    logits = jnp.dot(x_ref[...], rw_ref[...], precision=lax.Precision.HIGHEST, preferred_element_type=F32)
    lane = lax.broadcasted_iota(jnp.int32, logits.shape, 1)
    lanef = lane.astype(F32)
    ninf = -jnp.inf
    lg = jnp.where(lane < N_EXPERTS, logits, ninf)
    m1 = jnp.max(lg, axis=1, keepdims=True)
    i1 = jnp.min(jnp.where(lg == m1, lanef, float(LANES)), axis=1, keepdims=True)
    lg2 = jnp.where(lanef == i1, ninf, lg)
    m2 = jnp.max(lg2, axis=1, keepdims=True)
    i2 = jnp.min(jnp.where(lg2 == m2, lanef, float(LANES)), axis=1, keepdims=True)
    e = jnp.exp(m2 - m1)
    w1 = 1.0 / (1.0 + e)
    w2 = e / (1.0 + e)
    gates = jnp.where(lanef == i1, w1, jnp.where(lanef == i2, w2, 0.0))
    gates_ref[...] = gates
    sel = gates > 0.0
    onef = jnp.where(sel, 1.0, 0.0)
    rank = _dot(tri_ref[...], onef.astype(BF16))
    rank_ref[...] = rank
    selt_ref[...] = jnp.where(sel, rank, -1.0).T[0:N_EXPERTS, :]
    cnt_ref[...] = jnp.broadcast_to(jnp.sum(onef, axis=0, keepdims=True), cnt_ref.shape).astype(jnp.int32)


def _router(x, rw_pad, tm):
    t, d = x.shape
    nt = t // tm
    tri = jnp.tril(jnp.ones((tm, tm), BF16), -1)
    rows = pl.BlockSpec((tm, LANES), lambda i: (i, 0))
    return pl.pallas_call(
        _router_kernel,
        grid=(nt,),
        in_specs=[pl.BlockSpec((tm, d), lambda i: (i, 0)), pl.BlockSpec((d, LANES), lambda i: (0, 0)),
                  pl.BlockSpec((tm, tm), lambda i: (0, 0))],
        out_specs=[rows, rows, pl.BlockSpec((None, N_EXPERTS, tm), lambda i: (i, 0, 0)),
                   pl.BlockSpec((None, 8, LANES), lambda i: (i, 0, 0))],
        out_shape=[jax.ShapeDtypeStruct((t, LANES), F32), jax.ShapeDtypeStruct((t, LANES), F32),
                   jax.ShapeDtypeStruct((nt, N_EXPERTS, tm), F32), jax.ShapeDtypeStruct((nt, 8, LANES), jnp.int32)],
        compiler_params=_cparams(("parallel",)),
        name="moe_router",
    )(x, rw_pad, tri)


MOE_TM = 1024
MOE_CAP = 320
MOE_FC = 512


def _expert_kernel(xb_ref, selt_ref, wg_ref, wu_ref, wd_ref, oc_ref):
    e = pl.program_id(0)
    tm = xb_ref.shape[0]
    sel_row = selt_ref[pl.ds(e, 1), :]
    cidx = lax.broadcasted_iota(jnp.int32, (MOE_CAP, tm), 0).astype(F32)
    onehot = jnp.where(sel_row == cidx, 1.0, 0.0).astype(BF16)
    xc = _dot(onehot, xb_ref[...]).astype(BF16)
    acc = jnp.zeros((MOE_CAP, D_MODEL), F32)
    for c in range(wg_ref.shape[1] // MOE_FC):
        cs = slice(c * MOE_FC, (c + 1) * MOE_FC)
        gt = _dot(xc, wg_ref[:, cs])
        up = _dot(xc, wu_ref[:, cs])
        hh = (gt * jax.nn.sigmoid(gt) * up).astype(BF16)
        acc = acc + _dot(hh, wd_ref[cs, :])
    oc_ref[...] = acc.astype(oc_ref.dtype)


def _experts(xb, selt, wg, wu, wd):
    t, d = xb.shape
    ne, _, dex = wg.shape
    nt = t // MOE_TM
    one = pl.Buffered(1)
    return pl.pallas_call(
        _expert_kernel,
        grid=(ne, nt),
        in_specs=[pl.BlockSpec((MOE_TM, d), lambda e, i: (i, 0)),
                  pl.BlockSpec((None, N_EXPERTS, MOE_TM), lambda e, i: (i, 0, 0)),
                  pl.BlockSpec((None, d, dex), lambda e, i: (e, 0, 0), pipeline_mode=one),
                  pl.BlockSpec((None, d, dex), lambda e, i: (e, 0, 0), pipeline_mode=one),
                  pl.BlockSpec((None, dex, d), lambda e, i: (e, 0, 0), pipeline_mode=one)],
        out_specs=pl.BlockSpec((None, None, MOE_CAP, d), lambda e, i: (e, i, 0, 0)),
        out_shape=jax.ShapeDtypeStruct((ne, nt, MOE_CAP, d), BF16),
        compiler_params=_cparams(("arbitrary", "arbitrary")),
        name="moe_experts",
    )(xb, selt, wg, wu, wd)


def _combine_kernel(x_ref, gates_ref, rank_ref, oc_ref, g_ref, b_ref, xo_ref):
    tm = x_ref.shape[0]
    gates = gates_ref[...]
    rank = rank_ref[...]
    lane = lax.broadcasted_iota(jnp.int32, (tm, MOE_CAP), 1).astype(F32)
    y = jnp.zeros((tm, D_MODEL), F32)
    for e in range(N_EXPERTS):
        gcol = gates[:, e:e + 1]
        scol = jnp.where(gcol > 0.0, rank[:, e:e + 1], -1.0)
        onehot = jnp.where(scol == lane, 1.0, 0.0).astype(BF16)
        y = y + _dot(onehot, oc_ref[e]) * gcol
    xo_ref[...] = _layer_norm(DN_ALPHA * x_ref[...] + y, g_ref[...], b_ref[...])


def _combine(x, gates, rank, oc, g, b):
    t, d = x.shape
    ne = oc.shape[0]
    rows = pl.BlockSpec((MOE_TM, d), lambda i: (i, 0))
    lrows = pl.BlockSpec((MOE_TM, LANES), lambda i: (i, 0))
    vec = pl.BlockSpec((1, d), lambda i: (0, 0))
    return pl.pallas_call(
        _combine_kernel,
        grid=(t // MOE_TM,),
        in_specs=[rows, lrows, lrows, pl.BlockSpec((ne, None, MOE_CAP, d), lambda i: (0, i, 0, 0)), vec, vec],
        out_specs=rows,
        out_shape=jax.ShapeDtypeStruct((t, d), F32),
        compiler_params=_cparams(("parallel",)),
        name="moe_combine",
    )(x, gates, rank, oc, g, b)


def _moe_kernel(xb_ref, x_ref, gates_ref, wg_ref, wu_ref, wd_ref, g_ref, b_ref, xo_ref, acc):
    e = pl.program_id(1)
    f = pl.program_id(2)

    @pl.when(jnp.logical_and(e == 0, f == 0))
    def _():
        acc[...] = jnp.zeros_like(acc)

    xb = xb_ref[...]
    gate = _col_of(gates_ref[...], e)
    gt = _dot(xb, wg_ref[...])
    up = _dot(xb, wu_ref[...])
    hh = (gt * jax.nn.sigmoid(gt) * up * gate).astype(BF16)
    acc[...] += _dot(hh, wd_ref[...])

    @pl.when(jnp.logical_and(e == pl.num_programs(1) - 1, f == pl.num_programs(2) - 1))
    def _():
        xo_ref[...] = _layer_norm(DN_ALPHA * x_ref[...] + acc[...], g_ref[...], b_ref[...])


def _moe(xb, x, gates, wg, wu, wd, g, b, tm=1024, tf=512):
    t, d = x.shape
    ne, _, dex = wg.shape
    rows = pl.BlockSpec((tm, d), lambda i, e, f: (i, 0))
    vec = pl.BlockSpec((1, d), lambda i, e, f: (0, 0))
    return pl.pallas_call(
        _moe_kernel,
        grid=(t // tm, ne, dex // tf),
        in_specs=[rows, rows, pl.BlockSpec((tm, LANES), lambda i, e, f: (i, 0)),
                  pl.BlockSpec((None, d, tf), lambda i, e, f: (e, 0, f)),
                  pl.BlockSpec((None, d, tf), lambda i, e, f: (e, 0, f)),
                  pl.BlockSpec((None, tf, d), lambda i, e, f: (e, f, 0)), vec, vec],
        out_specs=rows,
        out_shape=jax.ShapeDtypeStruct((t, d), F32),
        scratch_shapes=[pltpu.VMEM((tm, d), F32)],
        compiler_params=_cparams(("parallel", "arbitrary", "arbitrary")),
        name="moe_swiglu",
    )(xb, x, gates, wg, wu, wd, g, b)


def _rope_tables(positions):
    half = ROT_DIM // 2
    inv_freq = ROPE_THETA ** (-jnp.arange(0, ROT_DIM, 2, dtype=F32) / ROT_DIM)
    ang = positions.astype(F32)[..., None] * inv_freq
    cos, sin = jnp.cos(ang), jnp.sin(ang)
    shp = cos.shape[:-1]
    c = jnp.concatenate([cos, cos, jnp.ones(shp + (LANES - ROT_DIM,), F32)], axis=-1)
    s1 = jnp.concatenate([-sin, jnp.zeros(shp + (LANES - half,), F32)], axis=-1)
    s2 = jnp.concatenate([jnp.zeros(shp + (half,), F32), sin, jnp.zeros(shp + (LANES - ROT_DIM,), F32)], axis=-1)
    return c, s1, s2


def kernel(x, positions, w_in, conv_w, a_log, dt_bias, dn_norm_w, w_branch_a, w_branch_b, w_out, ln1_g, ln1_b,
           ffn_w_gate, ffn_w_up, ffn_w_down, router_w, moe_w_gate, moe_w_up, moe_w_down, ln2_g, ln2_b):
    bsz, seq, d = x.shape
    t = bsz * seq
    nh = B_HEADS
    qa_w = 3 * A_QKV_WIDTH
    o_qkvb = qa_w
    o_z = o_qkvb + 3 * B_WIDTH
    o_bd = o_z + B_WIDTH
    o_gates = o_bd + 2 * nh

    tabs = [tb.reshape(t, LANES) for tb in _rope_tables(positions)]

    dn_tt = 1024
    xf = x.reshape(t, d)
    xb = xf.astype(BF16)
    for layer in range(DEPTH):
        w = w_in[layer]
        w_qkva = w[:, :qa_w].astype(BF16)
        w_b = jnp.concatenate([w[:, o_qkvb:o_bd], w[:, o_gates:]], axis=1).astype(BF16)
        w_bd = jnp.pad(w[:, o_bd:o_gates], ((0, 0), (0, LANES - 2 * nh))).astype(BF16)

        zeros8 = jnp.zeros((nh,), F32)
        arow = jnp.concatenate([zeros8, -jnp.exp(a_log[layer].astype(F32)), jnp.zeros((LANES - 2 * nh,), F32)])[None, :]
        dtrow = jnp.concatenate([zeros8, dt_bias[layer].astype(F32), jnp.zeros((LANES - 2 * nh,), F32)])[None, :]
        subs = _proj_a(xb, w_qkva, tabs, bsz, seq)
        qn, kn, vn, zg, beta, gc, gct, a_p = _proj_b(xb, w_b, w_bd, arow, dtrow, conv_w[layer].astype(F32),
                                                      bsz, seq, PROJ_B_TM)

        ya = _attention([subs[p * N_GROUPS:(p + 1) * N_GROUPS] for p in range(3)], bsz, seq)

        sh3 = (bsz, seq, -1)
        t_p = _tri_solve(a_p.reshape((-1,) + a_p.shape[2:]))
        yb = _dn_main(qn.reshape(sh3), kn.reshape(sh3), vn.reshape(sh3), beta.reshape(sh3), gc.reshape(sh3), gct,
                      t_p, zg.reshape(sh3), 0, dn_norm_w[layer].astype(F32)[None, :], dn_tt)

        xf, xb = _mix_out(ya.reshape(t, A_GROUP_WIDTH), yb.reshape(t, B_WIDTH), zg, B_WIDTH, xf,
                          w_branch_a[layer].astype(BF16), w_branch_b[layer].astype(BF16), w_out[layer].astype(BF16),
                          ln1_g[layer].astype(F32)[None, :], ln1_b[layer].astype(F32)[None, :])

        g2 = ln2_g[layer].astype(F32)[None, :]
        b2 = ln2_b[layer].astype(F32)[None, :]
        if layer % 2 == 0:
            i = layer // 2
            xf, xb = _ffn(xb, xf, ffn_w_gate[i].astype(BF16), ffn_w_up[i].astype(BF16), ffn_w_down[i].astype(BF16), g2, b2)
        else:
            i = layer // 2
            rw = jnp.pad(router_w[i].astype(F32), ((0, 0), (0, LANES - N_EXPERTS)))
            gates, rank, selt, cnt = _router(xf, rw, MOE_TM)
            wg = moe_w_gate[i].astype(BF16)
            wu = moe_w_up[i].astype(BF16)
            wd = moe_w_down[i].astype(BF16)

            def routed(xb, xf, gates, rank, selt):
                return _combine(xf, gates, rank, _experts(xb, selt, wg, wu, wd), g2, b2)

            def dense(xb, xf, gates, rank, selt):
                return _moe(xb, xf, gates, wg, wu, wd, g2, b2)

            xf = lax.cond(jnp.max(cnt) > MOE_CAP, dense, routed, xb, xf, gates, rank, selt)
            xb = xf.astype(BF16)
    return xf.reshape(bsz, seq, d)
```

```python
import functools
import math

import jax
import jax.numpy as jnp
from jax import lax
from jax.experimental import pallas as pl
from jax.experimental.pallas import tpu as pltpu

F32 = jnp.float32
BF16 = jnp.bfloat16

D_MODEL = 1024
DEPTH = 2
A_PAIRS = ((128, 1), (512, 4), (2048, 16))
A_HEADS = 4
HEAD_DIM = 128
A_GROUP_WIDTH = A_HEADS * HEAD_DIM
A_QKV_WIDTH = len(A_PAIRS) * A_GROUP_WIDTH
A_BLOCK = 128
ROPE_THETA = 500000.0
ROT_DIM = HEAD_DIM // 4
B_HEADS = 8
B_WIDTH = B_HEADS * HEAD_DIM
CONV_K = 4
CHUNK = 64
N_EXPERTS = 8
DN_ALPHA = (2 * DEPTH) ** 0.25
LN_EPS = 1e-5
RMS_EPS = 1e-6
NEG = -1e30
LANES = 128
VMEM_LIMIT = 56 * 1024 * 1024


def _cparams(sem):
    return pltpu.CompilerParams(dimension_semantics=sem, vmem_limit_bytes=VMEM_LIMIT)


def _dot(a, b):
    return jnp.dot(a, b, preferred_element_type=F32)


def _dot_nt(a, b):
    return lax.dot_general(a, b, (((1,), (1,)), ((), ())), preferred_element_type=F32)


def _dot_tn(a, b):
    return lax.dot_general(a, b, (((0,), (0,)), ((), ())), preferred_element_type=F32)


def _col_of(x, idx):
    lane = lax.broadcasted_iota(jnp.int32, x.shape, 1)
    return jnp.sum(jnp.where(lane == idx, x, 0.0), axis=1, keepdims=True)


def _layer_norm(v, g, b):
    mu = jnp.mean(v, axis=-1, keepdims=True)
    c = v - mu
    var = jnp.mean(c * c, axis=-1, keepdims=True)
    return c * lax.rsqrt(var + LN_EPS) * g + b


def _rope(t, c, s1, s2):
    return t * c + pltpu.roll(t, LANES - ROT_DIM // 2, 1) * s1 + pltpu.roll(t, ROT_DIM // 2, 1) * s2


PB_FC = 1024
HALO = 16
DN_HG = 4


def _proj_b_kernel(npb, x_ref, w_ref, wbd_ref, arow_ref, dtrow_ref, cw_ref,
                   qo_ref, ko_ref, vo_ref, zg_ref, beta_ref, gc_ref, gct_ref, a_ref, scr, carry):
    tm = x_ref.shape[0]
    first = pl.program_id(0) % npb == 0
    x = x_ref[...]

    @pl.when(pl.program_id(0) == 0)
    def _():
        carry[...] = jnp.zeros_like(carry)

    bd = _dot(x, wbd_ref[...])
    beta = jax.nn.sigmoid(bd)
    beta_ref[...] = beta
    y = bd + dtrow_ref[...]
    softplus = jnp.maximum(y, 0.0) + jnp.log(1.0 + jnp.exp(-jnp.abs(y)))
    g = arow_ref[...] * softplus
    ri = lax.broadcasted_iota(jnp.int32, (tm, tm), 0)
    ci = lax.broadcasted_iota(jnp.int32, (tm, tm), 1)
    tri = jnp.where(jnp.logical_and(ri // CHUNK == ci // CHUNK, ci <= ri), 1.0, 0.0).astype(F32)
    gc = jnp.dot(tri, g, precision=lax.Precision.HIGHEST, preferred_element_type=F32)
    gc_ref[...] = gc
    gct = gc.T
    gct_ref[...] = gct[B_HEADS:2 * B_HEADS, :]

    def conv_silu(col, raw):
        scr[col, 0:HALO, :] = jnp.where(first, 0.0, carry[col])
        scr[col, HALO:, :] = raw
        carry[col] = raw[tm - HALO:, :]
        w = cw_ref[:, col * HEAD_DIM:(col + 1) * HEAD_DIM]
        yy = scr[col, HALO - 3:HALO - 3 + tm, :] * w[0:1, :]
        for j in range(1, CONV_K):
            yy = yy + scr[col, HALO - 3 + j:HALO - 3 + j + tm, :] * w[j:j + 1, :]
        return yy * jax.nn.sigmoid(yy)

    def l2n(v):
        return v * lax.rsqrt(jnp.sum(v * v, axis=1, keepdims=True) + RMS_EPS)

    g2 = 2 * CHUNK
    r2 = lax.broadcasted_iota(jnp.int32, (g2, g2), 0)
    c2 = lax.broadcasted_iota(jnp.int32, (g2, g2), 1)
    strict = jnp.logical_and((r2 >= CHUNK) == (c2 >= CHUNK), r2 > c2)
    lane = lax.broadcasted_iota(jnp.int32, (CHUNK, g2), 1)
    tps = tm // g2

    def plain_chunk(c):
        zg_ref[:, (c - 3) * PB_FC:(c - 2) * PB_FC] = _dot(x, w_ref[:, c * PB_FC:(c + 1) * PB_FC]).astype(zg_ref.dtype)

    acc = _dot(x, w_ref[:, 0:PB_FC])
    plain_chunk(3)
    for h in range(B_HEADS):
        hs = slice(h * HEAD_DIM, (h + 1) * HEAD_DIM)
        qo_ref[:, hs] = (l2n(conv_silu(h, acc[:, hs])) * (HEAD_DIM ** -0.5)).astype(qo_ref.dtype)
    acc = _dot(x, w_ref[:, 2 * PB_FC:3 * PB_FC])
    plain_chunk(4)
    for h in range(B_HEADS):
        hs = slice(h * HEAD_DIM, (h + 1) * HEAD_DIM)
        vo_ref[:, hs] = conv_silu(2 * B_HEADS + h, acc[:, hs]).astype(vo_ref.dtype)
    acc = _dot(x, w_ref[:, PB_FC:2 * PB_FC])
    plain_chunk(5)
    for h in range(B_HEADS):
        hs = slice(h * HEAD_DIM, (h + 1) * HEAD_DIM)
        k = l2n(conv_silu(B_HEADS + h, acc[:, hs])).astype(BF16)
        ko_ref[:, hs] = k
        bcol = beta[:, h:h + 1]
        gcol = gc[:, B_HEADS + h:B_HEADS + h + 1]
        grow = gct[B_HEADS + h:B_HEADS + h + 1, :]
        for gi in range(tps):
            rs = slice(gi * g2, (gi + 1) * g2)
            kb = k[rs, :]
            kbeta = (kb.astype(F32) * bcol[rs, :]).astype(BF16)
            diff = gcol[rs, :] - grow[:, rs]
            a = _dot_nt(kbeta, kb) * jnp.exp(jnp.where(strict, diff, NEG))
            a_ref[h // DN_HG, :, (h % DN_HG) * tps + gi, :] = jnp.where(lane < CHUNK, a[0:CHUNK, :], a[CHUNK:g2, :])


def _proj_b(xb, w, w_bd, arow, dtrow, conv_w, bsz, seq, tm):
    t, k = xb.shape
    n = w.shape[1]
    npb = seq // tm
    tps = tm // (2 * CHUNK)
    slots = DN_HG * tps
    one = pl.Buffered(1)
    rows = pl.BlockSpec((tm, B_WIDTH), lambda i: (i, 0))
    lrow = pl.BlockSpec((tm, LANES), lambda i: (i, 0))
    rspec = pl.BlockSpec((1, LANES), lambda i: (0, 0))
    ng = B_HEADS // DN_HG
    return pl.pallas_call(
        functools.partial(_proj_b_kernel, npb),
        grid=(t // tm,),
        in_specs=[pl.BlockSpec((tm, k), lambda i: (i, 0)),
                  pl.BlockSpec(w.shape, lambda i: (0, 0), pipeline_mode=one),
                  pl.BlockSpec(w_bd.shape, lambda i: (0, 0), pipeline_mode=one), rspec, rspec,
                  pl.BlockSpec(conv_w.shape, lambda i: (0, 0))],
        out_specs=[rows, rows, rows, pl.BlockSpec((tm, n - 3 * B_WIDTH), lambda i: (i, 0)), lrow, lrow,
                   pl.BlockSpec((B_HEADS, tm), lambda i: (0, i)),
                   pl.BlockSpec((None, ng, CHUNK, slots, 2 * CHUNK), lambda i: (i // npb, 0, 0, i % npb, 0))],
        out_shape=[jax.ShapeDtypeStruct((t, B_WIDTH), BF16)] * 3
        + [jax.ShapeDtypeStruct((t, n - 3 * B_WIDTH), BF16)]
        + [jax.ShapeDtypeStruct((t, LANES), F32)] * 2
        + [jax.ShapeDtypeStruct((B_HEADS, t), F32),
           jax.ShapeDtypeStruct((bsz, ng, CHUNK, slots * npb, 2 * CHUNK), F32)],
        scratch_shapes=[pltpu.VMEM((3 * B_HEADS, tm + HALO, HEAD_DIM), F32),
                        pltpu.VMEM((3 * B_HEADS, HALO, HEAD_DIM), F32)],
        compiler_params=_cparams(("arbitrary",)),
        name="proj_mixer_b",
    )(xb, w, w_bd, arow, dtrow, conv_w)


N_GROUPS = len(A_PAIRS)
DILS = tuple(d for _, d in A_PAIRS)


def _proj_a_kernel(x_ref, w_ref, c_ref, s1_ref, s2_ref, *refs):
    outs = refs[:3 * N_GROUPS]
    scr = refs[3 * N_GROUPS]
    tm = x_ref.shape[0]
    w = A_GROUP_WIDTH
    scale = 1.0 / math.sqrt(HEAD_DIM)
    x = x_ref[...]
    for jj in range(3 * N_GROUPS):
        p, g = divmod(jj, N_GROUPS)
        dil = DILS[g]
        o_ref = outs[jj]
        acc = _dot(x, w_ref[:, jj * w:(jj + 1) * w])
        for h in range(A_HEADS):
            sl = slice(h * HEAD_DIM, (h + 1) * HEAD_DIM)
            t = acc[:, sl]
            if p < 2:
                t = _rope(t, c_ref[...], s1_ref[...], s2_ref[...])
            if p == 0:
                t = t * scale
            if dil == 1:
                o_ref[:, sl] = t.astype(o_ref.dtype)
            else:
                slot = (jj % 2) * A_HEADS + h
                scr[slot] = t
                for r in range(dil):
                    o_ref[r, :, sl] = scr[slot, pl.ds(r, tm // dil, stride=dil), :].astype(o_ref.dtype)


def _proj_a(xb, w_qkva, tabs, bsz, seq, tm=512):
    t, k = xb.shape
    npb = seq // tm
    w = A_GROUP_WIDTH
    out_specs, out_shape = [], []
    for jj in range(3 * N_GROUPS):
        dil = DILS[jj % N_GROUPS]
        if dil == 1:
            out_specs.append(pl.BlockSpec((None, None, tm, w), lambda i: (i // npb, 0, i % npb, 0)))
        else:
            out_specs.append(pl.BlockSpec((None, dil, tm // dil, w), lambda i: (i // npb, 0, i % npb, 0)))
        out_shape.append(jax.ShapeDtypeStruct((bsz, dil, seq // dil, w), BF16))
    tab = pl.BlockSpec((tm, LANES), lambda i: (i, 0))
    return pl.pallas_call(
        _proj_a_kernel,
        grid=(t // tm,),
        in_specs=[pl.BlockSpec((tm, k), lambda i: (i, 0)),
                  pl.BlockSpec(w_qkva.shape, lambda i: (0, 0), pipeline_mode=pl.Buffered(1)), tab, tab, tab],
        out_specs=out_specs,
        out_shape=out_shape,
        scratch_shapes=[pltpu.VMEM((2 * A_HEADS, tm, HEAD_DIM), F32)],
        compiler_params=_cparams(("parallel",)),
        name="proj_mixer_a",
    )(xb, w_qkva, *tabs)


ATT_TT = 2048
ATT_UNROLL = 16


def _attn_kernel(*refs):
    ins = refs[:5 * N_GROUPS]
    o_ref = refs[5 * N_GROUPS]
    od, ld, on, ln = refs[5 * N_GROUPS + 1:5 * N_GROUPS + 5]
    kvbufs = refs[5 * N_GROUPS + 5:]
    it = pl.program_id(1)
    row = lax.broadcasted_iota(jnp.int32, (A_BLOCK, 2 * A_BLOCK), 0)
    col = lax.broadcasted_iota(jnp.int32, (A_BLOCK, 2 * A_BLOCK), 1)
    band = jnp.logical_and(col >= row, col <= row + A_BLOCK)
    bias_all = jnp.where(band, 0.0, NEG).astype(F32)
    bias_own = jnp.where(jnp.logical_and(band, col >= A_BLOCK), 0.0, NEG).astype(F32)
    nblocks = ATT_TT // A_BLOCK
    for g, dil in enumerate(DILS):
        q_ref, k_ref, v_ref, kh_ref, vh_ref = ins[5 * g:5 * g + 5]
        kbuf, vbuf = kvbufs[2 * g:2 * g + 2]
        nlb = nblocks // dil
        rows = ATT_TT // dil
        kbuf[:, 0:A_BLOCK, :] = kh_ref[...]
        kbuf[:, A_BLOCK:, :] = k_ref[...]
        vbuf[:, 0:A_BLOCK, :] = vh_ref[...]
        vbuf[:, A_BLOCK:, :] = v_ref[...]

        def block(c, carry, g=g, nlb=nlb, rows=rows, q_ref=q_ref, kbuf=kbuf, vbuf=vbuf):
            r = c // nlb
            nb = c % nlb
            off = pl.multiple_of(nb * A_BLOCK, A_BLOCK)
            q = q_ref[r, pl.ds(off, A_BLOCK), :]
            kk = kbuf[r, pl.ds(off, 2 * A_BLOCK), :]
            vv = vbuf[r, pl.ds(off, 2 * A_BLOCK), :]
            has_prev = jnp.logical_or(nb > 0, it > 0)
            s = _dot_nt(q, kk) + jnp.where(has_prev, bias_all, bias_own)
            m = jnp.max(s, axis=1, keepdims=True)
            p = jnp.exp(s - m)
            den = jnp.sum(p, axis=1, keepdims=True)
            o = _dot(p.astype(BF16), vv) / den
            dst = pl.multiple_of(r * rows + off, A_BLOCK)
            od[g, pl.ds(dst, A_BLOCK), :] = o
            ld[g, pl.ds(dst, A_BLOCK), :] = jnp.broadcast_to(m + jnp.log(den), (A_BLOCK, HEAD_DIM))
            return carry

        lax.fori_loop(0, nblocks, block, 0, unroll=ATT_UNROLL)
    for g, dil in enumerate(DILS):
        if dil == 1:
            continue
        rows = ATT_TT // dil
        for r in range(dil):
            on[g - 1, pl.ds(r, rows, stride=dil), :] = od[g, r * rows:(r + 1) * rows, :]
            ln[g - 1, pl.ds(r, rows, stride=dil), :] = ld[g, r * rows:(r + 1) * rows, :]
    step = 256
    for c in range(ATT_TT // step):
        sl = slice(c * step, (c + 1) * step)
        lse = [ld[0, sl, :]] + [ln[g - 1, sl, :] for g in range(1, N_GROUPS)]
        outs = [od[0, sl, :]] + [on[g - 1, sl, :] for g in range(1, N_GROUPS)]
        m = functools.reduce(jnp.maximum, lse)
        es = [jnp.exp(l - m) for l in lse]
        num = functools.reduce(lambda a, b: a + b, [e * o for e, o in zip(es, outs)])
        o_ref[sl, :] = (num / functools.reduce(lambda a, b: a + b, es)).astype(o_ref.dtype)


def _attention(qkv_sub, bsz, seq):
    specs, args = [], []
    for g, dil in enumerate(DILS):
        rows = ATT_TT // dil
        hb = rows // A_BLOCK
        cur = pl.BlockSpec((None, dil, rows, HEAD_DIM), lambda b, i, h: (b, 0, i, h))
        halo = pl.BlockSpec((None, dil, A_BLOCK, HEAD_DIM), lambda b, i, h, hb=hb: (b, 0, jnp.maximum(i * hb - 1, 0), h))
        specs += [cur, cur, cur, halo, halo]
        args += [qkv_sub[0][g], qkv_sub[1][g], qkv_sub[2][g], qkv_sub[1][g], qkv_sub[2][g]]
    return pl.pallas_call(
        _attn_kernel,
        grid=(bsz, seq // ATT_TT, A_HEADS),
        in_specs=specs,
        out_specs=pl.BlockSpec((None, ATT_TT, HEAD_DIM), lambda b, i, h: (b, i, h)),
        out_shape=jax.ShapeDtypeStruct((bsz, seq, A_GROUP_WIDTH), BF16),
        scratch_shapes=[pltpu.VMEM((N_GROUPS, ATT_TT, HEAD_DIM), F32)] * 2
        + [pltpu.VMEM((N_GROUPS - 1, ATT_TT, HEAD_DIM), F32)] * 2
        + [pltpu.VMEM((dil, ATT_TT // dil + A_BLOCK, HEAD_DIM), BF16) for dil in DILS for _ in range(2)],
        compiler_params=_cparams(("parallel", "parallel", "parallel")),
        name="dilated_attention",
    )(*args)


SOLVE_TILES = 128
SUB = 8


def _tri_solve_kernel(a_ref, t_ref, at, tt):
    def load_row(i, c):
        at[i] = a_ref[i].T
        return c

    lax.fori_loop(0, CHUNK, load_row, 0, unroll=4)
    tt[...] = jnp.zeros_like(tt)
    rowid = lax.broadcasted_iota(jnp.int32, (SUB, SOLVE_TILES), 0)
    nk = CHUNK // SUB

    def solve_row(i, c):
        acc = tuple(jnp.zeros((SUB, SOLVE_TILES), F32) for _ in range(2 * nk))
        for mb in range(nk):
            def apply_block(acc, mb=mb):
                new = list(acc)
                for m in range(mb * SUB, (mb + 1) * SUB):
                    a0 = jnp.broadcast_to(at[i, m:m + 1, :], (SUB, SOLVE_TILES))
                    a1 = jnp.broadcast_to(at[i, CHUNK + m:CHUNK + m + 1, :], (SUB, SOLVE_TILES))
                    for k in range(mb + 1):
                        new[k] = new[k] - a0 * tt[m, k * SUB:(k + 1) * SUB, :]
                        new[nk + k] = new[nk + k] - a1 * tt[m, CHUNK + k * SUB:CHUNK + (k + 1) * SUB, :]
                return tuple(new)

            acc = lax.cond(mb * SUB < i, apply_block, lambda a: a, acc)
        for k in range(nk):
            diag = jnp.where(rowid + k * SUB == i, 1.0, 0.0).astype(F32)
            tt[i, k * SUB:(k + 1) * SUB, :] = acc[k] + diag
            tt[i, CHUNK + k * SUB:CHUNK + (k + 1) * SUB, :] = acc[nk + k] + diag
        return c

    lax.fori_loop(0, CHUNK, solve_row, 0)

    def store_row(i, c):
        t_ref[i] = tt[i].T
        return c

    lax.fori_loop(0, CHUNK, store_row, 0, unroll=4)


def _tri_solve(a_p):
    nb, _, slots, _ = a_p.shape
    assert slots == SOLVE_TILES
    spec = pl.BlockSpec((None, CHUNK, SOLVE_TILES, 2 * CHUNK), lambda n: (n, 0, 0, 0))
    return pl.pallas_call(
        _tri_solve_kernel,
        grid=(nb,),
        in_specs=[spec],
        out_specs=spec,
        out_shape=jax.ShapeDtypeStruct(a_p.shape, F32),
        scratch_shapes=[pltpu.VMEM((CHUNK, 2 * CHUNK, SOLVE_TILES), F32)] * 2,
        compiler_params=_cparams(("parallel",)),
        name="deltanet_tri_solve",
    )(a_p)


DN_HP = DN_HG
PROJ_B_TM = 512
PROD_TPS = PROJ_B_TM // (2 * CHUNK)


def _dn_main_kernel(q_ref, k_ref, v_ref, beta_ref, gc_ref, gct_ref, t_ref, o_ref, state):
    t = pl.program_id(2)
    hh = pl.program_id(1)
    tt = q_ref.shape[0]
    g2 = 2 * CHUNK

    @pl.when(t == 0)
    def _():
        state[...] = jnp.zeros_like(state)

    ri = lax.broadcasted_iota(jnp.int32, (g2, g2), 0)
    ci = lax.broadcasted_iota(jnp.int32, (g2, g2), 1)
    incl = jnp.logical_and((ri >= CHUNK) == (ci >= CHUNK), ri >= ci)
    lane = lax.broadcasted_iota(jnp.int32, (CHUNK, g2), 1)
    beta = beta_ref[...]
    gc = gc_ref[...]
    bcols, gcols, grows, states = [], [], [], []
    for hp in range(DN_HP):
        h = hh * DN_HP + hp
        bcols.append(_col_of(beta, h))
        gcols.append(_col_of(gc, B_HEADS + h))
        grows.append(gct_ref[pl.ds(h, 1), :])
        states.append(state[hp])
    for gi in range(tt // g2):
        rs = slice(gi * g2, (gi + 1) * g2)
        pre = []
        for hp in range(DN_HP):
            hs = slice(hp * HEAD_DIM, (hp + 1) * HEAD_DIM)
            qb = q_ref[rs, hs]
            kb = k_ref[rs, hs]
            kf = kb.astype(F32)
            b = bcols[hp][rs, :]
            gcl = gcols[hp][rs, :]
            eg = jnp.exp(gcl)
            kbeta = kf * b
            rhs = jnp.concatenate([v_ref[rs, hs].astype(F32) * b, kbeta * eg], axis=1).astype(BF16)
            tp = t_ref[:, (gi // PROD_TPS) * DN_HP * PROD_TPS + hp * PROD_TPS + gi % PROD_TPS, :]
            tbd = jnp.concatenate([jnp.where(lane < CHUNK, tp, 0.0), jnp.where(lane >= CHUNK, tp, 0.0)], axis=0)
            uw = _dot(tbd.astype(BF16), rhs).astype(BF16)
            diff = gcl - grows[hp][:, rs]
            attn = (_dot_nt(qb, kb) * jnp.exp(jnp.where(incl, diff, NEG))).astype(BF16)
            auw = _dot(attn, uw)
            qeff = (qb.astype(F32) * eg - auw[:, HEAD_DIM:]).astype(BF16)
            per_chunk = []
            for cc in range(2):
                cs = slice(cc * CHUNK, (cc + 1) * CHUNK)
                glast = gcl[cc * CHUNK + CHUNK - 1:cc * CHUNK + CHUNK, :]
                ktail = (kf[cs, :] * jnp.exp(glast - gcl[cs, :])).astype(BF16)
                kuw = _dot_tn(ktail, uw[cs, :])
                lhs = jnp.concatenate([kuw[:, HEAD_DIM:].astype(BF16), qeff[cs, :]], axis=0)
                per_chunk.append((lhs, kuw[:, :HEAD_DIM], jnp.exp(glast), auw[cs, :HEAD_DIM]))
            pre.append(per_chunk)
        outs = [[] for _ in range(DN_HP)]
        for cc in range(2):
            for hp in range(DN_HP):
                lhs, ku, decay, au = pre[hp][cc]
                s = states[hp]
                xs = _dot(lhs, s.astype(BF16))
                outs[hp].append(xs[HEAD_DIM:, :] + au)
                states[hp] = s * decay + ku - xs[:HEAD_DIM, :]
        for hp in range(DN_HP):
            hs = slice(hp * HEAD_DIM, (hp + 1) * HEAD_DIM)
            o_ref[rs, hs] = jnp.concatenate(outs[hp], axis=0).astype(o_ref.dtype)
    for hp in range(DN_HP):
        state[hp] = states[hp]


def _dn_main(qn, kn, vn, beta, gc, gct, t_p, tt):
    bsz, seq, _ = qn.shape
    nh = B_HEADS
    hw = DN_HP * HEAD_DIM
    tok = pl.BlockSpec((None, tt, hw), lambda b, h, t: (b, t, h))
    full = pl.BlockSpec((None, tt, LANES), lambda b, h, t: (b, t, 0))
    nst = seq // tt
    gts = pl.BlockSpec((nh, tt), lambda b, h, t: (0, b * nst + t))
    tspec = pl.BlockSpec((None, CHUNK, DN_HP * tt // (2 * CHUNK), 2 * CHUNK),
                         lambda b, h, t: (b * (nh // DN_HP) + h, 0, t, 0))
    return pl.pallas_call(
        _dn_main_kernel,
        grid=(bsz, nh // DN_HP, seq // tt),
        in_specs=[tok, tok, tok, full, full, gts, tspec],
        out_specs=tok,
        out_shape=jax.ShapeDtypeStruct((bsz, seq, B_WIDTH), BF16),
        scratch_shapes=[pltpu.VMEM((DN_HP, HEAD_DIM, HEAD_DIM), F32)],
        compiler_params=_cparams(("parallel", "parallel", "arbitrary")),
        name="deltanet_main",
    )(qn, kn, vn, beta, gc, gct, t_p)


def _mix_out_kernel(ya_ref, ob_ref, z_ref, nw_ref, ga_ref, gb_ref, x_ref,
                    wa_ref, wb_ref, wo_ref, g_ref, b_ref, xo_ref, xbo_ref):
    ma = _dot(ya_ref[...], wa_ref[...])
    nw = nw_ref[...]
    parts = []
    for h in range(B_HEADS):
        hs = slice(h * HEAD_DIM, (h + 1) * HEAD_DIM)
        o = ob_ref[:, hs].astype(F32)
        z = z_ref[:, hs].astype(F32)
        o = o * lax.rsqrt(jnp.mean(o * o, axis=1, keepdims=True) + RMS_EPS) * nw
        parts.append((o * (z * jax.nn.sigmoid(z))).astype(BF16))
    mb = _dot(jnp.concatenate(parts, axis=1), wb_ref[...])
    merged = jax.nn.sigmoid(ga_ref[...].astype(F32)) * ma + jax.nn.sigmoid(gb_ref[...].astype(F32)) * mb
    r = _dot(merged.astype(BF16), wo_ref[...])
    y = _layer_norm(DN_ALPHA * x_ref[...] + r, g_ref[...], b_ref[...])
    xo_ref[...] = y
    xbo_ref[...] = y.astype(BF16)


def _mix_out(ya, ob, zg, gate_off, norm_w, x, wa, wb, wo, g, b, tm=512):
    t = x.shape[0]
    d = D_MODEL
    aw = A_GROUP_WIDTH
    gblk = gate_off // d

    def rows(wd, c=0):
        return pl.BlockSpec((tm, wd), lambda i: (i, c))

    def whole(shape):
        return pl.BlockSpec(shape, lambda i: (0, 0))

    return pl.pallas_call(
        _mix_out_kernel,
        grid=(t // tm,),
        in_specs=[rows(aw), rows(d), rows(d, 0), whole((1, HEAD_DIM)), rows(d, gblk), rows(d, gblk + 1), rows(d),
                  whole((aw, d)), whole((d, d)), whole((d, d)), whole((1, d)), whole((1, d))],
        out_specs=[rows(d), rows(d)],
        out_shape=[jax.ShapeDtypeStruct((t, d), F32), jax.ShapeDtypeStruct((t, d), BF16)],
        compiler_params=_cparams(("parallel",)),
        name="mix_out",
    )(ya, ob, zg, norm_w, zg, zg, x, wa, wb, wo, g, b)


FFN_FC = 256


def _ffn_kernel(xb_ref, x_ref, wg_ref, wu_ref, wd_ref, g_ref, b_ref, xo_ref, xbo_ref):
    xb = xb_ref[...]
    acc = jnp.zeros(x_ref.shape, F32)
    for c in range(wg_ref.shape[1] // FFN_FC):
        cs = slice(c * FFN_FC, (c + 1) * FFN_FC)
        gt = _dot(xb, wg_ref[:, cs])
        up = _dot(xb, wu_ref[:, cs])
        hh = (gt * jax.nn.sigmoid(gt) * up).astype(BF16)
        acc = acc + _dot(hh, wd_ref[cs, :])
    y = _layer_norm(DN_ALPHA * x_ref[...] + acc, g_ref[...], b_ref[...])
    xo_ref[...] = y
    xbo_ref[...] = y.astype(BF16)


def _ffn(xb, x, wg, wu, wd, g, b, tm=512):
    t, d = x.shape
    rows = pl.BlockSpec((tm, d), lambda i: (i, 0))
    vec = pl.BlockSpec((1, d), lambda i: (0, 0))
    one = pl.Buffered(1)
    return pl.pallas_call(
        _ffn_kernel,
        grid=(t // tm,),
        in_specs=[rows, rows,
                  pl.BlockSpec(wg.shape, lambda i: (0, 0), pipeline_mode=one),
                  pl.BlockSpec(wu.shape, lambda i: (0, 0), pipeline_mode=one),
                  pl.BlockSpec(wd.shape, lambda i: (0, 0), pipeline_mode=one), vec, vec],
        out_specs=[rows, rows],
        out_shape=[jax.ShapeDtypeStruct((t, d), F32), jax.ShapeDtypeStruct((t, d), BF16)],
        compiler_params=_cparams(("parallel",)),
        name="dense_swiglu",
    )(xb, x, wg, wu, wd, g, b)


def _router_kernel(x_ref, rw_ref, tri_ref, gates_ref, rank_ref, selt_ref, cnt_ref):
    x = x_ref[...]
    rw = rw_ref[...]
    x_hi = x.astype(BF16)
    x_lo = (x - x_hi.astype(F32)).astype(BF16)
    w_hi = rw.astype(BF16)
    w_lo = (rw - w_hi.astype(F32)).astype(BF16)
    logits = _dot(x_hi, w_hi) + (_dot(x_hi, w_lo) + _dot(x_lo, w_hi))
    lane = lax.broadcasted_iota(jnp.int32, logits.shape, 1)
    lanef = lane.astype(F32)
    ninf = -jnp.inf
    lg = jnp.where(lane < N_EXPERTS, logits, ninf)
    m1 = jnp.max(lg, axis=1, keepdims=True)
    i1 = jnp.min(jnp.where(lg == m1, lanef, float(LANES)), axis=1, keepdims=True)
    lg2 = jnp.where(lanef == i1, ninf, lg)
    m2 = jnp.max(lg2, axis=1, keepdims=True)
    i2 = jnp.min(jnp.where(lg2 == m2, lanef, float(LANES)), axis=1, keepdims=True)
    e = jnp.exp(m2 - m1)
    w1 = 1.0 / (1.0 + e)
    w2 = e / (1.0 + e)
    gates = jnp.where(lanef == i1, w1, jnp.where(lanef == i2, w2, 0.0))
    gates_ref[...] = gates
    sel = gates > 0.0
    onef = jnp.where(sel, 1.0, 0.0)
    rank = _dot(tri_ref[...], onef.astype(BF16))
    rank_ref[...] = rank
    selt_ref[...] = jnp.where(sel, rank, -1.0).T[0:N_EXPERTS, :]
    cnt_ref[...] = jnp.broadcast_to(jnp.sum(onef, axis=0, keepdims=True), cnt_ref.shape).astype(jnp.int32)


def _router(x, rw_pad, tm):
    t, d = x.shape
    nt = t // tm
    tri = jnp.tril(jnp.ones((tm, tm), BF16), -1)
    rows = pl.BlockSpec((tm, LANES), lambda i: (i, 0))
    return pl.pallas_call(
        _router_kernel,
        grid=(nt,),
        in_specs=[pl.BlockSpec((tm, d), lambda i: (i, 0)), pl.BlockSpec((d, LANES), lambda i: (0, 0)),
                  pl.BlockSpec((tm, tm), lambda i: (0, 0))],
        out_specs=[rows, rows, pl.BlockSpec((None, N_EXPERTS, tm), lambda i: (i, 0, 0)),
                   pl.BlockSpec((None, 8, LANES), lambda i: (i, 0, 0))],
        out_shape=[jax.ShapeDtypeStruct((t, LANES), F32), jax.ShapeDtypeStruct((t, LANES), F32),
                   jax.ShapeDtypeStruct((nt, N_EXPERTS, tm), F32), jax.ShapeDtypeStruct((nt, 8, LANES), jnp.int32)],
        compiler_params=_cparams(("parallel",)),
        name="moe_router",
    )(x, rw_pad, tri)


MOE_TM = 1024
MOE_CAP = 320
MOE_MAIN = 288
MOE_FC = 512


def _expert_kernel(cnt_ref, xb_ref, selt_ref, wg_ref, wu_ref, wd_ref, oc_ref):
    e = pl.program_id(0)
    i = pl.program_id(1)
    tm = xb_ref.shape[0]
    sel_row = selt_ref[pl.ds(e, 1), :]

    def expert_rows(r0, n):
        cidx = lax.broadcasted_iota(jnp.int32, (n, tm), 0).astype(F32) + float(r0)
        onehot = jnp.where(sel_row == cidx, 1.0, 0.0).astype(BF16)
        xc = _dot(onehot, xb_ref[...]).astype(BF16)
        acc = jnp.zeros((n, D_MODEL), F32)
        for c in range(wg_ref.shape[1] // MOE_FC):
            cs = slice(c * MOE_FC, (c + 1) * MOE_FC)
            gt = _dot(xc, wg_ref[:, cs])
            up = _dot(xc, wu_ref[:, cs])
            hh = (gt * jax.nn.sigmoid(gt) * up).astype(BF16)
            acc = acc + _dot(hh, wd_ref[cs, :])
        return acc.astype(oc_ref.dtype)

    oc_ref[0:MOE_MAIN, :] = expert_rows(0, MOE_MAIN)
    many = cnt_ref[i * N_EXPERTS + e] > MOE_MAIN

    @pl.when(many)
    def _():
        oc_ref[MOE_MAIN:, :] = expert_rows(MOE_MAIN, MOE_CAP - MOE_MAIN)

    @pl.when(jnp.logical_not(many))
    def _():
        oc_ref[MOE_MAIN:, :] = jnp.zeros((MOE_CAP - MOE_MAIN, D_MODEL), oc_ref.dtype)


def _experts(cnt, xb, selt, wg, wu, wd):
    t, d = xb.shape
    ne, _, dex = wg.shape
    nt = t // MOE_TM
    one = pl.Buffered(1)
    grid_spec = pltpu.PrefetchScalarGridSpec(
        num_scalar_prefetch=1,
        grid=(ne, nt),
        in_specs=[pl.BlockSpec((MOE_TM, d), lambda e, i, c: (i, 0)),
                  pl.BlockSpec((None, N_EXPERTS, MOE_TM), lambda e, i, c: (i, 0, 0)),
                  pl.BlockSpec((None, d, dex), lambda e, i, c: (e, 0, 0), pipeline_mode=one),
                  pl.BlockSpec((None, d, dex), lambda e, i, c: (e, 0, 0), pipeline_mode=one),
                  pl.BlockSpec((None, dex, d), lambda e, i, c: (e, 0, 0), pipeline_mode=one)],
        out_specs=pl.BlockSpec((None, None, MOE_CAP, d), lambda e, i, c: (e, i, 0, 0)),
    )
    return pl.pallas_call(
        _expert_kernel,
        grid_spec=grid_spec,
        out_shape=jax.ShapeDtypeStruct((ne, nt, MOE_CAP, d), BF16),
        compiler_params=_cparams(("arbitrary", "arbitrary")),
        name="moe_experts",
    )(cnt, xb, selt, wg, wu, wd)


def _combine_kernel(x_ref, gates_ref, rank_ref, oc_ref, g_ref, b_ref, xo_ref):
    tm = x_ref.shape[0]
    gates = gates_ref[...]
    rank = rank_ref[...]
    lane = lax.broadcasted_iota(jnp.int32, (tm, MOE_CAP), 1).astype(F32)
    y = jnp.zeros((tm, D_MODEL), F32)
    for e in range(N_EXPERTS):
        gcol = gates[:, e:e + 1]
        scol = jnp.where(gcol > 0.0, rank[:, e:e + 1], -1.0)
        onehot = jnp.where(scol == lane, 1.0, 0.0).astype(BF16)
        y = y + _dot(onehot, oc_ref[e]) * gcol
    xo_ref[...] = _layer_norm(DN_ALPHA * x_ref[...] + y, g_ref[...], b_ref[...])


def _combine(x, gates, rank, oc, g, b):
    t, d = x.shape
    ne = oc.shape[0]
    rows = pl.BlockSpec((MOE_TM, d), lambda i: (i, 0))
    lrows = pl.BlockSpec((MOE_TM, LANES), lambda i: (i, 0))
    vec = pl.BlockSpec((1, d), lambda i: (0, 0))
    return pl.pallas_call(
        _combine_kernel,
        grid=(t // MOE_TM,),
        in_specs=[rows, lrows, lrows, pl.BlockSpec((ne, None, MOE_CAP, d), lambda i: (0, i, 0, 0)), vec, vec],
        out_specs=rows,
        out_shape=jax.ShapeDtypeStruct((t, d), F32),
        compiler_params=_cparams(("parallel",)),
        name="moe_combine",
    )(x, gates, rank, oc, g, b)


def _moe_kernel(xb_ref, x_ref, gates_ref, wg_ref, wu_ref, wd_ref, g_ref, b_ref, xo_ref, acc):
    e = pl.program_id(1)
    f = pl.program_id(2)

    @pl.when(jnp.logical_and(e == 0, f == 0))
    def _():
        acc[...] = jnp.zeros_like(acc)

    xb = xb_ref[...]
    gate = _col_of(gates_ref[...], e)
    gt = _dot(xb, wg_ref[...])
    up = _dot(xb, wu_ref[...])
    hh = (gt * jax.nn.sigmoid(gt) * up * gate).astype(BF16)
    acc[...] += _dot(hh, wd_ref[...])

    @pl.when(jnp.logical_and(e == pl.num_programs(1) - 1, f == pl.num_programs(2) - 1))
    def _():
        xo_ref[...] = _layer_norm(DN_ALPHA * x_ref[...] + acc[...], g_ref[...], b_ref[...])


def _moe(xb, x, gates, wg, wu, wd, g, b, tm=1024, tf=512):
    t, d = x.shape
    ne, _, dex = wg.shape
    rows = pl.BlockSpec((tm, d), lambda i, e, f: (i, 0))
    vec = pl.BlockSpec((1, d), lambda i, e, f: (0, 0))
    return pl.pallas_call(
        _moe_kernel,
        grid=(t // tm, ne, dex // tf),
        in_specs=[rows, rows, pl.BlockSpec((tm, LANES), lambda i, e, f: (i, 0)),
                  pl.BlockSpec((None, d, tf), lambda i, e, f: (e, 0, f)),
                  pl.BlockSpec((None, d, tf), lambda i, e, f: (e, 0, f)),
                  pl.BlockSpec((None, tf, d), lambda i, e, f: (e, f, 0)), vec, vec],
        out_specs=rows,
        out_shape=jax.ShapeDtypeStruct((t, d), F32),
        scratch_shapes=[pltpu.VMEM((tm, d), F32)],
        compiler_params=_cparams(("parallel", "arbitrary", "arbitrary")),
        name="moe_swiglu",
    )(xb, x, gates, wg, wu, wd, g, b)


def _rope_tables(positions):
    half = ROT_DIM // 2
    inv_freq = ROPE_THETA ** (-jnp.arange(0, ROT_DIM, 2, dtype=F32) / ROT_DIM)
    ang = positions.astype(F32)[..., None] * inv_freq
    cos, sin = jnp.cos(ang), jnp.sin(ang)
    shp = cos.shape[:-1]
    c = jnp.concatenate([cos, cos, jnp.ones(shp + (LANES - ROT_DIM,), F32)], axis=-1)
    s1 = jnp.concatenate([-sin, jnp.zeros(shp + (LANES - half,), F32)], axis=-1)
    s2 = jnp.concatenate([jnp.zeros(shp + (half,), F32), sin, jnp.zeros(shp + (LANES - ROT_DIM,), F32)], axis=-1)
    return c, s1, s2


def kernel(x, positions, w_in, conv_w, a_log, dt_bias, dn_norm_w, w_branch_a, w_branch_b, w_out, ln1_g, ln1_b,
           ffn_w_gate, ffn_w_up, ffn_w_down, router_w, moe_w_gate, moe_w_up, moe_w_down, ln2_g, ln2_b):
    bsz, seq, d = x.shape
    t = bsz * seq
    nh = B_HEADS
    qa_w = 3 * A_QKV_WIDTH
    o_qkvb = qa_w
    o_z = o_qkvb + 3 * B_WIDTH
    o_bd = o_z + B_WIDTH
    o_gates = o_bd + 2 * nh

    tabs = [tb.reshape(t, LANES) for tb in _rope_tables(positions)]

    dn_tt = 1024
    xf = x.reshape(t, d)
    xb = xf.astype(BF16)
    for layer in range(DEPTH):
        w = w_in[layer]
        w_qkva = w[:, :qa_w].astype(BF16)
        w_b = jnp.concatenate([w[:, o_qkvb:o_bd], w[:, o_gates:]], axis=1).astype(BF16)
        w_bd = jnp.pad(w[:, o_bd:o_gates], ((0, 0), (0, LANES - 2 * nh))).astype(BF16)

        zeros8 = jnp.zeros((nh,), F32)
        arow = jnp.concatenate([zeros8, -jnp.exp(a_log[layer].astype(F32)), jnp.zeros((LANES - 2 * nh,), F32)])[None, :]
        dtrow = jnp.concatenate([zeros8, dt_bias[layer].astype(F32), jnp.zeros((LANES - 2 * nh,), F32)])[None, :]
        subs = _proj_a(xb, w_qkva, tabs, bsz, seq)
        qn, kn, vn, zg, beta, gc, gct, a_p = _proj_b(xb, w_b, w_bd, arow, dtrow, conv_w[layer].astype(F32),
                                                      bsz, seq, PROJ_B_TM)

        ya = _attention([subs[p * N_GROUPS:(p + 1) * N_GROUPS] for p in range(3)], bsz, seq)

        sh3 = (bsz, seq, -1)
        t_p = _tri_solve(a_p.reshape((-1,) + a_p.shape[2:]))
        ob = _dn_main(qn.reshape(sh3), kn.reshape(sh3), vn.reshape(sh3), beta.reshape(sh3), gc.reshape(sh3), gct,
                      t_p, dn_tt)

        xf, xb = _mix_out(ya.reshape(t, A_GROUP_WIDTH), ob.reshape(t, B_WIDTH), zg, B_WIDTH,
                          dn_norm_w[layer].astype(F32)[None, :], xf,
                          w_branch_a[layer].astype(BF16), w_branch_b[layer].astype(BF16), w_out[layer].astype(BF16),
                          ln1_g[layer].astype(F32)[None, :], ln1_b[layer].astype(F32)[None, :])

        g2 = ln2_g[layer].astype(F32)[None, :]
        b2 = ln2_b[layer].astype(F32)[None, :]
        if layer % 2 == 0:
            i = layer // 2
            xf, xb = _ffn(xb, xf, ffn_w_gate[i].astype(BF16), ffn_w_up[i].astype(BF16), ffn_w_down[i].astype(BF16), g2, b2)
        else:
            i = layer // 2
            rw = jnp.pad(router_w[i].astype(F32), ((0, 0), (0, LANES - N_EXPERTS)))
            gates, rank, selt, cnt = _router(xf, rw, MOE_TM)
            wg = moe_w_gate[i].astype(BF16)
            wu = moe_w_up[i].astype(BF16)
            wd = moe_w_down[i].astype(BF16)

            cnt_flat = cnt[:, 0, :N_EXPERTS].reshape(-1)

            def routed(xb, xf, gates, rank, selt, cnt_flat):
                return _combine(xf, gates, rank, _experts(cnt_flat, xb, selt, wg, wu, wd), g2, b2)

            def dense(xb, xf, gates, rank, selt, cnt_flat):
                return _moe(xb, xf, gates, wg, wu, wd, g2, b2)

            xf = lax.cond(jnp.max(cnt_flat) > MOE_CAP, dense, routed, xb, xf, gates, rank, selt, cnt_flat)
    return xf.reshape(bsz, seq, d)
```

```python
import functools
import math

import jax
import jax.numpy as jnp
from jax import lax
from jax.experimental import pallas as pl
from jax.experimental.pallas import tpu as pltpu

F32 = jnp.float32
BF16 = jnp.bfloat16

D_MODEL = 1024
DEPTH = 2
A_PAIRS = ((128, 1), (512, 4), (2048, 16))
A_HEADS = 4
HEAD_DIM = 128
A_GROUP_WIDTH = A_HEADS * HEAD_DIM
A_QKV_WIDTH = len(A_PAIRS) * A_GROUP_WIDTH
A_BLOCK = 128
ROPE_THETA = 500000.0
ROT_DIM = HEAD_DIM // 4
B_HEADS = 8
B_WIDTH = B_HEADS * HEAD_DIM
CONV_K = 4
CHUNK = 64
N_EXPERTS = 8
DN_ALPHA = (2 * DEPTH) ** 0.25
LN_EPS = 1e-5
RMS_EPS = 1e-6
NEG = -1e30
LANES = 128
VMEM_LIMIT = 56 * 1024 * 1024


def _cparams(sem):
    return pltpu.CompilerParams(dimension_semantics=sem, vmem_limit_bytes=VMEM_LIMIT)


def _dot(a, b):
    return jnp.dot(a, b, preferred_element_type=F32)


def _dot_nt(a, b):
    return lax.dot_general(a, b, (((1,), (1,)), ((), ())), preferred_element_type=F32)


def _dot_tn(a, b):
    return lax.dot_general(a, b, (((0,), (0,)), ((), ())), preferred_element_type=F32)


def _col_of(x, idx):
    lane = lax.broadcasted_iota(jnp.int32, x.shape, 1)
    return jnp.sum(jnp.where(lane == idx, x, 0.0), axis=1, keepdims=True)


def _layer_norm(v, g, b):
    mu = jnp.mean(v, axis=-1, keepdims=True)
    c = v - mu
    var = jnp.mean(c * c, axis=-1, keepdims=True)
    return c * lax.rsqrt(var + LN_EPS) * g + b


def _rope(t, c, s1, s2):
    return t * c + pltpu.roll(t, LANES - ROT_DIM // 2, 1) * s1 + pltpu.roll(t, ROT_DIM // 2, 1) * s2


PB_FC = 1024
HALO = 16
DN_HG = 4


def _proj_b_kernel(npb, x_ref, w_ref, wbd_ref, arow_ref, dtrow_ref, cw_ref,
                   qo_ref, ko_ref, vo_ref, zg_ref, beta_ref, gc_ref, gct_ref, a_ref, scr, carry):
    tm = x_ref.shape[0]
    first = pl.program_id(0) % npb == 0
    x = x_ref[...].astype(BF16)

    @pl.when(pl.program_id(0) == 0)
    def _():
        carry[...] = jnp.zeros_like(carry)

    bd = _dot(x, wbd_ref[...])
    beta = jax.nn.sigmoid(bd)
    beta_ref[...] = beta
    y = bd + dtrow_ref[...]
    softplus = jnp.maximum(y, 0.0) + jnp.log(1.0 + jnp.exp(-jnp.abs(y)))
    g = arow_ref[...] * softplus
    ri = lax.broadcasted_iota(jnp.int32, (tm, tm), 0)
    ci = lax.broadcasted_iota(jnp.int32, (tm, tm), 1)
    tri = jnp.where(jnp.logical_and(ri // CHUNK == ci // CHUNK, ci <= ri), 1.0, 0.0).astype(F32)
    gc = jnp.dot(tri, g, precision=lax.Precision.HIGHEST, preferred_element_type=F32)
    gc_ref[...] = gc
    gct = gc.T
    gct_ref[...] = gct[B_HEADS:2 * B_HEADS, :]

    def conv_silu(col, raw):
        scr[col, 0:HALO, :] = jnp.where(first, 0.0, carry[col])
        scr[col, HALO:, :] = raw
        carry[col] = raw[tm - HALO:, :]
        w = cw_ref[:, col * HEAD_DIM:(col + 1) * HEAD_DIM]
        yy = scr[col, HALO - 3:HALO - 3 + tm, :] * w[0:1, :]
        for j in range(1, CONV_K):
            yy = yy + scr[col, HALO - 3 + j:HALO - 3 + j + tm, :] * w[j:j + 1, :]
        return yy * jax.nn.sigmoid(yy)

    def l2n(v):
        return v * lax.rsqrt(jnp.sum(v * v, axis=1, keepdims=True) + RMS_EPS)

    g2 = 2 * CHUNK
    r2 = lax.broadcasted_iota(jnp.int32, (g2, g2), 0)
    c2 = lax.broadcasted_iota(jnp.int32, (g2, g2), 1)
    strict = jnp.logical_and((r2 >= CHUNK) == (c2 >= CHUNK), r2 > c2)
    lane = lax.broadcasted_iota(jnp.int32, (CHUNK, g2), 1)
    tps = tm // g2

    def plain_chunk(c):
        zg_ref[:, (c - 3) * PB_FC:(c - 2) * PB_FC] = _dot(x, w_ref[:, c * PB_FC:(c + 1) * PB_FC]).astype(zg_ref.dtype)

    acc = _dot(x, w_ref[:, 0:PB_FC])
    plain_chunk(3)
    for h in range(B_HEADS):
        hs = slice(h * HEAD_DIM, (h + 1) * HEAD_DIM)
        qo_ref[:, hs] = (l2n(conv_silu(h, acc[:, hs])) * (HEAD_DIM ** -0.5)).astype(qo_ref.dtype)
    acc = _dot(x, w_ref[:, 2 * PB_FC:3 * PB_FC])
    plain_chunk(4)
    for h in range(B_HEADS):
        hs = slice(h * HEAD_DIM, (h + 1) * HEAD_DIM)
        vo_ref[:, hs] = conv_silu(2 * B_HEADS + h, acc[:, hs]).astype(vo_ref.dtype)
    acc = _dot(x, w_ref[:, PB_FC:2 * PB_FC])
    plain_chunk(5)
    for h in range(B_HEADS):
        hs = slice(h * HEAD_DIM, (h + 1) * HEAD_DIM)
        k = l2n(conv_silu(B_HEADS + h, acc[:, hs])).astype(BF16)
        ko_ref[:, hs] = k
        bcol = beta[:, h:h + 1]
        gcol = gc[:, B_HEADS + h:B_HEADS + h + 1]
        grow = gct[B_HEADS + h:B_HEADS + h + 1, :]
        for gi in range(tps):
            rs = slice(gi * g2, (gi + 1) * g2)
            kb = k[rs, :]
            kbeta = (kb.astype(F32) * bcol[rs, :]).astype(BF16)
            diff = gcol[rs, :] - grow[:, rs]
            a = _dot_nt(kbeta, kb) * jnp.exp(jnp.where(strict, diff, NEG))
            a_ref[h // DN_HG, :, (h % DN_HG) * tps + gi, :] = jnp.where(lane < CHUNK, a[0:CHUNK, :], a[CHUNK:g2, :])


def _proj_b(xb, w, w_bd, arow, dtrow, conv_w, bsz, seq, tm):
    t, k = xb.shape
    n = w.shape[1]
    npb = seq // tm
    tps = tm // (2 * CHUNK)
    slots = DN_HG * tps
    one = pl.Buffered(1)
    rows = pl.BlockSpec((tm, B_WIDTH), lambda i: (i, 0))
    lrow = pl.BlockSpec((tm, LANES), lambda i: (i, 0))
    rspec = pl.BlockSpec((1, LANES), lambda i: (0, 0))
    ng = B_HEADS // DN_HG
    return pl.pallas_call(
        functools.partial(_proj_b_kernel, npb),
        grid=(t // tm,),
        in_specs=[pl.BlockSpec((tm, k), lambda i: (i, 0)),
                  pl.BlockSpec(w.shape, lambda i: (0, 0), pipeline_mode=one),
                  pl.BlockSpec(w_bd.shape, lambda i: (0, 0), pipeline_mode=one), rspec, rspec,
                  pl.BlockSpec(conv_w.shape, lambda i: (0, 0))],
        out_specs=[rows, rows, rows, pl.BlockSpec((tm, n - 3 * B_WIDTH), lambda i: (i, 0)), lrow, lrow,
                   pl.BlockSpec((B_HEADS, tm), lambda i: (0, i)),
                   pl.BlockSpec((None, ng, CHUNK, slots, 2 * CHUNK), lambda i: (i // npb, 0, 0, i % npb, 0))],
        out_shape=[jax.ShapeDtypeStruct((t, B_WIDTH), BF16)] * 3
        + [jax.ShapeDtypeStruct((t, n - 3 * B_WIDTH), BF16)]
        + [jax.ShapeDtypeStruct((t, LANES), F32)] * 2
        + [jax.ShapeDtypeStruct((B_HEADS, t), F32),
           jax.ShapeDtypeStruct((bsz, ng, CHUNK, slots * npb, 2 * CHUNK), F32)],
        scratch_shapes=[pltpu.VMEM((3 * B_HEADS, tm + HALO, HEAD_DIM), F32),
                        pltpu.VMEM((3 * B_HEADS, HALO, HEAD_DIM), F32)],
        compiler_params=_cparams(("arbitrary",)),
        name="proj_mixer_b",
    )(xb, w, w_bd, arow, dtrow, conv_w)


N_GROUPS = len(A_PAIRS)
DILS = tuple(d for _, d in A_PAIRS)


def _proj_a_kernel(x_ref, w_ref, c_ref, s1_ref, s2_ref, *refs):
    outs = refs[:3 * N_GROUPS]
    scr = refs[3 * N_GROUPS]
    tm = x_ref.shape[0]
    w = A_GROUP_WIDTH
    scale = 1.0 / math.sqrt(HEAD_DIM)
    x = x_ref[...].astype(BF16)
    for jj in range(3 * N_GROUPS):
        p, g = divmod(jj, N_GROUPS)
        dil = DILS[g]
        o_ref = outs[jj]
        acc = _dot(x, w_ref[:, jj * w:(jj + 1) * w])
        for h in range(A_HEADS):
            sl = slice(h * HEAD_DIM, (h + 1) * HEAD_DIM)
            t = acc[:, sl]
            if p < 2:
                t = _rope(t, c_ref[...], s1_ref[...], s2_ref[...])
            if p == 0:
                t = t * scale
            if dil == 1:
                o_ref[:, sl] = t.astype(o_ref.dtype)
            else:
                slot = (jj % 2) * A_HEADS + h
                scr[slot] = t
                for r in range(dil):
                    o_ref[r, :, sl] = scr[slot, pl.ds(r, tm // dil, stride=dil), :].astype(o_ref.dtype)


def _proj_a(xb, w_qkva, tabs, bsz, seq, tm=512):
    t, k = xb.shape
    npb = seq // tm
    w = A_GROUP_WIDTH
    out_specs, out_shape = [], []
    for jj in range(3 * N_GROUPS):
        dil = DILS[jj % N_GROUPS]
        if dil == 1:
            out_specs.append(pl.BlockSpec((None, None, tm, w), lambda i: (i // npb, 0, i % npb, 0)))
        else:
            out_specs.append(pl.BlockSpec((None, dil, tm // dil, w), lambda i: (i // npb, 0, i % npb, 0)))
        out_shape.append(jax.ShapeDtypeStruct((bsz, dil, seq // dil, w), BF16))
    tab = pl.BlockSpec((tm, LANES), lambda i: (i, 0))
    return pl.pallas_call(
        _proj_a_kernel,
        grid=(t // tm,),
        in_specs=[pl.BlockSpec((tm, k), lambda i: (i, 0)),
                  pl.BlockSpec(w_qkva.shape, lambda i: (0, 0), pipeline_mode=pl.Buffered(1)), tab, tab, tab],
        out_specs=out_specs,
        out_shape=out_shape,
        scratch_shapes=[pltpu.VMEM((2 * A_HEADS, tm, HEAD_DIM), F32)],
        compiler_params=_cparams(("parallel",)),
        name="proj_mixer_a",
    )(xb, w_qkva, *tabs)


ATT_TT = 2048
ATT_UNROLL = 16


def _attn_kernel(*refs):
    ins = refs[:5 * N_GROUPS]
    o_ref = refs[5 * N_GROUPS]
    od, ld, on, ln = refs[5 * N_GROUPS + 1:5 * N_GROUPS + 5]
    kvbufs = refs[5 * N_GROUPS + 5:]
    it = pl.program_id(1)
    row = lax.broadcasted_iota(jnp.int32, (A_BLOCK, 2 * A_BLOCK), 0)
    col = lax.broadcasted_iota(jnp.int32, (A_BLOCK, 2 * A_BLOCK), 1)
    band = jnp.logical_and(col >= row, col <= row + A_BLOCK)
    bias_all = jnp.where(band, 0.0, NEG).astype(F32)
    bias_own = jnp.where(jnp.logical_and(band, col >= A_BLOCK), 0.0, NEG).astype(F32)
    nblocks = ATT_TT // A_BLOCK
    for g, dil in enumerate(DILS):
        q_ref, k_ref, v_ref, kh_ref, vh_ref = ins[5 * g:5 * g + 5]
        kbuf, vbuf = kvbufs[2 * g:2 * g + 2]
        nlb = nblocks // dil
        rows = ATT_TT // dil
        kbuf[:, 0:A_BLOCK, :] = kh_ref[...]
        kbuf[:, A_BLOCK:, :] = k_ref[...]
        vbuf[:, 0:A_BLOCK, :] = vh_ref[...]
        vbuf[:, A_BLOCK:, :] = v_ref[...]

        def block(c, carry, g=g, nlb=nlb, rows=rows, q_ref=q_ref, kbuf=kbuf, vbuf=vbuf):
            r = c // nlb
            nb = c % nlb
            off = pl.multiple_of(nb * A_BLOCK, A_BLOCK)
            q = q_ref[r, pl.ds(off, A_BLOCK), :]
            kk = kbuf[r, pl.ds(off, 2 * A_BLOCK), :]
            vv = vbuf[r, pl.ds(off, 2 * A_BLOCK), :]
            has_prev = jnp.logical_or(nb > 0, it > 0)
            s = _dot_nt(q, kk) + jnp.where(has_prev, bias_all, bias_own)
            m = jnp.max(s, axis=1, keepdims=True)
            p = jnp.exp(s - m)
            den = jnp.sum(p, axis=1, keepdims=True)
            o = _dot(p.astype(BF16), vv) / den
            dst = pl.multiple_of(r * rows + off, A_BLOCK)
            od[g, pl.ds(dst, A_BLOCK), :] = o
            ld[g, pl.ds(dst, A_BLOCK), :] = jnp.broadcast_to(m + jnp.log(den), (A_BLOCK, HEAD_DIM))
            return carry

        lax.fori_loop(0, nblocks, block, 0, unroll=ATT_UNROLL)
    for g, dil in enumerate(DILS):
        if dil == 1:
            continue
        rows = ATT_TT // dil
        for r in range(dil):
            on[g - 1, pl.ds(r, rows, stride=dil), :] = od[g, r * rows:(r + 1) * rows, :]
            ln[g - 1, pl.ds(r, rows, stride=dil), :] = ld[g, r * rows:(r + 1) * rows, :]
    step = 256
    for c in range(ATT_TT // step):
        sl = slice(c * step, (c + 1) * step)
        lse = [ld[0, sl, :]] + [ln[g - 1, sl, :] for g in range(1, N_GROUPS)]
        outs = [od[0, sl, :]] + [on[g - 1, sl, :] for g in range(1, N_GROUPS)]
        m = functools.reduce(jnp.maximum, lse)
        es = [jnp.exp(l - m) for l in lse]
        num = functools.reduce(lambda a, b: a + b, [e * o for e, o in zip(es, outs)])
        o_ref[sl, :] = (num / functools.reduce(lambda a, b: a + b, es)).astype(o_ref.dtype)


def _attention(qkv_sub, bsz, seq):
    specs, args = [], []
    for g, dil in enumerate(DILS):
        rows = ATT_TT // dil
        hb = rows // A_BLOCK
        cur = pl.BlockSpec((None, dil, rows, HEAD_DIM), lambda b, i, h: (b, 0, i, h))
        halo = pl.BlockSpec((None, dil, A_BLOCK, HEAD_DIM), lambda b, i, h, hb=hb: (b, 0, jnp.maximum(i * hb - 1, 0), h))
        specs += [cur, cur, cur, halo, halo]
        args += [qkv_sub[0][g], qkv_sub[1][g], qkv_sub[2][g], qkv_sub[1][g], qkv_sub[2][g]]
    return pl.pallas_call(
        _attn_kernel,
        grid=(bsz, seq // ATT_TT, A_HEADS),
        in_specs=specs,
        out_specs=pl.BlockSpec((None, ATT_TT, HEAD_DIM), lambda b, i, h: (b, i, h)),
        out_shape=jax.ShapeDtypeStruct((bsz, seq, A_GROUP_WIDTH), BF16),
        scratch_shapes=[pltpu.VMEM((N_GROUPS, ATT_TT, HEAD_DIM), F32)] * 2
        + [pltpu.VMEM((N_GROUPS - 1, ATT_TT, HEAD_DIM), F32)] * 2
        + [pltpu.VMEM((dil, ATT_TT // dil + A_BLOCK, HEAD_DIM), BF16) for dil in DILS for _ in range(2)],
        compiler_params=_cparams(("parallel", "parallel", "parallel")),
        name="dilated_attention",
    )(*args)


SOLVE_TILES = 128
SUB = 8


def _tri_solve_kernel(a_ref, t_ref, at, tt):
    def load_row(i, c):
        at[i] = a_ref[i].T
        return c

    lax.fori_loop(0, CHUNK, load_row, 0, unroll=4)
    tt[...] = jnp.zeros_like(tt)
    rowid = lax.broadcasted_iota(jnp.int32, (SUB, SOLVE_TILES), 0)
    nk = CHUNK // SUB

    def solve_row(i, c):
        acc = tuple(jnp.zeros((SUB, SOLVE_TILES), F32) for _ in range(2 * nk))
        for mb in range(nk):
            def apply_block(acc, mb=mb):
                new = list(acc)
                for m in range(mb * SUB, (mb + 1) * SUB):
                    a0 = jnp.broadcast_to(at[i, m:m + 1, :], (SUB, SOLVE_TILES))
                    a1 = jnp.broadcast_to(at[i, CHUNK + m:CHUNK + m + 1, :], (SUB, SOLVE_TILES))
                    for k in range(mb + 1):
                        new[k] = new[k] - a0 * tt[m, k * SUB:(k + 1) * SUB, :]
                        new[nk + k] = new[nk + k] - a1 * tt[m, CHUNK + k * SUB:CHUNK + (k + 1) * SUB, :]
                return tuple(new)

            acc = lax.cond(mb * SUB < i, apply_block, lambda a: a, acc)
        for k in range(nk):
            diag = jnp.where(rowid + k * SUB == i, 1.0, 0.0).astype(F32)
            tt[i, k * SUB:(k + 1) * SUB, :] = acc[k] + diag
            tt[i, CHUNK + k * SUB:CHUNK + (k + 1) * SUB, :] = acc[nk + k] + diag
        return c

    lax.fori_loop(0, CHUNK, solve_row, 0)

    def store_row(i, c):
        t_ref[i] = tt[i].T
        return c

    lax.fori_loop(0, CHUNK, store_row, 0, unroll=4)


def _tri_solve(a_p):
    nb, _, slots, _ = a_p.shape
    assert slots == SOLVE_TILES
    spec = pl.BlockSpec((None, CHUNK, SOLVE_TILES, 2 * CHUNK), lambda n: (n, 0, 0, 0))
    return pl.pallas_call(
        _tri_solve_kernel,
        grid=(nb,),
        in_specs=[spec],
        out_specs=spec,
        out_shape=jax.ShapeDtypeStruct(a_p.shape, F32),
        scratch_shapes=[pltpu.VMEM((CHUNK, 2 * CHUNK, SOLVE_TILES), F32)] * 2,
        compiler_params=_cparams(("parallel",)),
        name="deltanet_tri_solve",
    )(a_p)


DN_HP = DN_HG
PROJ_B_TM = 512
PROD_TPS = PROJ_B_TM // (2 * CHUNK)


def _dn_main_kernel(q_ref, k_ref, v_ref, beta_ref, gc_ref, gct_ref, t_ref, z_ref, nw_ref, o_ref, state):
    t = pl.program_id(2)
    hh = pl.program_id(1)
    tt = q_ref.shape[0]
    g2 = 2 * CHUNK

    @pl.when(t == 0)
    def _():
        state[...] = jnp.zeros_like(state)

    ri = lax.broadcasted_iota(jnp.int32, (g2, g2), 0)
    ci = lax.broadcasted_iota(jnp.int32, (g2, g2), 1)
    incl = jnp.logical_and((ri >= CHUNK) == (ci >= CHUNK), ri >= ci)
    lane = lax.broadcasted_iota(jnp.int32, (CHUNK, g2), 1)
    nw = nw_ref[...]
    beta = beta_ref[...]
    gc = gc_ref[...]
    bcols, gcols, grows, states = [], [], [], []
    for hp in range(DN_HP):
        h = hh * DN_HP + hp
        bcols.append(_col_of(beta, h))
        gcols.append(_col_of(gc, B_HEADS + h))
        grows.append(gct_ref[pl.ds(h, 1), :])
        states.append(state[hp])
    for gi in range(tt // g2):
        rs = slice(gi * g2, (gi + 1) * g2)
        pre = []
        for hp in range(DN_HP):
            hs = slice(hp * HEAD_DIM, (hp + 1) * HEAD_DIM)
            qb = q_ref[rs, hs]
            kb = k_ref[rs, hs]
            kf = kb.astype(F32)
            b = bcols[hp][rs, :]
            gcl = gcols[hp][rs, :]
            eg = jnp.exp(gcl)
            kbeta = kf * b
            rhs = jnp.concatenate([v_ref[rs, hs].astype(F32) * b, kbeta * eg], axis=1).astype(BF16)
            tp = t_ref[:, (gi // PROD_TPS) * DN_HP * PROD_TPS + hp * PROD_TPS + gi % PROD_TPS, :]
            tbd = jnp.concatenate([jnp.where(lane < CHUNK, tp, 0.0), jnp.where(lane >= CHUNK, tp, 0.0)], axis=0)
            uw = _dot(tbd.astype(BF16), rhs).astype(BF16)
            diff = gcl - grows[hp][:, rs]
            attn = (_dot_nt(qb, kb) * jnp.exp(jnp.where(incl, diff, NEG))).astype(BF16)
            auw = _dot(attn, uw)
            qeff = (qb.astype(F32) * eg - auw[:, HEAD_DIM:]).astype(BF16)
            per_chunk = []
            for cc in range(2):
                cs = slice(cc * CHUNK, (cc + 1) * CHUNK)
                glast = gcl[cc * CHUNK + CHUNK - 1:cc * CHUNK + CHUNK, :]
                ktail = (kf[cs, :] * jnp.exp(glast - gcl[cs, :])).astype(BF16)
                kuw = _dot_tn(ktail, uw[cs, :])
                lhs = jnp.concatenate([kuw[:, HEAD_DIM:].astype(BF16), qeff[cs, :]], axis=0)
                per_chunk.append((lhs, kuw[:, :HEAD_DIM], jnp.exp(glast), auw[cs, :HEAD_DIM]))
            pre.append(per_chunk)
        outs = [[] for _ in range(DN_HP)]
        for cc in range(2):
            for hp in range(DN_HP):
                lhs, ku, decay, au = pre[hp][cc]
                s = states[hp]
                xs = _dot(lhs, s.astype(BF16))
                outs[hp].append(xs[HEAD_DIM:, :] + au)
                states[hp] = s * decay + ku - xs[:HEAD_DIM, :]
        for hp in range(DN_HP):
            hs = slice(hp * HEAD_DIM, (hp + 1) * HEAD_DIM)
            o = jnp.concatenate(outs[hp], axis=0)
            o = o * lax.rsqrt(jnp.mean(o * o, axis=1, keepdims=True) + RMS_EPS) * nw
            z = z_ref[rs, hs].astype(F32)
            o_ref[rs, hs] = (o * (z * jax.nn.sigmoid(z))).astype(o_ref.dtype)
    for hp in range(DN_HP):
        state[hp] = states[hp]


def _dn_main(qn, kn, vn, beta, gc, gct, t_p, zg, z_off, norm_w, tt):
    bsz, seq, _ = qn.shape
    nh = B_HEADS
    hw = DN_HP * HEAD_DIM
    tok = pl.BlockSpec((None, tt, hw), lambda b, h, t: (b, t, h))
    full = pl.BlockSpec((None, tt, LANES), lambda b, h, t: (b, t, 0))
    nst = seq // tt
    gts = pl.BlockSpec((nh, tt), lambda b, h, t: (0, b * nst + t))
    tspec = pl.BlockSpec((None, CHUNK, DN_HP * tt // (2 * CHUNK), 2 * CHUNK),
                         lambda b, h, t: (b * (nh // DN_HP) + h, 0, t, 0))
    nspec = pl.BlockSpec((1, HEAD_DIM), lambda b, h, t: (0, 0))
    return pl.pallas_call(
        _dn_main_kernel,
        grid=(bsz, nh // DN_HP, seq // tt),
        in_specs=[tok, tok, tok, full, full, gts, tspec,
                  pl.BlockSpec((None, tt, hw), lambda b, h, t: (b, t, z_off // hw + h)), nspec],
        out_specs=tok,
        out_shape=jax.ShapeDtypeStruct((bsz, seq, B_WIDTH), BF16),
        scratch_shapes=[pltpu.VMEM((DN_HP, HEAD_DIM, HEAD_DIM), F32)],
        compiler_params=_cparams(("parallel", "parallel", "arbitrary")),
        name="deltanet_main",
    )(qn, kn, vn, beta, gc, gct, t_p, zg, norm_w)


def _mix_out_kernel(ya_ref, yb_ref, ga_ref, gb_ref, x_ref,
                    wa_ref, wb_ref, wo_ref, g_ref, b_ref, xo_ref, xbo_ref):
    ma = _dot(ya_ref[...], wa_ref[...])
    mb = _dot(yb_ref[...], wb_ref[...])
    merged = jax.nn.sigmoid(ga_ref[...].astype(F32)) * ma + jax.nn.sigmoid(gb_ref[...].astype(F32)) * mb
    r = _dot(merged.astype(BF16), wo_ref[...])
    y = _layer_norm(DN_ALPHA * x_ref[...] + r, g_ref[...], b_ref[...])
    xo_ref[...] = y
    xbo_ref[...] = y.astype(BF16)


def _mix_out(ya, yb, zg, gate_off, x, wa, wb, wo, g, b, tm=512):
    t = x.shape[0]
    d = D_MODEL
    aw = A_GROUP_WIDTH
    gblk = gate_off // d

    def rows(wd, c=0):
        return pl.BlockSpec((tm, wd), lambda i: (i, c))

    def whole(shape):
        return pl.BlockSpec(shape, lambda i: (0, 0))

    return pl.pallas_call(
        _mix_out_kernel,
        grid=(t // tm,),
        in_specs=[rows(aw), rows(d), rows(d, gblk), rows(d, gblk + 1), rows(d),
                  whole((aw, d)), whole((d, d)), whole((d, d)), whole((1, d)), whole((1, d))],
        out_specs=[rows(d), rows(d)],
        out_shape=[jax.ShapeDtypeStruct((t, d), F32), jax.ShapeDtypeStruct((t, d), BF16)],
        compiler_params=_cparams(("parallel",)),
        name="mix_out",
    )(ya, yb, zg, zg, x, wa, wb, wo, g, b)


FFN_FC = 256


def _ffn_kernel(xb_ref, x_ref, wg_ref, wu_ref, wd_ref, g_ref, b_ref, xo_ref, xbo_ref):
    xb = xb_ref[...]
    acc = jnp.zeros(x_ref.shape, F32)
    for c in range(wg_ref.shape[1] // FFN_FC):
        cs = slice(c * FFN_FC, (c + 1) * FFN_FC)
        gt = _dot(xb, wg_ref[:, cs])
        up = _dot(xb, wu_ref[:, cs])
        hh = (gt * jax.nn.sigmoid(gt) * up).astype(BF16)
        acc = acc + _dot(hh, wd_ref[cs, :])
    y = _layer_norm(DN_ALPHA * x_ref[...] + acc, g_ref[...], b_ref[...])
    xo_ref[...] = y
    xbo_ref[...] = y.astype(BF16)


def _ffn(xb, x, wg, wu, wd, g, b, tm=512):
    t, d = x.shape
    rows = pl.BlockSpec((tm, d), lambda i: (i, 0))
    vec = pl.BlockSpec((1, d), lambda i: (0, 0))
    one = pl.Buffered(1)
    return pl.pallas_call(
        _ffn_kernel,
        grid=(t // tm,),
        in_specs=[rows, rows,
                  pl.BlockSpec(wg.shape, lambda i: (0, 0), pipeline_mode=one),
                  pl.BlockSpec(wu.shape, lambda i: (0, 0), pipeline_mode=one),
                  pl.BlockSpec(wd.shape, lambda i: (0, 0), pipeline_mode=one), vec, vec],
        out_specs=[rows, rows],
        out_shape=[jax.ShapeDtypeStruct((t, d), F32), jax.ShapeDtypeStruct((t, d), BF16)],
        compiler_params=_cparams(("parallel",)),
        name="dense_swiglu",
    )(xb, x, wg, wu, wd, g, b)


def _router_kernel(x_ref, rw_ref, tri_ref, gates_ref, rank_ref, selt_ref, cnt_ref):
    x = x_ref[...]
    rw = rw_ref[...]
    x_hi = x.astype(BF16)
    x_lo = (x - x_hi.astype(F32)).astype(BF16)
    w_hi = rw.astype(BF16)
    w_lo = (rw - w_hi.astype(F32)).astype(BF16)
    logits = _dot(x_hi, w_hi) + (_dot(x_hi, w_lo) + _dot(x_lo, w_hi))
    lane = lax.broadcasted_iota(jnp.int32, logits.shape, 1)
    lanef = lane.astype(F32)
    ninf = -jnp.inf
    lg = jnp.where(lane < N_EXPERTS, logits, ninf)
    m1 = jnp.max(lg, axis=1, keepdims=True)
    i1 = jnp.min(jnp.where(lg == m1, lanef, float(LANES)), axis=1, keepdims=True)
    lg2 = jnp.where(lanef == i1, ninf, lg)
    m2 = jnp.max(lg2, axis=1, keepdims=True)
    i2 = jnp.min(jnp.where(lg2 == m2, lanef, float(LANES)), axis=1, keepdims=True)
    e = jnp.exp(m2 - m1)
    w1 = 1.0 / (1.0 + e)
    w2 = e / (1.0 + e)
    gates = jnp.where(lanef == i1, w1, jnp.where(lanef == i2, w2, 0.0))
    gates_ref[...] = gates
    sel = gates > 0.0
    onef = jnp.where(sel, 1.0, 0.0)
    rank = _dot(tri_ref[...], onef.astype(BF16))
    rank_ref[...] = rank
    selt_ref[...] = jnp.where(sel, rank, -1.0).T[0:N_EXPERTS, :]
    cnt_ref[...] = jnp.broadcast_to(jnp.sum(onef, axis=0, keepdims=True), cnt_ref.shape).astype(jnp.int32)


def _router(x, rw_pad, tm):
    t, d = x.shape
    nt = t // tm
    tri = jnp.tril(jnp.ones((tm, tm), BF16), -1)
    rows = pl.BlockSpec((tm, LANES), lambda i: (i, 0))
    return pl.pallas_call(
        _router_kernel,
        grid=(nt,),
        in_specs=[pl.BlockSpec((tm, d), lambda i: (i, 0)), pl.BlockSpec((d, LANES), lambda i: (0, 0)),
                  pl.BlockSpec((tm, tm), lambda i: (0, 0))],
        out_specs=[rows, rows, pl.BlockSpec((None, N_EXPERTS, tm), lambda i: (i, 0, 0)),
                   pl.BlockSpec((None, 8, LANES), lambda i: (i, 0, 0))],
        out_shape=[jax.ShapeDtypeStruct((t, LANES), F32), jax.ShapeDtypeStruct((t, LANES), F32),
                   jax.ShapeDtypeStruct((nt, N_EXPERTS, tm), F32), jax.ShapeDtypeStruct((nt, 8, LANES), jnp.int32)],
        compiler_params=_cparams(("parallel",)),
        name="moe_router",
    )(x, rw_pad, tri)


MOE_TM = 1024
MOE_CAP = 320
MOE_MAIN = 288
MOE_FC = 512


def _expert_kernel(cnt_ref, xb_ref, selt_ref, wg_ref, wu_ref, wd_ref, oc_ref):
    e = pl.program_id(0)
    i = pl.program_id(1)
    tm = xb_ref.shape[0]
    sel_row = selt_ref[pl.ds(e, 1), :]

    def expert_rows(r0, n):
        cidx = lax.broadcasted_iota(jnp.int32, (n, tm), 0).astype(F32) + float(r0)
        onehot = jnp.where(sel_row == cidx, 1.0, 0.0).astype(BF16)
        xc = _dot(onehot, xb_ref[...]).astype(BF16)
        acc = jnp.zeros((n, D_MODEL), F32)
        for c in range(wg_ref.shape[1] // MOE_FC):
            cs = slice(c * MOE_FC, (c + 1) * MOE_FC)
            gt = _dot(xc, wg_ref[:, cs])
            up = _dot(xc, wu_ref[:, cs])
            hh = (gt * jax.nn.sigmoid(gt) * up).astype(BF16)
            acc = acc + _dot(hh, wd_ref[cs, :])
        return acc.astype(oc_ref.dtype)

    oc_ref[0:MOE_MAIN, :] = expert_rows(0, MOE_MAIN)
    many = cnt_ref[i * N_EXPERTS + e] > MOE_MAIN

    @pl.when(many)
    def _():
        oc_ref[MOE_MAIN:, :] = expert_rows(MOE_MAIN, MOE_CAP - MOE_MAIN)

    @pl.when(jnp.logical_not(many))
    def _():
        oc_ref[MOE_MAIN:, :] = jnp.zeros((MOE_CAP - MOE_MAIN, D_MODEL), oc_ref.dtype)


def _experts(cnt, xb, selt, wg, wu, wd):
    t, d = xb.shape
    ne, _, dex = wg.shape
    nt = t // MOE_TM
    one = pl.Buffered(1)
    grid_spec = pltpu.PrefetchScalarGridSpec(
        num_scalar_prefetch=1,
        grid=(ne, nt),
        in_specs=[pl.BlockSpec((MOE_TM, d), lambda e, i, c: (i, 0)),
                  pl.BlockSpec((None, N_EXPERTS, MOE_TM), lambda e, i, c: (i, 0, 0)),
                  pl.BlockSpec((None, d, dex), lambda e, i, c: (e, 0, 0), pipeline_mode=one),
                  pl.BlockSpec((None, d, dex), lambda e, i, c: (e, 0, 0), pipeline_mode=one),
                  pl.BlockSpec((None, dex, d), lambda e, i, c: (e, 0, 0), pipeline_mode=one)],
        out_specs=pl.BlockSpec((None, None, MOE_CAP, d), lambda e, i, c: (e, i, 0, 0)),
    )
    return pl.pallas_call(
        _expert_kernel,
        grid_spec=grid_spec,
        out_shape=jax.ShapeDtypeStruct((ne, nt, MOE_CAP, d), BF16),
        compiler_params=_cparams(("arbitrary", "arbitrary")),
        name="moe_experts",
    )(cnt, xb, selt, wg, wu, wd)


def _combine_kernel(x_ref, gates_ref, rank_ref, oc_ref, g_ref, b_ref, xo_ref):
    tm = x_ref.shape[0]
    gates = gates_ref[...]
    rank = rank_ref[...]
    lane = lax.broadcasted_iota(jnp.int32, (tm, MOE_CAP), 1).astype(F32)
    y = jnp.zeros((tm, D_MODEL), F32)
    for e in range(N_EXPERTS):
        gcol = gates[:, e:e + 1]
        scol = jnp.where(gcol > 0.0, rank[:, e:e + 1], -1.0)
        onehot = jnp.where(scol == lane, 1.0, 0.0).astype(BF16)
        y = y + _dot(onehot, oc_ref[e]) * gcol
    xo_ref[...] = _layer_norm(DN_ALPHA * x_ref[...] + y, g_ref[...], b_ref[...])


def _combine(x, gates, rank, oc, g, b):
    t, d = x.shape
    ne = oc.shape[0]
    rows = pl.BlockSpec((MOE_TM, d), lambda i: (i, 0))
    lrows = pl.BlockSpec((MOE_TM, LANES), lambda i: (i, 0))
    vec = pl.BlockSpec((1, d), lambda i: (0, 0))
    return pl.pallas_call(
        _combine_kernel,
        grid=(t // MOE_TM,),
        in_specs=[rows, lrows, lrows, pl.BlockSpec((ne, None, MOE_CAP, d), lambda i: (0, i, 0, 0)), vec, vec],
        out_specs=rows,
        out_shape=jax.ShapeDtypeStruct((t, d), F32),
        compiler_params=_cparams(("parallel",)),
        name="moe_combine",
    )(x, gates, rank, oc, g, b)


def _moe_kernel(xb_ref, x_ref, gates_ref, wg_ref, wu_ref, wd_ref, g_ref, b_ref, xo_ref, acc):
    e = pl.program_id(1)
    f = pl.program_id(2)

    @pl.when(jnp.logical_and(e == 0, f == 0))
    def _():
        acc[...] = jnp.zeros_like(acc)

    xb = xb_ref[...]
    gate = _col_of(gates_ref[...], e)
    gt = _dot(xb, wg_ref[...])
    up = _dot(xb, wu_ref[...])
    hh = (gt * jax.nn.sigmoid(gt) * up * gate).astype(BF16)
    acc[...] += _dot(hh, wd_ref[...])

    @pl.when(jnp.logical_and(e == pl.num_programs(1) - 1, f == pl.num_programs(2) - 1))
    def _():
        xo_ref[...] = _layer_norm(DN_ALPHA * x_ref[...] + acc[...], g_ref[...], b_ref[...])


def _moe(xb, x, gates, wg, wu, wd, g, b, tm=1024, tf=512):
    t, d = x.shape
    ne, _, dex = wg.shape
    rows = pl.BlockSpec((tm, d), lambda i, e, f: (i, 0))
    vec = pl.BlockSpec((1, d), lambda i, e, f: (0, 0))
    return pl.pallas_call(
        _moe_kernel,
        grid=(t // tm, ne, dex // tf),
        in_specs=[rows, rows, pl.BlockSpec((tm, LANES), lambda i, e, f: (i, 0)),
                  pl.BlockSpec((None, d, tf), lambda i, e, f: (e, 0, f)),
                  pl.BlockSpec((None, d, tf), lambda i, e, f: (e, 0, f)),
                  pl.BlockSpec((None, tf, d), lambda i, e, f: (e, f, 0)), vec, vec],
        out_specs=rows,
        out_shape=jax.ShapeDtypeStruct((t, d), F32),
        scratch_shapes=[pltpu.VMEM((tm, d), F32)],
        compiler_params=_cparams(("parallel", "arbitrary", "arbitrary")),
        name="moe_swiglu",
    )(xb, x, gates, wg, wu, wd, g, b)


def _rope_tables(positions):
    half = ROT_DIM // 2
    inv_freq = ROPE_THETA ** (-jnp.arange(0, ROT_DIM, 2, dtype=F32) / ROT_DIM)
    ang = positions.astype(F32)[..., None] * inv_freq
    cos, sin = jnp.cos(ang), jnp.sin(ang)
    shp = cos.shape[:-1]
    c = jnp.concatenate([cos, cos, jnp.ones(shp + (LANES - ROT_DIM,), F32)], axis=-1)
    s1 = jnp.concatenate([-sin, jnp.zeros(shp + (LANES - half,), F32)], axis=-1)
    s2 = jnp.concatenate([jnp.zeros(shp + (half,), F32), sin, jnp.zeros(shp + (LANES - ROT_DIM,), F32)], axis=-1)
    return c, s1, s2


def kernel(x, positions, w_in, conv_w, a_log, dt_bias, dn_norm_w, w_branch_a, w_branch_b, w_out, ln1_g, ln1_b,
           ffn_w_gate, ffn_w_up, ffn_w_down, router_w, moe_w_gate, moe_w_up, moe_w_down, ln2_g, ln2_b):
    bsz, seq, d = x.shape
    t = bsz * seq
    nh = B_HEADS
    qa_w = 3 * A_QKV_WIDTH
    o_qkvb = qa_w
    o_z = o_qkvb + 3 * B_WIDTH
    o_bd = o_z + B_WIDTH
    o_gates = o_bd + 2 * nh

    tabs = [tb.reshape(t, LANES) for tb in _rope_tables(positions)]

    dn_tt = 1024
    xf = x.reshape(t, d)
    xb = xf
    for layer in range(DEPTH):
        w = w_in[layer]
        w_qkva = w[:, :qa_w].astype(BF16)
        w_b = jnp.concatenate([w[:, o_qkvb:o_bd], w[:, o_gates:]], axis=1).astype(BF16)
        w_bd = jnp.pad(w[:, o_bd:o_gates], ((0, 0), (0, LANES - 2 * nh))).astype(BF16)

        zeros8 = jnp.zeros((nh,), F32)
        arow = jnp.concatenate([zeros8, -jnp.exp(a_log[layer].astype(F32)), jnp.zeros((LANES - 2 * nh,), F32)])[None, :]
        dtrow = jnp.concatenate([zeros8, dt_bias[layer].astype(F32), jnp.zeros((LANES - 2 * nh,), F32)])[None, :]
        subs = _proj_a(xb, w_qkva, tabs, bsz, seq)
        qn, kn, vn, zg, beta, gc, gct, a_p = _proj_b(xb, w_b, w_bd, arow, dtrow, conv_w[layer].astype(F32),
                                                      bsz, seq, PROJ_B_TM)

        ya = _attention([subs[p * N_GROUPS:(p + 1) * N_GROUPS] for p in range(3)], bsz, seq)

        sh3 = (bsz, seq, -1)
        t_p = _tri_solve(a_p.reshape((-1,) + a_p.shape[2:]))
        yb = _dn_main(qn.reshape(sh3), kn.reshape(sh3), vn.reshape(sh3), beta.reshape(sh3), gc.reshape(sh3), gct,
                      t_p, zg.reshape(sh3), 0, dn_norm_w[layer].astype(F32)[None, :], dn_tt)

        xf, xb = _mix_out(ya.reshape(t, A_GROUP_WIDTH), yb.reshape(t, B_WIDTH), zg, B_WIDTH, xf,
                          w_branch_a[layer].astype(BF16), w_branch_b[layer].astype(BF16), w_out[layer].astype(BF16),
                          ln1_g[layer].astype(F32)[None, :], ln1_b[layer].astype(F32)[None, :])

        g2 = ln2_g[layer].astype(F32)[None, :]
        b2 = ln2_b[layer].astype(F32)[None, :]
        if layer % 2 == 0:
            i = layer // 2
            xf, xb = _ffn(xb, xf, ffn_w_gate[i].astype(BF16), ffn_w_up[i].astype(BF16), ffn_w_down[i].astype(BF16), g2, b2)
        else:
            i = layer // 2
            rw = jnp.pad(router_w[i].astype(F32), ((0, 0), (0, LANES - N_EXPERTS)))
            gates, rank, selt, cnt = _router(xf, rw, MOE_TM)
            wg = moe_w_gate[i].astype(BF16)
            wu = moe_w_up[i].astype(BF16)
            wd = moe_w_down[i].astype(BF16)

            cnt_flat = cnt[:, 0, :N_EXPERTS].reshape(-1)

            def routed(xb, xf, gates, rank, selt, cnt_flat):
                return _combine(xf, gates, rank, _experts(cnt_flat, xb, selt, wg, wu, wd), g2, b2)

            def dense(xb, xf, gates, rank, selt, cnt_flat):
                return _moe(xb, xf, gates, wg, wu, wd, g2, b2)

            xf = lax.cond(jnp.max(cnt_flat) > MOE_CAP, dense, routed, xb, xf, gates, rank, selt, cnt_flat)
    return xf.reshape(bsz, seq, d)
```

```python
import functools
import math

import jax
import jax.numpy as jnp
from jax import lax
from jax.experimental import pallas as pl
from jax.experimental.pallas import tpu as pltpu

F32 = jnp.float32
BF16 = jnp.bfloat16

D_MODEL = 1024
DEPTH = 2
A_PAIRS = ((128, 1), (512, 4), (2048, 16))
A_HEADS = 4
HEAD_DIM = 128
A_GROUP_WIDTH = A_HEADS * HEAD_DIM
A_QKV_WIDTH = len(A_PAIRS) * A_GROUP_WIDTH
A_BLOCK = 128
ROPE_THETA = 500000.0
ROT_DIM = HEAD_DIM // 4
B_HEADS = 8
B_WIDTH = B_HEADS * HEAD_DIM
CONV_K = 4
CHUNK = 64
N_EXPERTS = 8
DN_ALPHA = (2 * DEPTH) ** 0.25
LN_EPS = 1e-5
RMS_EPS = 1e-6
NEG = -1e30
LANES = 128
VMEM_LIMIT = 56 * 1024 * 1024


def _cparams(sem):
    return pltpu.CompilerParams(dimension_semantics=sem, vmem_limit_bytes=VMEM_LIMIT)


def _dot(a, b):
    return jnp.dot(a, b, preferred_element_type=F32)


def _dot_nt(a, b):
    return lax.dot_general(a, b, (((1,), (1,)), ((), ())), preferred_element_type=F32)


def _dot_tn(a, b):
    return lax.dot_general(a, b, (((0,), (0,)), ((), ())), preferred_element_type=F32)


def _col_of(x, idx):
    lane = lax.broadcasted_iota(jnp.int32, x.shape, 1)
    return jnp.sum(jnp.where(lane == idx, x, 0.0), axis=1, keepdims=True)


def _layer_norm(v, g, b):
    mu = jnp.mean(v, axis=-1, keepdims=True)
    c = v - mu
    var = jnp.mean(c * c, axis=-1, keepdims=True)
    return c * lax.rsqrt(var + LN_EPS) * g + b


def _rope(t, c, s1, s2):
    return t * c + pltpu.roll(t, LANES - ROT_DIM // 2, 1) * s1 + pltpu.roll(t, ROT_DIM // 2, 1) * s2


PB_FC = 1024
HALO = 16
DN_HG = 4


def _proj_b_kernel(npb, x_ref, w_ref, wbd_ref, arow_ref, dtrow_ref, cw_ref,
                   qo_ref, ko_ref, vo_ref, zg_ref, beta_ref, gc_ref, gct_ref, a_ref, scr, carry):
    tm = x_ref.shape[0]
    first = pl.program_id(0) % npb == 0
    x = x_ref[...].astype(BF16)

    @pl.when(pl.program_id(0) == 0)
    def _():
        carry[...] = jnp.zeros_like(carry)

    bd = _dot(x, wbd_ref[...])
    beta = jax.nn.sigmoid(bd)
    beta_ref[...] = beta
    y = bd + dtrow_ref[...]
    softplus = jnp.maximum(y, 0.0) + jnp.log(1.0 + jnp.exp(-jnp.abs(y)))
    g = arow_ref[...] * softplus
    ri = lax.broadcasted_iota(jnp.int32, (tm, tm), 0)
    ci = lax.broadcasted_iota(jnp.int32, (tm, tm), 1)
    tri = jnp.where(jnp.logical_and(ri // CHUNK == ci // CHUNK, ci <= ri), 1.0, 0.0).astype(BF16)
    g_hi = g.astype(BF16)
    r1 = g - g_hi.astype(F32)
    g_mid = r1.astype(BF16)
    g_lo = (r1 - g_mid.astype(F32)).astype(BF16)
    parts = _dot(tri, jnp.concatenate([g_hi, g_mid, g_lo], axis=1))
    gc = parts[:, 0:LANES] + (parts[:, LANES:2 * LANES] + parts[:, 2 * LANES:])
    gc_ref[...] = gc
    gct = gc.T
    gct_ref[...] = gct[B_HEADS:2 * B_HEADS, :]

    def conv_silu(col, raw):
        scr[col, 0:HALO, :] = jnp.where(first, 0.0, carry[col])
        scr[col, HALO:, :] = raw
        carry[col] = raw[tm - HALO:, :]
        w = cw_ref[:, col * HEAD_DIM:(col + 1) * HEAD_DIM]
        yy = scr[col, HALO - 3:HALO - 3 + tm, :] * w[0:1, :]
        for j in range(1, CONV_K):
            yy = yy + scr[col, HALO - 3 + j:HALO - 3 + j + tm, :] * w[j:j + 1, :]
        return yy * jax.nn.sigmoid(yy)

    def l2n(v):
        return v * lax.rsqrt(jnp.sum(v * v, axis=1, keepdims=True) + RMS_EPS)

    g2 = 2 * CHUNK
    r2 = lax.broadcasted_iota(jnp.int32, (g2, g2), 0)
    c2 = lax.broadcasted_iota(jnp.int32, (g2, g2), 1)
    strict = jnp.logical_and((r2 >= CHUNK) == (c2 >= CHUNK), r2 > c2)
    lane = lax.broadcasted_iota(jnp.int32, (CHUNK, g2), 1)
    tps = tm // g2

    def plain_chunk(c):
        zg_ref[:, (c - 3) * PB_FC:(c - 2) * PB_FC] = _dot(x, w_ref[:, c * PB_FC:(c + 1) * PB_FC]).astype(zg_ref.dtype)

    acc = _dot(x, w_ref[:, 0:PB_FC])
    plain_chunk(3)
    for h in range(B_HEADS):
        hs = slice(h * HEAD_DIM, (h + 1) * HEAD_DIM)
        qo_ref[:, hs] = (l2n(conv_silu(h, acc[:, hs])) * (HEAD_DIM ** -0.5)).astype(qo_ref.dtype)
    acc = _dot(x, w_ref[:, 2 * PB_FC:3 * PB_FC])
    plain_chunk(4)
    for h in range(B_HEADS):
        hs = slice(h * HEAD_DIM, (h + 1) * HEAD_DIM)
        vo_ref[:, hs] = conv_silu(2 * B_HEADS + h, acc[:, hs]).astype(vo_ref.dtype)
    acc = _dot(x, w_ref[:, PB_FC:2 * PB_FC])
    plain_chunk(5)
    for h in range(B_HEADS):
        hs = slice(h * HEAD_DIM, (h + 1) * HEAD_DIM)
        k = l2n(conv_silu(B_HEADS + h, acc[:, hs])).astype(BF16)
        ko_ref[:, hs] = k
        bcol = beta[:, h:h + 1]
        gcol = gc[:, B_HEADS + h:B_HEADS + h + 1]
        grow = gct[B_HEADS + h:B_HEADS + h + 1, :]
        for gi in range(tps):
            rs = slice(gi * g2, (gi + 1) * g2)
            kb = k[rs, :]
            kbeta = (kb.astype(F32) * bcol[rs, :]).astype(BF16)
            diff = gcol[rs, :] - grow[:, rs]
            a = _dot_nt(kbeta, kb) * jnp.exp(jnp.where(strict, diff, NEG))
            a_ref[h // DN_HG, :, (h % DN_HG) * tps + gi, :] = jnp.where(lane < CHUNK, a[0:CHUNK, :], a[CHUNK:g2, :])


def _proj_b(xb, w, w_bd, arow, dtrow, conv_w, bsz, seq, tm):
    t, k = xb.shape
    n = w.shape[1]
    npb = seq // tm
    tps = tm // (2 * CHUNK)
    slots = DN_HG * tps
    one = pl.Buffered(1)
    rows = pl.BlockSpec((tm, B_WIDTH), lambda i: (i, 0))
    lrow = pl.BlockSpec((tm, LANES), lambda i: (i, 0))
    rspec = pl.BlockSpec((1, LANES), lambda i: (0, 0))
    ng = B_HEADS // DN_HG
    return pl.pallas_call(
        functools.partial(_proj_b_kernel, npb),
        grid=(t // tm,),
        in_specs=[pl.BlockSpec((tm, k), lambda i: (i, 0)),
                  pl.BlockSpec(w.shape, lambda i: (0, 0), pipeline_mode=one),
                  pl.BlockSpec(w_bd.shape, lambda i: (0, 0), pipeline_mode=one), rspec, rspec,
                  pl.BlockSpec(conv_w.shape, lambda i: (0, 0))],
        out_specs=[rows, rows, rows, pl.BlockSpec((tm, n - 3 * B_WIDTH), lambda i: (i, 0)), lrow, lrow,
                   pl.BlockSpec((B_HEADS, tm), lambda i: (0, i)),
                   pl.BlockSpec((None, ng, CHUNK, slots, 2 * CHUNK), lambda i: (i // npb, 0, 0, i % npb, 0))],
        out_shape=[jax.ShapeDtypeStruct((t, B_WIDTH), BF16)] * 3
        + [jax.ShapeDtypeStruct((t, n - 3 * B_WIDTH), BF16)]
        + [jax.ShapeDtypeStruct((t, LANES), F32)] * 2
        + [jax.ShapeDtypeStruct((B_HEADS, t), F32),
           jax.ShapeDtypeStruct((bsz, ng, CHUNK, slots * npb, 2 * CHUNK), F32)],
        scratch_shapes=[pltpu.VMEM((3 * B_HEADS, tm + HALO, HEAD_DIM), F32),
                        pltpu.VMEM((3 * B_HEADS, HALO, HEAD_DIM), F32)],
        compiler_params=_cparams(("arbitrary",)),
        name="proj_mixer_b",
    )(xb, w, w_bd, arow, dtrow, conv_w)


N_GROUPS = len(A_PAIRS)
DILS = tuple(d for _, d in A_PAIRS)


def _proj_a_kernel(x_ref, w_ref, c_ref, s1_ref, s2_ref, *refs):
    outs = refs[:3 * N_GROUPS]
    scr = refs[3 * N_GROUPS]
    tm = x_ref.shape[0]
    w = A_GROUP_WIDTH
    scale = 1.0 / math.sqrt(HEAD_DIM)
    x = x_ref[...].astype(BF16)
    for jj in range(3 * N_GROUPS):
        p, g = divmod(jj, N_GROUPS)
        dil = DILS[g]
        o_ref = outs[jj]
        acc = _dot(x, w_ref[:, jj * w:(jj + 1) * w])
        for h in range(A_HEADS):
            sl = slice(h * HEAD_DIM, (h + 1) * HEAD_DIM)
            t = acc[:, sl]
            if p < 2:
                t = _rope(t, c_ref[...], s1_ref[...], s2_ref[...])
            if p == 0:
                t = t * scale
            if dil == 1:
                o_ref[:, sl] = t.astype(o_ref.dtype)
            else:
                slot = (jj % 2) * A_HEADS + h
                scr[slot] = t
                for r in range(dil):
                    o_ref[r, :, sl] = scr[slot, pl.ds(r, tm // dil, stride=dil), :].astype(o_ref.dtype)


def _proj_a(xb, w_qkva, tabs, bsz, seq, tm=512):
    t, k = xb.shape
    npb = seq // tm
    w = A_GROUP_WIDTH
    out_specs, out_shape = [], []
    for jj in range(3 * N_GROUPS):
        dil = DILS[jj % N_GROUPS]
        if dil == 1:
            out_specs.append(pl.BlockSpec((None, None, tm, w), lambda i: (i // npb, 0, i % npb, 0)))
        else:
            out_specs.append(pl.BlockSpec((None, dil, tm // dil, w), lambda i: (i // npb, 0, i % npb, 0)))
        out_shape.append(jax.ShapeDtypeStruct((bsz, dil, seq // dil, w), BF16))
    tab = pl.BlockSpec((tm, LANES), lambda i: (i, 0))
    return pl.pallas_call(
        _proj_a_kernel,
        grid=(t // tm,),
        in_specs=[pl.BlockSpec((tm, k), lambda i: (i, 0)),
                  pl.BlockSpec(w_qkva.shape, lambda i: (0, 0), pipeline_mode=pl.Buffered(1)), tab, tab, tab],
        out_specs=out_specs,
        out_shape=out_shape,
        scratch_shapes=[pltpu.VMEM((2 * A_HEADS, tm, HEAD_DIM), F32)],
        compiler_params=_cparams(("parallel",)),
        name="proj_mixer_a",
    )(xb, w_qkva, *tabs)


ATT_TT = 2048
ATT_UNROLL = 16


def _attn_kernel(*refs):
    ins = refs[:5 * N_GROUPS]
    o_ref = refs[5 * N_GROUPS]
    od, ld, on, ln = refs[5 * N_GROUPS + 1:5 * N_GROUPS + 5]
    kvbufs = refs[5 * N_GROUPS + 5:]
    it = pl.program_id(1)
    row = lax.broadcasted_iota(jnp.int32, (A_BLOCK, 2 * A_BLOCK), 0)
    col = lax.broadcasted_iota(jnp.int32, (A_BLOCK, 2 * A_BLOCK), 1)
    band = jnp.logical_and(col >= row, col <= row + A_BLOCK)
    bias_all = jnp.where(band, 0.0, NEG).astype(F32)
    bias_own = jnp.where(jnp.logical_and(band, col >= A_BLOCK), 0.0, NEG).astype(F32)
    nblocks = ATT_TT // A_BLOCK
    for g, dil in enumerate(DILS):
        q_ref, k_ref, v_ref, kh_ref, vh_ref = ins[5 * g:5 * g + 5]
        kbuf, vbuf = kvbufs[2 * g:2 * g + 2]
        nlb = nblocks // dil
        rows = ATT_TT // dil
        kbuf[:, 0:A_BLOCK, :] = kh_ref[...]
        kbuf[:, A_BLOCK:, :] = k_ref[...]
        vbuf[:, 0:A_BLOCK, :] = vh_ref[...]
        vbuf[:, A_BLOCK:, :] = v_ref[...]

        def block(c, carry, g=g, nlb=nlb, rows=rows, q_ref=q_ref, kbuf=kbuf, vbuf=vbuf):
            r = c // nlb
            nb = c % nlb
            off = pl.multiple_of(nb * A_BLOCK, A_BLOCK)
            q = q_ref[r, pl.ds(off, A_BLOCK), :]
            kk = kbuf[r, pl.ds(off, 2 * A_BLOCK), :]
            vv = vbuf[r, pl.ds(off, 2 * A_BLOCK), :]
            has_prev = jnp.logical_or(nb > 0, it > 0)
            s = _dot_nt(q, kk) + jnp.where(has_prev, bias_all, bias_own)
            m = jnp.max(s, axis=1, keepdims=True)
            p = jnp.exp(s - m)
            den = jnp.sum(p, axis=1, keepdims=True)
            o = _dot(p.astype(BF16), vv) / den
            dst = pl.multiple_of(r * rows + off, A_BLOCK)
            od[g, pl.ds(dst, A_BLOCK), :] = o
            ld[g, pl.ds(dst, A_BLOCK), :] = jnp.broadcast_to(m + jnp.log(den), (A_BLOCK, HEAD_DIM))
            return carry

        lax.fori_loop(0, nblocks, block, 0, unroll=ATT_UNROLL)
    for g, dil in enumerate(DILS):
        if dil == 1:
            continue
        rows = ATT_TT // dil
        for r in range(dil):
            on[g - 1, pl.ds(r, rows, stride=dil), :] = od[g, r * rows:(r + 1) * rows, :]
            ln[g - 1, pl.ds(r, rows, stride=dil), :] = ld[g, r * rows:(r + 1) * rows, :]
    step = 256
    for c in range(ATT_TT // step):
        sl = slice(c * step, (c + 1) * step)
        lse = [ld[0, sl, :]] + [ln[g - 1, sl, :] for g in range(1, N_GROUPS)]
        outs = [od[0, sl, :]] + [on[g - 1, sl, :] for g in range(1, N_GROUPS)]
        m = functools.reduce(jnp.maximum, lse)
        es = [jnp.exp(l - m) for l in lse]
        num = functools.reduce(lambda a, b: a + b, [e * o for e, o in zip(es, outs)])
        o_ref[sl, :] = (num / functools.reduce(lambda a, b: a + b, es)).astype(o_ref.dtype)


def _attention(qkv_sub, bsz, seq):
    specs, args = [], []
    for g, dil in enumerate(DILS):
        rows = ATT_TT // dil
        hb = rows // A_BLOCK
        cur = pl.BlockSpec((None, dil, rows, HEAD_DIM), lambda b, i, h: (b, 0, i, h))
        halo = pl.BlockSpec((None, dil, A_BLOCK, HEAD_DIM), lambda b, i, h, hb=hb: (b, 0, jnp.maximum(i * hb - 1, 0), h))
        specs += [cur, cur, cur, halo, halo]
        args += [qkv_sub[0][g], qkv_sub[1][g], qkv_sub[2][g], qkv_sub[1][g], qkv_sub[2][g]]
    return pl.pallas_call(
        _attn_kernel,
        grid=(bsz, seq // ATT_TT, A_HEADS),
        in_specs=specs,
        out_specs=pl.BlockSpec((None, ATT_TT, HEAD_DIM), lambda b, i, h: (b, i, h)),
        out_shape=jax.ShapeDtypeStruct((bsz, seq, A_GROUP_WIDTH), BF16),
        scratch_shapes=[pltpu.VMEM((N_GROUPS, ATT_TT, HEAD_DIM), F32)] * 2
        + [pltpu.VMEM((N_GROUPS - 1, ATT_TT, HEAD_DIM), F32)] * 2
        + [pltpu.VMEM((dil, ATT_TT // dil + A_BLOCK, HEAD_DIM), BF16) for dil in DILS for _ in range(2)],
        compiler_params=_cparams(("parallel", "parallel", "parallel")),
        name="dilated_attention",
    )(*args)


SOLVE_TILES = 128
SUB = 8


def _tri_solve_kernel(a_ref, t_ref, at, tt):
    def load_row(i, c):
        at[i] = a_ref[i].T
        return c

    lax.fori_loop(0, CHUNK, load_row, 0, unroll=4)
    tt[...] = jnp.zeros_like(tt)
    rowid = lax.broadcasted_iota(jnp.int32, (SUB, SOLVE_TILES), 0)
    nk = CHUNK // SUB

    def solve_row(i, c):
        acc = tuple(jnp.zeros((SUB, SOLVE_TILES), F32) for _ in range(2 * nk))
        for mb in range(nk):
            def apply_block(acc, mb=mb):
                new = list(acc)
                for m in range(mb * SUB, (mb + 1) * SUB):
                    a0 = jnp.broadcast_to(at[i, m:m + 1, :], (SUB, SOLVE_TILES))
                    a1 = jnp.broadcast_to(at[i, CHUNK + m:CHUNK + m + 1, :], (SUB, SOLVE_TILES))
                    for k in range(mb + 1):
                        new[k] = new[k] - a0 * tt[m, k * SUB:(k + 1) * SUB, :]
                        new[nk + k] = new[nk + k] - a1 * tt[m, CHUNK + k * SUB:CHUNK + (k + 1) * SUB, :]
                return tuple(new)

            acc = lax.cond(mb * SUB < i, apply_block, lambda a: a, acc)
        for k in range(nk):
            diag = jnp.where(rowid + k * SUB == i, 1.0, 0.0).astype(F32)
            tt[i, k * SUB:(k + 1) * SUB, :] = acc[k] + diag
            tt[i, CHUNK + k * SUB:CHUNK + (k + 1) * SUB, :] = acc[nk + k] + diag
        return c

    lax.fori_loop(0, CHUNK, solve_row, 0)

    def store_row(i, c):
        t_ref[i] = tt[i].T
        return c

    lax.fori_loop(0, CHUNK, store_row, 0, unroll=4)


def _tri_solve(a_p):
    nb, _, slots, _ = a_p.shape
    assert slots == SOLVE_TILES
    spec = pl.BlockSpec((None, CHUNK, SOLVE_TILES, 2 * CHUNK), lambda n: (n, 0, 0, 0))
    return pl.pallas_call(
        _tri_solve_kernel,
        grid=(nb,),
        in_specs=[spec],
        out_specs=spec,
        out_shape=jax.ShapeDtypeStruct(a_p.shape, F32),
        scratch_shapes=[pltpu.VMEM((CHUNK, 2 * CHUNK, SOLVE_TILES), F32)] * 2,
        compiler_params=_cparams(("parallel",)),
        name="deltanet_tri_solve",
    )(a_p)


DN_HP = DN_HG
PROJ_B_TM = 512
PROD_TPS = PROJ_B_TM // (2 * CHUNK)


def _dn_main_kernel(q_ref, k_ref, v_ref, beta_ref, gc_ref, gct_ref, t_ref, z_ref, nw_ref, o_ref, state):
    t = pl.program_id(2)
    hh = pl.program_id(1)
    tt = q_ref.shape[0]
    g2 = 2 * CHUNK

    @pl.when(t == 0)
    def _():
        state[...] = jnp.zeros_like(state)

    ri = lax.broadcasted_iota(jnp.int32, (g2, g2), 0)
    ci = lax.broadcasted_iota(jnp.int32, (g2, g2), 1)
    incl = jnp.logical_and((ri >= CHUNK) == (ci >= CHUNK), ri >= ci)
    lane = lax.broadcasted_iota(jnp.int32, (CHUNK, g2), 1)
    nw = nw_ref[...]
    beta = beta_ref[...]
    gc = gc_ref[...]
    bcols, gcols, grows, states = [], [], [], []
    for hp in range(DN_HP):
        h = hh * DN_HP + hp
        bcols.append(_col_of(beta, h))
        gcols.append(_col_of(gc, B_HEADS + h))
        grows.append(gct_ref[pl.ds(h, 1), :])
        states.append(state[hp])
    for gi in range(tt // g2):
        rs = slice(gi * g2, (gi + 1) * g2)
        pre = []
        for hp in range(DN_HP):
            hs = slice(hp * HEAD_DIM, (hp + 1) * HEAD_DIM)
            qb = q_ref[rs, hs]
            kb = k_ref[rs, hs]
            kf = kb.astype(F32)
            b = bcols[hp][rs, :]
            gcl = gcols[hp][rs, :]
            eg = jnp.exp(gcl)
            kbeta = kf * b
            rhs = jnp.concatenate([v_ref[rs, hs].astype(F32) * b, kbeta * eg], axis=1).astype(BF16)
            tp = t_ref[:, (gi // PROD_TPS) * DN_HP * PROD_TPS + hp * PROD_TPS + gi % PROD_TPS, :]
            tbd = jnp.concatenate([jnp.where(lane < CHUNK, tp, 0.0), jnp.where(lane >= CHUNK, tp, 0.0)], axis=0)
            uw = _dot(tbd.astype(BF16), rhs).astype(BF16)
            diff = gcl - grows[hp][:, rs]
            attn = (_dot_nt(qb, kb) * jnp.exp(jnp.where(incl, diff, NEG))).astype(BF16)
            auw = _dot(attn, uw)
            qeff = (qb.astype(F32) * eg - auw[:, HEAD_DIM:]).astype(BF16)
            per_chunk = []
            for cc in range(2):
                cs = slice(cc * CHUNK, (cc + 1) * CHUNK)
                glast = gcl[cc * CHUNK + CHUNK - 1:cc * CHUNK + CHUNK, :]
                ktail = (kf[cs, :] * jnp.exp(glast - gcl[cs, :])).astype(BF16)
                kuw = _dot_tn(ktail, uw[cs, :])
                lhs = jnp.concatenate([kuw[:, HEAD_DIM:].astype(BF16), qeff[cs, :]], axis=0)
                per_chunk.append((lhs, kuw[:, :HEAD_DIM], jnp.exp(glast), auw[cs, :HEAD_DIM]))
            pre.append(per_chunk)
        outs = [[] for _ in range(DN_HP)]
        for cc in range(2):
            for hp in range(DN_HP):
                lhs, ku, decay, au = pre[hp][cc]
                s = states[hp]
                xs = _dot(lhs, s.astype(BF16))
                outs[hp].append(xs[HEAD_DIM:, :] + au)
                states[hp] = s * decay + ku - xs[:HEAD_DIM, :]
        for hp in range(DN_HP):
            hs = slice(hp * HEAD_DIM, (hp + 1) * HEAD_DIM)
            o = jnp.concatenate(outs[hp], axis=0)
            o = o * lax.rsqrt(jnp.mean(o * o, axis=1, keepdims=True) + RMS_EPS) * nw
            z = z_ref[rs, hs].astype(F32)
            o_ref[rs, hs] = (o * (z * jax.nn.sigmoid(z))).astype(o_ref.dtype)
    for hp in range(DN_HP):
        state[hp] = states[hp]


def _dn_main(qn, kn, vn, beta, gc, gct, t_p, zg, z_off, norm_w, tt):
    bsz, seq, _ = qn.shape
    nh = B_HEADS
    hw = DN_HP * HEAD_DIM
    tok = pl.BlockSpec((None, tt, hw), lambda b, h, t: (b, t, h))
    full = pl.BlockSpec((None, tt, LANES), lambda b, h, t: (b, t, 0))
    nst = seq // tt
    gts = pl.BlockSpec((nh, tt), lambda b, h, t: (0, b * nst + t))
    tspec = pl.BlockSpec((None, CHUNK, DN_HP * tt // (2 * CHUNK), 2 * CHUNK),
                         lambda b, h, t: (b * (nh // DN_HP) + h, 0, t, 0))
    nspec = pl.BlockSpec((1, HEAD_DIM), lambda b, h, t: (0, 0))
    return pl.pallas_call(
        _dn_main_kernel,
        grid=(bsz, nh // DN_HP, seq // tt),
        in_specs=[tok, tok, tok, full, full, gts, tspec,
                  pl.BlockSpec((None, tt, hw), lambda b, h, t: (b, t, z_off // hw + h)), nspec],
        out_specs=tok,
        out_shape=jax.ShapeDtypeStruct((bsz, seq, B_WIDTH), BF16),
        scratch_shapes=[pltpu.VMEM((DN_HP, HEAD_DIM, HEAD_DIM), F32)],
        compiler_params=_cparams(("parallel", "parallel", "arbitrary")),
        name="deltanet_main",
    )(qn, kn, vn, beta, gc, gct, t_p, zg, norm_w)


def _mix_out_kernel(ya_ref, yb_ref, ga_ref, gb_ref, x_ref,
                    wa_ref, wb_ref, wo_ref, g_ref, b_ref, xo_ref, xbo_ref):
    ma = _dot(ya_ref[...], wa_ref[...])
    mb = _dot(yb_ref[...], wb_ref[...])
    merged = jax.nn.sigmoid(ga_ref[...].astype(F32)) * ma + jax.nn.sigmoid(gb_ref[...].astype(F32)) * mb
    r = _dot(merged.astype(BF16), wo_ref[...])
    y = _layer_norm(DN_ALPHA * x_ref[...] + r, g_ref[...], b_ref[...])
    xo_ref[...] = y
    xbo_ref[...] = y.astype(BF16)


def _mix_out(ya, yb, zg, gate_off, x, wa, wb, wo, g, b, tm=512):
    t = x.shape[0]
    d = D_MODEL
    aw = A_GROUP_WIDTH
    gblk = gate_off // d

    def rows(wd, c=0):
        return pl.BlockSpec((tm, wd), lambda i: (i, c))

    def whole(shape):
        return pl.BlockSpec(shape, lambda i: (0, 0))

    return pl.pallas_call(
        _mix_out_kernel,
        grid=(t // tm,),
        in_specs=[rows(aw), rows(d), rows(d, gblk), rows(d, gblk + 1), rows(d),
                  whole((aw, d)), whole((d, d)), whole((d, d)), whole((1, d)), whole((1, d))],
        out_specs=[rows(d), rows(d)],
        out_shape=[jax.ShapeDtypeStruct((t, d), F32), jax.ShapeDtypeStruct((t, d), BF16)],
        compiler_params=_cparams(("parallel",)),
        name="mix_out",
    )(ya, yb, zg, zg, x, wa, wb, wo, g, b)


FFN_FC = 256


def _ffn_kernel(xb_ref, x_ref, wg_ref, wu_ref, wd_ref, g_ref, b_ref, xo_ref, xbo_ref):
    xb = xb_ref[...]
    acc = jnp.zeros(x_ref.shape, F32)
    for c in range(wg_ref.shape[1] // FFN_FC):
        cs = slice(c * FFN_FC, (c + 1) * FFN_FC)
        gt = _dot(xb, wg_ref[:, cs])
        up = _dot(xb, wu_ref[:, cs])
        hh = (gt * jax.nn.sigmoid(gt) * up).astype(BF16)
        acc = acc + _dot(hh, wd_ref[cs, :])
    y = _layer_norm(DN_ALPHA * x_ref[...] + acc, g_ref[...], b_ref[...])
    xo_ref[...] = y
    xbo_ref[...] = y.astype(BF16)


def _ffn(xb, x, wg, wu, wd, g, b, tm=512):
    t, d = x.shape
    rows = pl.BlockSpec((tm, d), lambda i: (i, 0))
    vec = pl.BlockSpec((1, d), lambda i: (0, 0))
    one = pl.Buffered(1)
    return pl.pallas_call(
        _ffn_kernel,
        grid=(t // tm,),
        in_specs=[rows, rows,
                  pl.BlockSpec(wg.shape, lambda i: (0, 0), pipeline_mode=one),
                  pl.BlockSpec(wu.shape, lambda i: (0, 0), pipeline_mode=one),
                  pl.BlockSpec(wd.shape, lambda i: (0, 0), pipeline_mode=one), vec, vec],
        out_specs=[rows, rows],
        out_shape=[jax.ShapeDtypeStruct((t, d), F32), jax.ShapeDtypeStruct((t, d), BF16)],
        compiler_params=_cparams(("parallel",)),
        name="dense_swiglu",
    )(xb, x, wg, wu, wd, g, b)


def _router_kernel(x_ref, rw_ref, tri_ref, gates_ref, rank_ref, selt_ref, cnt_ref):
    x = x_ref[...]
    rw = rw_ref[...]
    x_hi = x.astype(BF16)
    x_lo = (x - x_hi.astype(F32)).astype(BF16)
    w_hi = rw.astype(BF16)
    w_lo = (rw - w_hi.astype(F32)).astype(BF16)
    logits = _dot(x_hi, w_hi) + (_dot(x_hi, w_lo) + _dot(x_lo, w_hi))
    lane = lax.broadcasted_iota(jnp.int32, logits.shape, 1)
    lanef = lane.astype(F32)
    ninf = -jnp.inf
    lg = jnp.where(lane < N_EXPERTS, logits, ninf)
    m1 = jnp.max(lg, axis=1, keepdims=True)
    i1 = jnp.min(jnp.where(lg == m1, lanef, float(LANES)), axis=1, keepdims=True)
    lg2 = jnp.where(lanef == i1, ninf, lg)
    m2 = jnp.max(lg2, axis=1, keepdims=True)
    i2 = jnp.min(jnp.where(lg2 == m2, lanef, float(LANES)), axis=1, keepdims=True)
    e = jnp.exp(m2 - m1)
    w1 = 1.0 / (1.0 + e)
    w2 = e / (1.0 + e)
    gates = jnp.where(lanef == i1, w1, jnp.where(lanef == i2, w2, 0.0))
    gates_ref[...] = gates
    sel = gates > 0.0
    onef = jnp.where(sel, 1.0, 0.0)
    rank = _dot(tri_ref[...], onef.astype(BF16))
    rank_ref[...] = rank
    selt_ref[...] = jnp.where(sel, rank, -1.0).T[0:N_EXPERTS, :]
    cnt_ref[...] = jnp.broadcast_to(jnp.sum(onef, axis=0, keepdims=True), cnt_ref.shape).astype(jnp.int32)


def _router(x, rw_pad, tm):
    t, d = x.shape
    nt = t // tm
    tri = jnp.tril(jnp.ones((tm, tm), BF16), -1)
    rows = pl.BlockSpec((tm, LANES), lambda i: (i, 0))
    return pl.pallas_call(
        _router_kernel,
        grid=(nt,),
        in_specs=[pl.BlockSpec((tm, d), lambda i: (i, 0)), pl.BlockSpec((d, LANES), lambda i: (0, 0)),
                  pl.BlockSpec((tm, tm), lambda i: (0, 0))],
        out_specs=[rows, rows, pl.BlockSpec((None, N_EXPERTS, tm), lambda i: (i, 0, 0)),
                   pl.BlockSpec((None, 8, LANES), lambda i: (i, 0, 0))],
        out_shape=[jax.ShapeDtypeStruct((t, LANES), F32), jax.ShapeDtypeStruct((t, LANES), F32),
                   jax.ShapeDtypeStruct((nt, N_EXPERTS, tm), F32), jax.ShapeDtypeStruct((nt, 8, LANES), jnp.int32)],
        compiler_params=_cparams(("parallel",)),
        name="moe_router",
    )(x, rw_pad, tri)


MOE_TM = 1024
MOE_CAP = 320
MOE_MAIN = 288
MOE_FC = 512


def _expert_kernel(cnt_ref, xb_ref, selt_ref, wg_ref, wu_ref, wd_ref, oc_ref):
    e = pl.program_id(0)
    i = pl.program_id(1)
    tm = xb_ref.shape[0]
    sel_row = selt_ref[pl.ds(e, 1), :]

    def expert_rows(r0, n):
        cidx = lax.broadcasted_iota(jnp.int32, (n, tm), 0).astype(F32) + float(r0)
        onehot = jnp.where(sel_row == cidx, 1.0, 0.0).astype(BF16)
        xc = _dot(onehot, xb_ref[...]).astype(BF16)
        acc = jnp.zeros((n, D_MODEL), F32)
        for c in range(wg_ref.shape[1] // MOE_FC):
            cs = slice(c * MOE_FC, (c + 1) * MOE_FC)
            gt = _dot(xc, wg_ref[:, cs])
            up = _dot(xc, wu_ref[:, cs])
            hh = (gt * jax.nn.sigmoid(gt) * up).astype(BF16)
            acc = acc + _dot(hh, wd_ref[cs, :])
        return acc.astype(oc_ref.dtype)

    oc_ref[0:MOE_MAIN, :] = expert_rows(0, MOE_MAIN)
    many = cnt_ref[i * N_EXPERTS + e] > MOE_MAIN

    @pl.when(many)
    def _():
        oc_ref[MOE_MAIN:, :] = expert_rows(MOE_MAIN, MOE_CAP - MOE_MAIN)

    @pl.when(jnp.logical_not(many))
    def _():
        oc_ref[MOE_MAIN:, :] = jnp.zeros((MOE_CAP - MOE_MAIN, D_MODEL), oc_ref.dtype)


def _experts(cnt, xb, selt, wg, wu, wd):
    t, d = xb.shape
    ne, _, dex = wg.shape
    nt = t // MOE_TM
    one = pl.Buffered(1)
    grid_spec = pltpu.PrefetchScalarGridSpec(
        num_scalar_prefetch=1,
        grid=(ne, nt),
        in_specs=[pl.BlockSpec((MOE_TM, d), lambda e, i, c: (i, 0)),
                  pl.BlockSpec((None, N_EXPERTS, MOE_TM), lambda e, i, c: (i, 0, 0)),
                  pl.BlockSpec((None, d, dex), lambda e, i, c: (e, 0, 0), pipeline_mode=one),
                  pl.BlockSpec((None, d, dex), lambda e, i, c: (e, 0, 0), pipeline_mode=one),
                  pl.BlockSpec((None, dex, d), lambda e, i, c: (e, 0, 0), pipeline_mode=one)],
        out_specs=pl.BlockSpec((None, None, MOE_CAP, d), lambda e, i, c: (e, i, 0, 0)),
    )
    return pl.pallas_call(
        _expert_kernel,
        grid_spec=grid_spec,
        out_shape=jax.ShapeDtypeStruct((ne, nt, MOE_CAP, d), BF16),
        compiler_params=_cparams(("arbitrary", "arbitrary")),
        name="moe_experts",
    )(cnt, xb, selt, wg, wu, wd)


def _combine_kernel(x_ref, gates_ref, rank_ref, oc_ref, g_ref, b_ref, xo_ref):
    tm = x_ref.shape[0]
    gates = gates_ref[...]
    rank = rank_ref[...]
    lane = lax.broadcasted_iota(jnp.int32, (tm, MOE_CAP), 1).astype(F32)
    y = jnp.zeros((tm, D_MODEL), F32)
    for e in range(N_EXPERTS):
        gcol = gates[:, e:e + 1]
        scol = jnp.where(gcol > 0.0, rank[:, e:e + 1], -1.0)
        onehot = jnp.where(scol == lane, 1.0, 0.0).astype(BF16)
        y = y + _dot(onehot, oc_ref[e]) * gcol
    xo_ref[...] = _layer_norm(DN_ALPHA * x_ref[...] + y, g_ref[...], b_ref[...])


def _combine(x, gates, rank, oc, g, b):
    t, d = x.shape
    ne = oc.shape[0]
    rows = pl.BlockSpec((MOE_TM, d), lambda i: (i, 0))
    lrows = pl.BlockSpec((MOE_TM, LANES), lambda i: (i, 0))
    vec = pl.BlockSpec((1, d), lambda i: (0, 0))
    return pl.pallas_call(
        _combine_kernel,
        grid=(t // MOE_TM,),
        in_specs=[rows, lrows, lrows, pl.BlockSpec((ne, None, MOE_CAP, d), lambda i: (0, i, 0, 0)), vec, vec],
        out_specs=rows,
        out_shape=jax.ShapeDtypeStruct((t, d), F32),
        compiler_params=_cparams(("parallel",)),
        name="moe_combine",
    )(x, gates, rank, oc, g, b)


def _moe_kernel(xb_ref, x_ref, gates_ref, wg_ref, wu_ref, wd_ref, g_ref, b_ref, xo_ref, acc):
    e = pl.program_id(1)
    f = pl.program_id(2)

    @pl.when(jnp.logical_and(e == 0, f == 0))
    def _():
        acc[...] = jnp.zeros_like(acc)

    xb = xb_ref[...]
    gate = _col_of(gates_ref[...], e)
    gt = _dot(xb, wg_ref[...])
    up = _dot(xb, wu_ref[...])
    hh = (gt * jax.nn.sigmoid(gt) * up * gate).astype(BF16)
    acc[...] += _dot(hh, wd_ref[...])

    @pl.when(jnp.logical_and(e == pl.num_programs(1) - 1, f == pl.num_programs(2) - 1))
    def _():
        xo_ref[...] = _layer_norm(DN_ALPHA * x_ref[...] + acc[...], g_ref[...], b_ref[...])


def _moe(xb, x, gates, wg, wu, wd, g, b, tm=1024, tf=512):
    t, d = x.shape
    ne, _, dex = wg.shape
    rows = pl.BlockSpec((tm, d), lambda i, e, f: (i, 0))
    vec = pl.BlockSpec((1, d), lambda i, e, f: (0, 0))
    return pl.pallas_call(
        _moe_kernel,
        grid=(t // tm, ne, dex // tf),
        in_specs=[rows, rows, pl.BlockSpec((tm, LANES), lambda i, e, f: (i, 0)),
                  pl.BlockSpec((None, d, tf), lambda i, e, f: (e, 0, f)),
                  pl.BlockSpec((None, d, tf), lambda i, e, f: (e, 0, f)),
                  pl.BlockSpec((None, tf, d), lambda i, e, f: (e, f, 0)), vec, vec],
        out_specs=rows,
        out_shape=jax.ShapeDtypeStruct((t, d), F32),
        scratch_shapes=[pltpu.VMEM((tm, d), F32)],
        compiler_params=_cparams(("parallel", "arbitrary", "arbitrary")),
        name="moe_swiglu",
    )(xb, x, gates, wg, wu, wd, g, b)


def _rope_tables(positions):
    half = ROT_DIM // 2
    inv_freq = ROPE_THETA ** (-jnp.arange(0, ROT_DIM, 2, dtype=F32) / ROT_DIM)
    ang = positions.astype(F32)[..., None] * inv_freq
    cos, sin = jnp.cos(ang), jnp.sin(ang)
    shp = cos.shape[:-1]
    c = jnp.concatenate([cos, cos, jnp.ones(shp + (LANES - ROT_DIM,), F32)], axis=-1)
    s1 = jnp.concatenate([-sin, jnp.zeros(shp + (LANES - half,), F32)], axis=-1)
    s2 = jnp.concatenate([jnp.zeros(shp + (half,), F32), sin, jnp.zeros(shp + (LANES - ROT_DIM,), F32)], axis=-1)
    return c, s1, s2


def kernel(x, positions, w_in, conv_w, a_log, dt_bias, dn_norm_w, w_branch_a, w_branch_b, w_out, ln1_g, ln1_b,
           ffn_w_gate, ffn_w_up, ffn_w_down, router_w, moe_w_gate, moe_w_up, moe_w_down, ln2_g, ln2_b):
    bsz, seq, d = x.shape
    t = bsz * seq
    nh = B_HEADS
    qa_w = 3 * A_QKV_WIDTH
    o_qkvb = qa_w
    o_z = o_qkvb + 3 * B_WIDTH
    o_bd = o_z + B_WIDTH
    o_gates = o_bd + 2 * nh

    tabs = [tb.reshape(t, LANES) for tb in _rope_tables(positions)]

    dn_tt = 1024
    xf = x.reshape(t, d)
    xb = xf
    for layer in range(DEPTH):
        w = w_in[layer]
        w_qkva = w[:, :qa_w].astype(BF16)
        w_b = jnp.concatenate([w[:, o_qkvb:o_bd], w[:, o_gates:]], axis=1).astype(BF16)
        w_bd = jnp.pad(w[:, o_bd:o_gates], ((0, 0), (0, LANES - 2 * nh))).astype(BF16)

        zeros8 = jnp.zeros((nh,), F32)
        arow = jnp.concatenate([zeros8, -jnp.exp(a_log[layer].astype(F32)), jnp.zeros((LANES - 2 * nh,), F32)])[None, :]
        dtrow = jnp.concatenate([zeros8, dt_bias[layer].astype(F32), jnp.zeros((LANES - 2 * nh,), F32)])[None, :]
        subs = _proj_a(xb, w_qkva, tabs, bsz, seq)
        qn, kn, vn, zg, beta, gc, gct, a_p = _proj_b(xb, w_b, w_bd, arow, dtrow, conv_w[layer].astype(F32),
                                                      bsz, seq, PROJ_B_TM)

        ya = _attention([subs[p * N_GROUPS:(p + 1) * N_GROUPS] for p in range(3)], bsz, seq)

        sh3 = (bsz, seq, -1)
        t_p = _tri_solve(a_p.reshape((-1,) + a_p.shape[2:]))
        yb = _dn_main(qn.reshape(sh3), kn.reshape(sh3), vn.reshape(sh3), beta.reshape(sh3), gc.reshape(sh3), gct,
                      t_p, zg.reshape(sh3), 0, dn_norm_w[layer].astype(F32)[None, :], dn_tt)

        xf, xb = _mix_out(ya.reshape(t, A_GROUP_WIDTH), yb.reshape(t, B_WIDTH), zg, B_WIDTH, xf,
                          w_branch_a[layer].astype(BF16), w_branch_b[layer].astype(BF16), w_out[layer].astype(BF16),
                          ln1_g[layer].astype(F32)[None, :], ln1_b[layer].astype(F32)[None, :])

        g2 = ln2_g[layer].astype(F32)[None, :]
        b2 = ln2_b[layer].astype(F32)[None, :]
        if layer % 2 == 0:
            i = layer // 2
            xf, xb = _ffn(xb, xf, ffn_w_gate[i].astype(BF16), ffn_w_up[i].astype(BF16), ffn_w_down[i].astype(BF16), g2, b2)
        else:
            i = layer // 2
            rw = jnp.pad(router_w[i].astype(F32), ((0, 0), (0, LANES - N_EXPERTS)))
            gates, rank, selt, cnt = _router(xf, rw, MOE_TM)
            wg = moe_w_gate[i].astype(BF16)
            wu = moe_w_up[i].astype(BF16)
            wd = moe_w_down[i].astype(BF16)

            cnt_flat = cnt[:, 0, :N_EXPERTS].reshape(-1)

            def routed(xb, xf, gates, rank, selt, cnt_flat):
                return _combine(xf, gates, rank, _experts(cnt_flat, xb, selt, wg, wu, wd), g2, b2)

            def dense(xb, xf, gates, rank, selt, cnt_flat):
                return _moe(xb, xf, gates, wg, wu, wd, g2, b2)

            xf = lax.cond(jnp.max(cnt_flat) > MOE_CAP, dense, routed, xb, xf, gates, rank, selt, cnt_flat)
    return xf.reshape(bsz, seq, d)
```
